```python
import math
import jax, jax.numpy as jnp
from jax import lax
import numpy as np

D_MODEL = 2048
BATCH = 2
SEQ = 4096
DEPTH = 1
DEC_BATCH = 32
DEC_SEQ = 4
PAST_LEN = 16384
PAGE_SIZE = 128

N_HEADS = 8
N_KV_HEADS = 4
Q_PER_KV = N_HEADS // N_KV_HEADS
HEAD_DIM = 64
ROT_DIM = HEAD_DIM // 4
ROPE_THETA = 500000.0
Q_BLOCK = 128
ATT_WIDTH = N_HEADS * 2 * HEAD_DIM
KV_WIDTH = N_KV_HEADS * 2 * HEAD_DIM
LRU_WIDTH = 1024
LRU_BLOCKS = 8
LRU_BLOCK_W = LRU_WIDTH // LRU_BLOCKS
CONV_W = 4
LRU_C = 8.0
N_KEYS = 128
N_EXPERTS = N_KEYS * N_KEYS
PEER_HEADS = 8
PEER_TOPK = 16
PEER_KEY_DIM = 256
PEER_HALF = PEER_KEY_DIM // 2
PEER_POS_BLOCK = 128
RMS_EPS = 1e-6
OFF_Q = ATT_WIDTH
OFF_K = OFF_Q + KV_WIDTH
OFF_V = OFF_K + KV_WIDTH
OFF_L = OFF_V + LRU_WIDTH
OFF_GA = OFF_L + D_MODEL
IN_WIDTH = OFF_GA + D_MODEL

kernel_name = 'hybrid_diffattn_rglru_peer_step'

F32 = jnp.float32


def rms_norm(x, g):
    xf = x.astype(F32)
    var = jnp.mean(xf * xf, axis=-1, keepdims=True)
    return (xf * lax.rsqrt(var + RMS_EPS) * g.astype(F32)).astype(x.dtype)


def partial_rope(t, pos):
    half = ROT_DIM // 2
    inv_freq = jnp.float32(ROPE_THETA) ** (-jnp.arange(half, dtype=F32) * 2.0 / ROT_DIM)
    ang = pos.astype(F32)[:, None] * inv_freq[None, :]
    cos = jnp.cos(ang)[None, :, None, None, :]
    sin = jnp.sin(ang)[None, :, None, None, :]
    tr = t[..., :ROT_DIM].astype(F32)
    t1, t2 = tr[..., :half], tr[..., half:]
    rot = jnp.concatenate([t1 * cos - t2 * sin, t2 * cos + t1 * sin], axis=-1).astype(t.dtype)
    return jnp.concatenate([rot, t[..., ROT_DIM:]], axis=-1)


def _project(xn, pos, w_in):
    b, t, _ = xn.shape
    proj = xn @ w_in
    q, k, v, xl, ga, gl = jnp.split(proj, [OFF_Q, OFF_K, OFF_V, OFF_L, OFF_GA], axis=-1)
    q = partial_rope(q.reshape(b, t, N_HEADS, 2, HEAD_DIM), pos)
    q = q.reshape(b, t, N_KV_HEADS, Q_PER_KV, 2, HEAD_DIM)
    k = partial_rope(k.reshape(b, t, N_KV_HEADS, 2, HEAD_DIM), pos)
    v = v.reshape(b, t, N_KV_HEADS, 2 * HEAD_DIM)
    return q, k, v, xl, ga, gl


def _diff_attend(q, k, v, mask, lam):
    s = jnp.einsum('bqgrcd,bkgcd->bgrcqk', q.astype(F32), k.astype(F32)) * (HEAD_DIM ** -0.5)
    s = jnp.where(mask, s, -jnp.inf)
    p = jax.nn.softmax(s, axis=-1)
    w = p[:, :, :, 0] - lam * p[:, :, :, 1]
    return jnp.einsum('bgrqk,bkgv->bqgrv', w, v.astype(F32))


def _attn_out(att, p, lam_init, dtype):
    b, t = att.shape[0], att.shape[1]
    att = rms_norm(att, p['subln_g']).astype(F32) * (1.0 - lam_init)
    return att.reshape(b, t, ATT_WIDTH).astype(dtype)


def _causal_conv(x_ext, w, b):
    t = x_ext.shape[1] - (CONV_W - 1)
    out = x_ext[:, 0:t] * w[0]
    for j in range(1, CONV_W):
        out = out + x_ext[:, j:j + t] * w[j]
    return out + b


def _lru_branch(xl_ext, h0, p):
    xc = _causal_conv(xl_ext, p['conv_w'], p['conv_b'])
    b, t, c = xc.shape
    xb = xc.reshape(b, t, LRU_BLOCKS, LRU_BLOCK_W)
    r = jax.nn.sigmoid(jnp.einsum('btni,nij->btnj', xb, p['lru_wa']).reshape(b, t, c) + p['lru_ba'])
    i = jax.nn.sigmoid(jnp.einsum('btni,nij->btnj', xb, p['lru_wx']).reshape(b, t, c) + p['lru_bx'])
    log_a = -LRU_C * r.astype(F32) * jax.nn.softplus(-p['lru_lambda'].astype(F32))
    a = jnp.exp(log_a)
    u = jnp.sqrt(-jnp.expm1(2.0 * log_a)) * (i * xc).astype(F32)

    def step(h, au):
        a_t, u_t = au
        h = a_t * h + u_t
        return h, h

    h_t, hs = lax.scan(step, h0.astype(F32), (jnp.swapaxes(a, 0, 1), jnp.swapaxes(u, 0, 1)))
    conv_state = xl_ext[:, -(CONV_W - 1):]
    return jnp.swapaxes(hs, 0, 1).astype(xc.dtype), conv_state, h_t.astype(h0.dtype)


def _merge(att, lru, ga, gl, p):
    m = jax.nn.sigmoid(ga) * (att @ p['w_att_up']) + jax.nn.sigmoid(gl) * (lru @ p['w_lru_up'])
    return m @ p['w_out']


def _peer(xt, p):
    t = xt.shape[0]
    q = (xt @ p['peer_wq']).astype(F32).reshape(t, PEER_HEADS, 2, PEER_HALF)
    s1 = jnp.einsum('thd,nd->thn', q[:, :, 0], p['peer_k1'].astype(F32))
    s2 = jnp.einsum('thd,nd->thn', q[:, :, 1], p['peer_k2'].astype(F32))
    v1, i1 = lax.top_k(s1, PEER_TOPK)
    v2, i2 = lax.top_k(s2, PEER_TOPK)
    cand = (v1[..., :, None] + v2[..., None, :]).reshape(t, PEER_HEADS, PEER_TOPK * PEER_TOPK)
    cid = (i1[..., :, None] * N_KEYS + i2[..., None, :]).reshape(t, PEER_HEADS, PEER_TOPK * PEER_TOPK)
    best, sel = lax.top_k(cand, PEER_TOPK)
    eid = jnp.take_along_axis(cid, sel, axis=-1).reshape(t, PEER_HEADS * PEER_TOPK)
    gate = jax.nn.softmax(best, axis=-1).reshape(t, PEER_HEADS * PEER_TOPK)
    u = p['peer_u'][eid]
    act = jax.nn.gelu(jnp.einsum('tkd,td->tk', u, xt).astype(F32), approximate=False)
    coef = (gate * act).astype(xt.dtype)
    return jnp.einsum('tk,tkd->td', coef, p['peer_v'][eid])


def _layer_prompt(x, p, lam, lam_init):
    b, t, _ = x.shape
    xn = rms_norm(x, p['norm1_g'])
    pos = jnp.arange(t, dtype=jnp.int32)
    q, k, v, xl, ga, gl = _project(xn, pos, p['w_in'])
    nb = t // Q_BLOCK
    qb = jnp.moveaxis(q.reshape(b, nb, Q_BLOCK, N_KV_HEADS, Q_PER_KV, 2, HEAD_DIM), 1, 0)
    kpos = jnp.arange(t, dtype=jnp.int32)

    def block(args):
        q_blk, bi = args
        qpos = bi * Q_BLOCK + jnp.arange(Q_BLOCK, dtype=jnp.int32)
        return _diff_attend(q_blk, k, v, kpos[None, :] <= qpos[:, None], lam)

    att = lax.map(block, (qb, jnp.arange(nb, dtype=jnp.int32)))
    att = jnp.moveaxis(att, 0, 1).reshape(b, t, N_HEADS, 2 * HEAD_DIM)
    att = _attn_out(att, p, lam_init, x.dtype)
    xl_ext = jnp.concatenate([jnp.zeros((b, CONV_W - 1, LRU_WIDTH), xl.dtype), xl], axis=1)
    h0 = jnp.zeros((b, LRU_WIDTH), x.dtype)
    lru, conv_state, h_t = _lru_branch(xl_ext, h0, p)
    h = x + _merge(att, lru, ga, gl, p)
    xn2 = rms_norm(h, p['norm2_g'])
    nbp = t // PEER_POS_BLOCK
    xb = xn2.reshape(b, nbp, PEER_POS_BLOCK, D_MODEL).transpose(1, 0, 2, 3).reshape(nbp, b * PEER_POS_BLOCK, D_MODEL)
    ffn = lax.map(lambda z: _peer(z, p), xb)
    ffn = ffn.reshape(nbp, b, PEER_POS_BLOCK, D_MODEL).transpose(1, 0, 2, 3).reshape(b, t, D_MODEL)
    k_rows = k.reshape(b, t, N_KV_HEADS, 2 * HEAD_DIM)
    return h + ffn, k_rows, v, conv_state, h_t


def _layer_sample(x, k_past, v_past, conv_buf, h0, p, lam, lam_init):
    b, t, _ = x.shape
    xn = rms_norm(x, p['norm1_g'])
    pos = PAST_LEN + jnp.arange(t, dtype=jnp.int32)
    q, k, v, xl, ga, gl = _project(xn, pos, p['w_in'])
    k_all = jnp.concatenate([k_past.reshape(b, PAST_LEN, N_KV_HEADS, 2, HEAD_DIM).astype(k.dtype), k], axis=1)
    v_all = jnp.concatenate([v_past.astype(v.dtype), v], axis=1)
    kpos = jnp.arange(PAST_LEN + t, dtype=jnp.int32)
    att = _diff_attend(q, k_all, v_all, kpos[None, :] <= pos[:, None], lam)
    att = _attn_out(att.reshape(b, t, N_HEADS, 2 * HEAD_DIM), p, lam_init, x.dtype)
    xl_ext = jnp.concatenate([conv_buf.astype(xl.dtype), xl], axis=1)
    lru, conv_state, h_t = _lru_branch(xl_ext, h0, p)
    h = x + _merge(att, lru, ga, gl, p)
    xn2 = rms_norm(h, p['norm2_g'])
    ffn = _peer(xn2.reshape(b * t, D_MODEL), p).reshape(b, t, D_MODEL)
    k_rows = k.reshape(b, t, N_KV_HEADS, 2 * HEAD_DIM)
    return h + ffn, k_rows, v, conv_state, h_t


def setup_inputs(seed: int = 0) -> dict:
    key = jax.random.key(seed)
    ks = jax.random.split(key, 32)
    n_pages = PAST_LEN // PAGE_SIZE
    n_used = DEC_BATCH * n_pages
    n_pool = (n_used * 5 + 3) // 4
    nrm = lambda k, shape, s: jax.random.normal(k, shape, F32) * s
    a0 = jax.random.uniform(ks[13], (DEPTH, LRU_WIDTH), F32, minval=0.9, maxval=0.999)
    s0 = a0 ** (1.0 / LRU_C)
    page_table = jax.random.permutation(ks[6], n_pool)[:n_used].reshape(DEC_BATCH, n_pages).astype(jnp.int32)
    return {
        'x_prompt': nrm(ks[0], (BATCH, SEQ, D_MODEL), 1.0),
        'x_sample': nrm(ks[1], (DEC_BATCH, DEC_SEQ, D_MODEL), 1.0),
        'cache_k': nrm(ks[2], (DEPTH, n_pool, PAGE_SIZE, N_KV_HEADS, 2 * HEAD_DIM), 1.0),
        'cache_v': nrm(ks[3], (DEPTH, n_pool, PAGE_SIZE, N_KV_HEADS, 2 * HEAD_DIM), 1.0),
        'state_conv': nrm(ks[4], (DEPTH, DEC_BATCH, CONV_W - 1, LRU_WIDTH), 1.0),
        'state_h': nrm(ks[5], (DEPTH, DEC_BATCH, LRU_WIDTH), 0.5),
        'page_table': page_table,
        'norm1_g': 1.0 + nrm(ks[7], (DEPTH, D_MODEL), 0.02),
        'w_in': nrm(ks[8], (DEPTH, D_MODEL, IN_WIDTH), D_MODEL ** -0.5),
        'lambda_q1': nrm(ks[9], (DEPTH, HEAD_DIM), 0.1),
        'lambda_k1': nrm(ks[10], (DEPTH, HEAD_DIM), 0.1),
        'lambda_q2': nrm(ks[11], (DEPTH, HEAD_DIM), 0.1),
        'lambda_k2': nrm(ks[12], (DEPTH, HEAD_DIM), 0.1),
        'subln_g': 1.0 + nrm(ks[14], (DEPTH, 2 * HEAD_DIM), 0.02),
        'conv_w': nrm(ks[15], (DEPTH, CONV_W, LRU_WIDTH), CONV_W ** -0.5),
        'conv_b': nrm(ks[16], (DEPTH, LRU_WIDTH), 0.01),
        'lru_wa': nrm(ks[17], (DEPTH, LRU_BLOCKS, LRU_BLOCK_W, LRU_BLOCK_W), LRU_BLOCK_W ** -0.5),
        'lru_ba': nrm(ks[18], (DEPTH, LRU_WIDTH), 0.01),
        'lru_wx': nrm(ks[19], (DEPTH, LRU_BLOCKS, LRU_BLOCK_W, LRU_BLOCK_W), LRU_BLOCK_W ** -0.5),
        'lru_bx': nrm(ks[20], (DEPTH, LRU_WIDTH), 0.01),
        'lru_lambda': jnp.log(s0) - jnp.log1p(-s0),
        'w_att_up': nrm(ks[21], (DEPTH, ATT_WIDTH, D_MODEL), ATT_WIDTH ** -0.5),
        'w_lru_up': nrm(ks[22], (DEPTH, LRU_WIDTH, D_MODEL), LRU_WIDTH ** -0.5),
        'w_out': nrm(ks[23], (DEPTH, D_MODEL, D_MODEL), D_MODEL ** -0.5),
        'norm2_g': 1.0 + nrm(ks[24], (DEPTH, D_MODEL), 0.02),
        'peer_wq': nrm(ks[25], (DEPTH, D_MODEL, PEER_HEADS * PEER_KEY_DIM), D_MODEL ** -0.5),
        'peer_k1': nrm(ks[26], (DEPTH, N_KEYS, PEER_HALF), PEER_HALF ** -0.5),
        'peer_k2': nrm(ks[27], (DEPTH, N_KEYS, PEER_HALF), PEER_HALF ** -0.5),
        'peer_u': nrm(ks[28], (DEPTH, N_EXPERTS, D_MODEL), D_MODEL ** -0.5),
        'peer_v': nrm(ks[29], (DEPTH, N_EXPERTS, D_MODEL), PEER_HEADS ** -0.5),
        'final_g': 1.0 + nrm(ks[30], (D_MODEL,), 0.02),
    }


def reference(x_prompt, x_sample, cache_k, cache_v, state_conv, state_h, page_table,
              norm1_g, w_in, lambda_q1, lambda_k1, lambda_q2, lambda_k2, subln_g,
              conv_w, conv_b, lru_wa, lru_ba, lru_wx, lru_bx, lru_lambda,
              w_att_up, w_lru_up, w_out, norm2_g, peer_wq, peer_k1, peer_k2, peer_u, peer_v, final_g):
    hp, hs = x_prompt, x_sample
    kp_l, vp_l, cp_l, hp_l, ks_l, vs_l, cs_l, hs_l = [], [], [], [], [], [], [], []
    for l in range(DEPTH):
        p = {
            'norm1_g': norm1_g[l], 'w_in': w_in[l], 'subln_g': subln_g[l],
            'conv_w': conv_w[l], 'conv_b': conv_b[l], 'lru_wa': lru_wa[l], 'lru_ba': lru_ba[l],
            'lru_wx': lru_wx[l], 'lru_bx': lru_bx[l], 'lru_lambda': lru_lambda[l],
            'w_att_up': w_att_up[l], 'w_lru_up': w_lru_up[l], 'w_out': w_out[l], 'norm2_g': norm2_g[l],
            'peer_wq': peer_wq[l], 'peer_k1': peer_k1[l], 'peer_k2': peer_k2[l],
            'peer_u': peer_u[l], 'peer_v': peer_v[l],
        }
        lam_init = 0.8 - 0.6 * math.exp(-0.3 * l)
        lam = (jnp.exp(jnp.sum(lambda_q1[l].astype(F32) * lambda_k1[l].astype(F32)))
               - jnp.exp(jnp.sum(lambda_q2[l].astype(F32) * lambda_k2[l].astype(F32))) + lam_init)
        hp, kp, vp, cp, hpt = _layer_prompt(hp, p, lam, lam_init)
        k_past = cache_k[l, page_table]
        v_past = cache_v[l, page_table]
        b = x_sample.shape[0]
        k_past = k_past.reshape(b, PAST_LEN, N_KV_HEADS, 2 * HEAD_DIM)
        v_past = v_past.reshape(b, PAST_LEN, N_KV_HEADS, 2 * HEAD_DIM)
        hs, kn, vn, cn, hst = _layer_sample(hs, k_past, v_past, state_conv[l], state_h[l], p, lam, lam_init)
        kp_l.append(kp); vp_l.append(vp); cp_l.append(cp); hp_l.append(hpt)
        ks_l.append(kn); vs_l.append(vn); cs_l.append(cn); hs_l.append(hst)
    y_prompt = rms_norm(hp, final_g)
    y_sample = rms_norm(hs, final_g)
    return (y_prompt, y_sample,
            jnp.stack(kp_l), jnp.stack(vp_l), jnp.stack(cp_l), jnp.stack(hp_l),
            jnp.stack(ks_l), jnp.stack(vs_l), jnp.stack(cs_l), jnp.stack(hs_l))
```

```python
import functools
import math

import jax
import jax.numpy as jnp
from jax import lax
from jax.experimental import pallas as pl
from jax.experimental.pallas import tpu as pltpu

F32 = jnp.float32
BF16 = jnp.bfloat16

D_MODEL = 2048
N_HEADS = 8
N_KV_HEADS = 4
HEAD_DIM = 64
HEAD_W = 2 * HEAD_DIM
ROT_DIM = HEAD_DIM // 4
ROPE_THETA = 500000.0
ATT_WIDTH = N_HEADS * HEAD_W
KV_WIDTH = N_KV_HEADS * HEAD_W
LRU_WIDTH = 1024
LRU_BLOCKS = 8
LRU_BLOCK_W = LRU_WIDTH // LRU_BLOCKS
CONV_W = 4
LRU_C = 8.0
N_KEYS = 128
KEY_BITS = 7
N_EXPERTS = N_KEYS * N_KEYS
PEER_HEADS = 8
PEER_TOPK = 16
PEER_SEL = PEER_HEADS * PEER_TOPK
PAGE_SIZE = 128
RMS_EPS = 1e-6
OFF_Q = ATT_WIDTH
OFF_K = OFF_Q + KV_WIDTH
OFF_V = OFF_K + KV_WIDTH
OFF_L = OFF_V + LRU_WIDTH
OFF_GA = OFF_L + D_MODEL
IN_WIDTH = OFF_GA + D_MODEL
LAM_INIT = 0.8 - 0.6 * math.exp(0.0)

LANES = 128
SUBLANES = 8
VMEM_LIMIT_BYTES = 56 * 1024 * 1024

NEG_INF = float("-inf")
NT_DIMS = (((1,), (1,)), ((), ()))


def _pick_block(total, cap, quantum):
  best = None
  b = quantum
  while b <= min(cap, total):
    if total % b == 0:
      best = b
    b += quantum
  assert best is not None, (total, cap, quantum)
  return best


def _params(sem, vmem=VMEM_LIMIT_BYTES):
  return pltpu.CompilerParams(dimension_semantics=sem, vmem_limit_bytes=vmem)


def _rms(x, g):
  var = jnp.mean(x * x, axis=-1, keepdims=True)
  return x * lax.rsqrt(var + RMS_EPS) * g


def _inproj_body(x_ref, g_ref, w_ref, cos_ref, sa_ref, sb_ref, o_ref, xn_ref, *, tn, n_q_blocks, n_rope_blocks):
  j = pl.program_id(1)

  @pl.when(j == 0)
  def _():
    xn_ref[...] = _rms(x_ref[...], g_ref[...]).astype(BF16)

  acc = jnp.dot(xn_ref[...], w_ref[...], preferred_element_type=F32)

  @pl.when(j < n_rope_blocks)
  def _():
    reps = tn // LANES
    cos = jnp.concatenate([cos_ref[...]] * reps, axis=1)
    sa = jnp.concatenate([sa_ref[...]] * reps, axis=1)
    sb = jnp.concatenate([sb_ref[...]] * reps, axis=1)
    half = ROT_DIM // 2
    rot = acc * cos + pltpu.roll(acc, tn - half, 1) * sa + pltpu.roll(acc, half, 1) * sb
    scale = jnp.where(j < n_q_blocks, HEAD_DIM ** -0.5, 1.0).astype(F32)
    o_ref[...] = rot * scale

  @pl.when(j >= n_rope_blocks)
  def _():
    o_ref[...] = acc


def _inproj(x_all, norm_g, w_in_bf, cos_t, sa_t, sb_t):
  t = x_all.shape[0]
  tm = _pick_block(t, 640, LANES)
  tn = 512
  body = functools.partial(_inproj_body, tn=tn, n_q_blocks=OFF_Q // tn, n_rope_blocks=OFF_K // tn)
  return pl.pallas_call(
      body,
      grid=(t // tm, IN_WIDTH // tn),
      in_specs=[
          pl.BlockSpec((tm, D_MODEL), lambda i, j: (i, 0)),
          pl.BlockSpec((1, D_MODEL), lambda i, j: (0, 0)),
          pl.BlockSpec((D_MODEL, tn), lambda i, j: (0, j)),
          pl.BlockSpec((tm, LANES), lambda i, j: (i, 0)),
          pl.BlockSpec((tm, LANES), lambda i, j: (i, 0)),
          pl.BlockSpec((tm, LANES), lambda i, j: (i, 0)),
      ],
      out_specs=pl.BlockSpec((tm, tn), lambda i, j: (i, j)),
      out_shape=jax.ShapeDtypeStruct((t, IN_WIDTH), F32),
      scratch_shapes=[pltpu.VMEM((tm, D_MODEL), BF16)],
      compiler_params=_params(("parallel", "arbitrary")),
      name="inproj",
  )(x_all, norm_g, w_in_bf, cos_t, sa_t, sb_t)


def _subln(o, g):
  var = jnp.mean(o * o, axis=-1, keepdims=True)
  return o * lax.rsqrt(var + RMS_EPS) * g * (1.0 - LAM_INIT)


def _pattn_body(lam_ref, q_ref, k_ref, v_ref, g_ref, o_ref, q4_ref, m_ref, l_ref, acc_ref, *, tq, tk):
  qi = pl.program_id(2)
  ki = pl.program_id(3)
  rows = 2 * 2 * tq

  @pl.when(ki == 0)
  def _():
    m_ref[...] = jnp.full((rows, 1), NEG_INF, F32)
    l_ref[...] = jnp.zeros((rows, 1), F32)
    acc_ref[...] = jnp.zeros((rows, HEAD_W), F32)
    lane = lax.broadcasted_iota(jnp.int32, (tq, HEAD_W), 1)
    for r in range(2):
      qh = q_ref[:, r * HEAD_W:(r + 1) * HEAD_W]
      q4_ref[(2 * r) * tq:(2 * r + 1) * tq, :] = jnp.where(lane < HEAD_DIM, qh, 0.0).astype(BF16)
      q4_ref[(2 * r + 1) * tq:(2 * r + 2) * tq, :] = jnp.where(lane >= HEAD_DIM, qh, 0.0).astype(BF16)

  @pl.when(ki <= qi)
  def _():
    k = k_ref[...].astype(BF16)
    v = v_ref[...].astype(BF16)
    s = lax.dot_general(q4_ref[...], k, NT_DIMS, preferred_element_type=F32)
    qpos = qi * tq + lax.broadcasted_iota(jnp.int32, (tq, tk), 0)
    kpos = ki * tk + lax.broadcasted_iota(jnp.int32, (tq, tk), 1)
    s = jnp.where((kpos <= qpos)[None], s.reshape(4, tq, tk), NEG_INF).reshape(rows, tk)
    m_old = m_ref[...]
    m_new = jnp.maximum(m_old, jnp.max(s, axis=1, keepdims=True))
    alpha = jnp.exp(m_old - m_new)
    p = jnp.exp(s - m_new)
    l_ref[...] = alpha * l_ref[...] + jnp.sum(p, axis=1, keepdims=True)
    acc_ref[...] = alpha * acc_ref[...] + jnp.dot(p.astype(BF16), v, preferred_element_type=F32)
    m_ref[...] = m_new

  @pl.when(ki == qi)
  def _():
    lam = lam_ref[0]
    o = acc_ref[...] / l_ref[...]
    for r in range(2):
      d = o[(2 * r) * tq:(2 * r + 1) * tq] - lam * o[(2 * r + 1) * tq:(2 * r + 2) * tq]
      o_ref[:, r * HEAD_W:(r + 1) * HEAD_W] = _subln(d, g_ref[...]).astype(o_ref.dtype)


def _prompt_attention(proj, lam, subln_g, batch, seq):
  tq = tk = _pick_block(seq, 512, LANES)
  nq = seq // tq
  gw = 2 * HEAD_W
  body = functools.partial(_pattn_body, tq=tq, tk=tk)
  return pl.pallas_call(
      body,
      grid=(batch, N_KV_HEADS, nq, nq),
      in_specs=[
          pl.BlockSpec(memory_space=pltpu.SMEM),
          pl.BlockSpec((tq, gw), lambda b, g, qi, ki: (b * nq + qi, g)),
          pl.BlockSpec((tk, HEAD_W), lambda b, g, qi, ki: (b * nq + jnp.minimum(ki, qi), OFF_Q // HEAD_W + g)),
          pl.BlockSpec((tk, HEAD_W), lambda b, g, qi, ki: (b * nq + jnp.minimum(ki, qi), OFF_K // HEAD_W + g)),
          pl.BlockSpec((1, HEAD_W), lambda b, g, qi, ki: (0, 0)),
      ],
      out_specs=pl.BlockSpec((tq, gw), lambda b, g, qi, ki: (b * nq + qi, g)),
      out_shape=jax.ShapeDtypeStruct((batch * seq, ATT_WIDTH), BF16),
      scratch_shapes=[
          pltpu.VMEM((4 * tq, HEAD_W), BF16),
          pltpu.VMEM((4 * tq, 1), F32),
          pltpu.VMEM((4 * tq, 1), F32),
          pltpu.VMEM((4 * tq, HEAD_W), F32),
      ],
      compiler_params=_params(("parallel", "parallel", "parallel", "arbitrary")),
      name="prompt_attention",
  )(lam, proj, proj, proj, subln_g)


PAGES_PER_CHUNK = 8
NEW_ROWS = 16


def _sattn_body(pt_ref, lam_ref, wq_ref, kn_ref, vn_ref, g_ref, ck_ref, cv_ref, o_ref,
                kbuf, vbuf, sem, *, n_chunks, dec_seq):
  b = pl.program_id(0)
  nb = pl.num_programs(0)
  rows = wq_ref.shape[1]
  chunk_keys = PAGES_PER_CHUNK * PAGE_SIZE

  def copies(bb, c, slot):
    out = []
    for p in range(PAGES_PER_CHUNK):
      page = pt_ref[bb, c * PAGES_PER_CHUNK + p]
      out.append(pltpu.make_async_copy(ck_ref.at[page], kbuf.at[slot, p], sem.at[0, slot]))
      out.append(pltpu.make_async_copy(cv_ref.at[page], vbuf.at[slot, p], sem.at[1, slot]))
    return out

  def start(bb, c, slot):
    for cp in copies(bb, c, slot):
      cp.start()

  @pl.when(b == 0)
  def _():
    start(0, 0, 0)

  wq = wq_ref[0]

  def chunk(c, carry):
    m_old, l_old, acc = carry
    slot = c % 2

    @pl.when(c + 1 < n_chunks)
    def _():
      start(b, c + 1, 1 - slot)

    @pl.when(jnp.logical_and(c + 1 == n_chunks, b + 1 < nb))
    def _():
      start(b + 1, 0, 1 - slot)

    for cp in copies(b, c, slot):
      cp.wait()
    kc = kbuf[slot].reshape(chunk_keys, KV_WIDTH).astype(BF16)
    vc = vbuf[slot].reshape(chunk_keys, KV_WIDTH).astype(BF16)
    s = lax.dot_general(wq, kc, NT_DIMS, preferred_element_type=F32)
    m_new = jnp.maximum(m_old, jnp.max(s, axis=1, keepdims=True))
    alpha = jnp.exp(m_old - m_new)
    p = jnp.exp(s - m_new)
    l_new = alpha * l_old + jnp.sum(p, axis=1, keepdims=True)
    acc = alpha * acc + jnp.dot(p.astype(BF16), vc, preferred_element_type=F32)
    return m_new, l_new, acc

  init = (jnp.full((rows, 1), NEG_INF, F32), jnp.zeros((rows, 1), F32), jnp.zeros((rows, KV_WIDTH), F32))
  m_old, l_old, acc = lax.fori_loop(0, n_chunks, chunk, init)

  s = lax.dot_general(wq, kn_ref[0], NT_DIMS, preferred_element_type=F32)
  t_of_row = lax.broadcasted_iota(jnp.int32, (rows, NEW_ROWS), 0) % dec_seq
  j_of_col = lax.broadcasted_iota(jnp.int32, (rows, NEW_ROWS), 1)
  s = jnp.where(j_of_col <= t_of_row, s, NEG_INF)
  m_new = jnp.maximum(m_old, jnp.max(s, axis=1, keepdims=True))
  alpha = jnp.exp(m_old - m_new)
  p = jnp.exp(s - m_new)
  l_new = alpha * l_old + jnp.sum(p, axis=1, keepdims=True)
  acc = alpha * acc + jnp.dot(p.astype(BF16), vn_ref[0], preferred_element_type=F32)

  o = acc / l_new
  lam = lam_ref[0]
  gr = rows // N_KV_HEADS
  for g in range(N_KV_HEADS):
    blk = o[g * gr:(g + 1) * gr, g * HEAD_W:(g + 1) * HEAD_W]
    d = blk[:gr // 2] - lam * blk[gr // 2:]
    o_ref[0, g * (gr // 2):(g + 1) * (gr // 2), :] = _subln(d, g_ref[...])


def _sample_attention(page_table, lam, wq, k_new, v_new, subln_g, cache_k, cache_v, dec_seq):
  dec_batch, n_pages = page_table.shape
  assert n_pages % (2 * PAGES_PER_CHUNK) == 0
  n_chunks = n_pages // PAGES_PER_CHUNK
  rows = wq.shape[1]
  body = functools.partial(_sattn_body, n_chunks=n_chunks, dec_seq=dec_seq)
  grid_spec = pltpu.PrefetchScalarGridSpec(
      num_scalar_prefetch=1,
      grid=(dec_batch,),
      in_specs=[
          pl.BlockSpec(memory_space=pltpu.SMEM),
          pl.BlockSpec((1, rows, KV_WIDTH), lambda b, pt: (b, 0, 0)),
          pl.BlockSpec((1, NEW_ROWS, KV_WIDTH), lambda b, pt: (b, 0, 0)),
          pl.BlockSpec((1, NEW_ROWS, KV_WIDTH), lambda b, pt: (b, 0, 0)),
          pl.BlockSpec((1, HEAD_W), lambda b, pt: (0, 0)),
          pl.BlockSpec(memory_space=pl.ANY),
          pl.BlockSpec(memory_space=pl.ANY),
      ],
      out_specs=pl.BlockSpec((1, rows // 2, HEAD_W), lambda b, pt: (b, 0, 0)),
      scratch_shapes=[
          pltpu.VMEM((2, PAGES_PER_CHUNK, PAGE_SIZE, KV_WIDTH), F32),
          pltpu.VMEM((2, PAGES_PER_CHUNK, PAGE_SIZE, KV_WIDTH), F32),
          pltpu.SemaphoreType.DMA((2, 2)),
      ],
  )
  return pl.pallas_call(
      body,
      grid_spec=grid_spec,
      out_shape=jax.ShapeDtypeStruct((dec_batch, rows // 2, HEAD_W), F32),
      compiler_params=_params(("arbitrary",)),
      name="sample_attention",
  )(page_table, lam, wq, k_new, v_new, subln_g, cache_k, cache_v)


def _lru_gates(xc, wa_ref, ba, wx_ref, bx, lam):
  ra, ix = [], []
  for n in range(LRU_BLOCKS):
    xb = xc[:, n * LRU_BLOCK_W:(n + 1) * LRU_BLOCK_W]
    ra.append(jnp.dot(xb, wa_ref[n], preferred_element_type=F32, precision=lax.Precision.HIGHEST))
    ix.append(jnp.dot(xb, wx_ref[n], preferred_element_type=F32, precision=lax.Precision.HIGHEST))
  r = jax.nn.sigmoid(jnp.concatenate(ra, axis=1) + ba)
  i = jax.nn.sigmoid(jnp.concatenate(ix, axis=1) + bx)
  neg = -lam
  softplus = jnp.maximum(neg, 0.0) + jnp.log1p(jnp.exp(-jnp.abs(neg)))
  log_a = -LRU_C * r * softplus
  a = jnp.exp(log_a)
  u = jnp.sqrt(1.0 - jnp.exp(2.0 * log_a)) * (i * xc)
  return a, u


def _conv(rows_of, cw_ref, cb):
  out = rows_of(0) * cw_ref[0:1, :]
  for j in range(1, CONV_W):
    out = out + rows_of(j) * cw_ref[j:j + 1, :]
  return out + cb


def _lru_prompt_body(xl_ref, cw_ref, cb_ref, wa_ref, ba_ref, wx_ref, bx_ref, lam_ref, y_ref, hl_ref,
                     ext_ref, a_ref, u_ref, hs_ref, h_ref, *, tt):
  ti = pl.program_id(1)
  head = SUBLANES

  @pl.when(ti == 0)
  def _():
    ext_ref[0:head, :] = jnp.zeros((head, LRU_WIDTH), F32)
    h_ref[...] = jnp.zeros((1, LRU_WIDTH), F32)

  ext_ref[head:head + tt, :] = xl_ref[...]
  xc = _conv(lambda j: ext_ref[head - (CONV_W - 1) + j:head - (CONV_W - 1) + j + tt, :], cw_ref, cb_ref[...])
  a, u = _lru_gates(xc, wa_ref, ba_ref[...], wx_ref, bx_ref[...], lam_ref[...])
  a_ref[...] = a
  u_ref[...] = u

  def step(t, h):
    h = a_ref[pl.ds(t, 1), :] * h + u_ref[pl.ds(t, 1), :]
    hs_ref[pl.ds(t, 1), :] = h
    return h

  h_fin = lax.fori_loop(0, tt, step, h_ref[...], unroll=8)
  h_ref[...] = h_fin
  y_ref[...] = hs_ref[...].astype(y_ref.dtype)
  ext_ref[0:head, :] = ext_ref[tt:tt + head, :]

  @pl.when(ti == pl.num_programs(1) - 1)
  def _():
    hl_ref[0] = h_fin


def _lru_weight_specs(nidx):
  zero2 = (lambda *a: (0, 0))
  zero3 = (lambda *a: (0, 0, 0))
  del nidx
  return [
      pl.BlockSpec((CONV_W, LRU_WIDTH), zero2),
      pl.BlockSpec((1, LRU_WIDTH), zero2),
      pl.BlockSpec((LRU_BLOCKS, LRU_BLOCK_W, LRU_BLOCK_W), zero3),
      pl.BlockSpec((1, LRU_WIDTH), zero2),
      pl.BlockSpec((LRU_BLOCKS, LRU_BLOCK_W, LRU_BLOCK_W), zero3),
      pl.BlockSpec((1, LRU_WIDTH), zero2),
      pl.BlockSpec((1, LRU_WIDTH), zero2),
  ]


def _lru_prompt(proj, lru_w, batch, seq):
  tt = _pick_block(seq, 512, LANES)
  nt = seq // tt
  body = functools.partial(_lru_prompt_body, tt=tt)
  return pl.pallas_call(
      body,
      grid=(batch, nt),
      in_specs=[pl.BlockSpec((tt, LRU_WIDTH), lambda b, ti: (b * nt + ti, OFF_V // LRU_WIDTH))]
      + _lru_weight_specs(2),
      out_specs=[
          pl.BlockSpec((tt, LRU_WIDTH), lambda b, ti: (b * nt + ti, 0)),
          pl.BlockSpec((1, 1, LRU_WIDTH), lambda b, ti: (b, 0, 0)),
      ],
      out_shape=[
          jax.ShapeDtypeStruct((batch * seq, LRU_WIDTH), BF16),
          jax.ShapeDtypeStruct((batch, 1, LRU_WIDTH), F32),
      ],
      scratch_shapes=[
          pltpu.VMEM((tt + 2 * SUBLANES, LRU_WIDTH), F32),
          pltpu.VMEM((tt, LRU_WIDTH), F32),
          pltpu.VMEM((tt, LRU_WIDTH), F32),
          pltpu.VMEM((tt, LRU_WIDTH), F32),
          pltpu.VMEM((1, LRU_WIDTH), F32),
      ],
      compiler_params=_params(("parallel", "arbitrary")),
      name="lru_prompt",
  )(proj, *lru_w)


def _lru_sample_body(xl_ref, cbuf_ref, h0_ref, cw_ref, cb_ref, wa_ref, ba_ref, wx_ref, bx_ref, lam_ref,
                     y_ref, hl_ref, *, dec_seq):
  ext = [cbuf_ref[j] for j in range(CONV_W - 1)] + [xl_ref[t] for t in range(dec_seq)]
  h = h0_ref[...]
  for t in range(dec_seq):
    xc = _conv(lambda j: ext[t + j], cw_ref, cb_ref[...])
    a, u = _lru_gates(xc, wa_ref, ba_ref[...], wx_ref, bx_ref[...], lam_ref[...])
    h = a * h + u
    y_ref[t] = h.astype(y_ref.dtype)
  hl_ref[...] = h


def _lru_sample(xl_t, cbuf_t, h0, lru_w):
  dec_seq, dec_batch, _ = xl_t.shape
  body = functools.partial(_lru_sample_body, dec_seq=dec_seq)
  return pl.pallas_call(
      body,
      out_shape=[
          jax.ShapeDtypeStruct((dec_seq, dec_batch, LRU_WIDTH), BF16),
          jax.ShapeDtypeStruct((dec_batch, LRU_WIDTH), F32),
      ],
      name="lru_sample",
  )(xl_t, cbuf_t, h0, *lru_w)


def _merge_body(att_ref, lru_ref, ga0_ref, ga1_ref, gl0_ref, gl1_ref, x_ref, wa_ref, wl_ref, wo_ref, g2_ref,
                h_ref, xn_ref):
  a1 = jnp.dot(att_ref[...], wa_ref[...], preferred_element_type=F32)
  a2 = jnp.dot(lru_ref[...], wl_ref[...], preferred_element_type=F32)
  ga = jnp.concatenate([ga0_ref[...], ga1_ref[...]], axis=1)
  gl = jnp.concatenate([gl0_ref[...], gl1_ref[...]], axis=1)
  m = jax.nn.sigmoid(ga) * a1 + jax.nn.sigmoid(gl) * a2
  h = x_ref[...] + jnp.dot(m.astype(BF16), wo_ref[...], preferred_element_type=F32)
  h_ref[...] = h
  xn_ref[...] = _rms(h, g2_ref[...]).astype(BF16)


def _merge(att, lru, proj, x_all, wa, wl, wo, g2):
  t = x_all.shape[0]
  tm = _pick_block(t, 320, 64)
  half = D_MODEL // 2
  const = lambda shape: pl.BlockSpec(shape, lambda i: (0, 0), pipeline_mode=pl.Buffered(1))
  gate = lambda blk: pl.BlockSpec((tm, half), lambda i: (i, blk))
  return pl.pallas_call(
      _merge_body,
      grid=(t // tm,),
      in_specs=[
          pl.BlockSpec((tm, ATT_WIDTH), lambda i: (i, 0)),
          pl.BlockSpec((tm, LRU_WIDTH), lambda i: (i, 0)),
          gate(OFF_L // half), gate(OFF_L // half + 1), gate(OFF_GA // half), gate(OFF_GA // half + 1),
          pl.BlockSpec((tm, D_MODEL), lambda i: (i, 0)),
          const((ATT_WIDTH, D_MODEL)), const((LRU_WIDTH, D_MODEL)), const((D_MODEL, D_MODEL)),
          const((1, D_MODEL)),
      ],
      out_specs=[
          pl.BlockSpec((tm, D_MODEL), lambda i: (i, 0)),
          pl.BlockSpec((tm, D_MODEL), lambda i: (i, 0)),
      ],
      out_shape=[
          jax.ShapeDtypeStruct((t, D_MODEL), F32),
          jax.ShapeDtypeStruct((t, D_MODEL), BF16),
      ],
      compiler_params=_params(("parallel",)),
      name="merge",
  )(att, lru, proj, proj, proj, proj, x_all, wa, wl, wo, g2)


def _mm_body(x_ref, w_ref, o_ref):
  o_ref[...] = jnp.dot(x_ref[...], w_ref[...], preferred_element_type=F32)


def _matmul(x, w):
  t, kdim = x.shape
  n = w.shape[1]
  tm = _pick_block(t, 640, LANES)
  tn = 512
  return pl.pallas_call(
      _mm_body,
      grid=(t // tm, n // tn),
      in_specs=[pl.BlockSpec((tm, kdim), lambda i, j: (i, 0)), pl.BlockSpec((kdim, tn), lambda i, j: (0, j))],
      out_specs=pl.BlockSpec((tm, tn), lambda i, j: (i, j)),
      out_shape=jax.ShapeDtypeStruct((t, n), F32),
      compiler_params=_params(("parallel", "arbitrary")),
      name="peer_query",
  )(x, w)


def _top16_rows(s, val_ref, idx_ref, row0):
  n = s.shape[0]
  sub = lax.broadcasted_iota(jnp.int32, s.shape, 0)
  for k in range(PEER_TOPK):
    m = jnp.max(s, axis=0, keepdims=True)
    idx = jnp.min(jnp.where(s == m, sub, n), axis=0, keepdims=True)
    val_ref[row0 + k:row0 + k + 1, :] = m
    idx_ref[row0 + k:row0 + k + 1, :] = idx
    s = jnp.where(sub == idx, NEG_INF, s)


def _topk_body(q_ref, k1_ref, k2_ref, eid_ref, gate_ref, val_ref, idx_ref, best_ref, sel_ref):
  tb = q_ref.shape[0]
  kk = PEER_TOPK
  half_w = N_KEYS

  def head(h, carry):
    for c, key_ref in enumerate((k1_ref, k2_ref)):
      col = pl.multiple_of((2 * h + c) * half_w, half_w)
      qh = q_ref[:, pl.ds(col, half_w)]
      s = lax.dot_general(key_ref[...], qh, NT_DIMS, preferred_element_type=F32,
                          precision=lax.Precision.HIGHEST)
      _top16_rows(s, val_ref, idx_ref, c * kk)
    v1, v2 = val_ref[0:kk, :], val_ref[kk:2 * kk, :]
    i1, i2 = idx_ref[0:kk, :], idx_ref[kk:2 * kk, :]
    b16 = lax.broadcasted_iota(jnp.int32, (kk, tb), 0)
    b8 = lax.broadcasted_iota(jnp.int32, (SUBLANES, tb), 0)
    vals = [v1[0:1] + v2]
    flat = [b16]
    code = [i1[0:1] * N_KEYS + i2]
    for a in range(1, SUBLANES):
      vals.append(v1[a:a + 1] + v2[0:SUBLANES])
      flat.append(a * kk + b8)
      code.append(i1[a:a + 1] * N_KEYS + i2[0:SUBLANES])
    vals.append(v1[SUBLANES:kk] + v2[0:1])
    flat.append((b8 + SUBLANES) * kk)
    code.append(i1[SUBLANES:kk] * N_KEYS + i2[0:1])
    cand = jnp.concatenate(vals, axis=0)
    flat = jnp.concatenate(flat, axis=0)
    code = jnp.concatenate(code, axis=0)
    for k in range(kk):
      m = jnp.max(cand, axis=0, keepdims=True)
      fsel = jnp.min(jnp.where(cand == m, flat, kk * kk), axis=0, keepdims=True)
      hit = flat == fsel
      best_ref[k:k + 1, :] = m
      sel_ref[k:k + 1, :] = jnp.max(jnp.where(hit, code, -1), axis=0, keepdims=True)
      cand = jnp.where(hit, NEG_INF, cand)
    best = best_ref[...]
    e = jnp.exp(best - best[0:1])
    row = pl.multiple_of(h * kk, kk)
    gate_ref[pl.ds(row, kk), :] = e / jnp.sum(e, axis=0, keepdims=True)
    eid_ref[pl.ds(row, kk), :] = sel_ref[...]
    return carry

  lax.fori_loop(0, PEER_HEADS, head, 0)


def _peer_topk(qp, k1, k2):
  t = qp.shape[0]
  tb = LANES
  return pl.pallas_call(
      _topk_body,
      grid=(t // tb,),
      in_specs=[
          pl.BlockSpec((tb, qp.shape[1]), lambda i: (i, 0)),
          pl.BlockSpec((N_KEYS, N_KEYS), lambda i: (0, 0)),
          pl.BlockSpec((N_KEYS, N_KEYS), lambda i: (0, 0)),
      ],
      out_specs=[
          pl.BlockSpec((PEER_SEL, tb), lambda i: (0, i)),
          pl.BlockSpec((PEER_SEL, tb), lambda i: (0, i)),
      ],
      out_shape=[
          jax.ShapeDtypeStruct((PEER_SEL, t), jnp.int32),
          jax.ShapeDtypeStruct((PEER_SEL, t), F32),
      ],
      scratch_shapes=[
          pltpu.VMEM((2 * PEER_TOPK, tb), F32),
          pltpu.VMEM((2 * PEER_TOPK, tb), jnp.int32),
          pltpu.VMEM((PEER_TOPK, tb), F32),
          pltpu.VMEM((PEER_TOPK, tb), jnp.int32),
      ],
      compiler_params=_params(("parallel",)),
      name="peer_topk",
  )(qp, k1, k2)


def _gate_matrix_body(eid_ref, gate_ref, g_ref, scr_ref):
  tb = eid_ref.shape[0]
  sub = lax.broadcasted_iota(jnp.int32, (N_KEYS, PEER_SEL), 0)

  def token(t, carry):
    e = eid_ref[pl.ds(t, 1), :]
    gt = gate_ref[pl.ds(t, 1), :]
    at = jnp.where(sub == lax.shift_right_logical(e, KEY_BITS), gt, 0.0).astype(BF16)
    bt = jnp.where(sub == jnp.bitwise_and(e, N_KEYS - 1), 1.0, 0.0).astype(BF16)
    gm = lax.dot_general(at, bt, NT_DIMS, preferred_element_type=F32)
    scr_ref[pl.ds(pl.multiple_of(t * N_KEYS, N_KEYS), N_KEYS), :] = gm
    return carry

  lax.fori_loop(0, tb, token, 0)
  for i1 in range(N_KEYS):
    g_ref[:, i1 * N_KEYS:(i1 + 1) * N_KEYS] = scr_ref[pl.ds(i1, tb, stride=N_KEYS), :].astype(g_ref.dtype)


def _gate_matrix(eid_t, gate_t):
  t = eid_t.shape[0]
  tb = LANES
  return pl.pallas_call(
      _gate_matrix_body,
      grid=(t // tb,),
      in_specs=[pl.BlockSpec((tb, PEER_SEL), lambda i: (i, 0)), pl.BlockSpec((tb, PEER_SEL), lambda i: (i, 0))],
      out_specs=pl.BlockSpec((tb, N_EXPERTS), lambda i: (i, 0)),
      out_shape=jax.ShapeDtypeStruct((t, N_EXPERTS), BF16),
      scratch_shapes=[pltpu.VMEM((tb * N_KEYS, N_KEYS), F32)],
      compiler_params=_params(("parallel",)),
      name="peer_gate_matrix",
  )(eid_t, gate_t)


def _peer_dense_body(x_ref, u_ref, v_ref, g_ref, h_ref, fg_ref, y_ref, acc_ref):
  c = pl.program_id(1)

  @pl.when(c == 0)
  def _():
    acc_ref[...] = jnp.zeros_like(acc_ref)

  s = lax.dot_general(x_ref[...], u_ref[...], NT_DIMS, preferred_element_type=F32)
  act = 0.5 * s * (1.0 + lax.erf(s * (2.0 ** -0.5)))
  coef = (g_ref[...].astype(F32) * act).astype(BF16)
  acc_ref[...] += jnp.dot(coef, v_ref[...], preferred_element_type=F32)

  @pl.when(c == pl.num_programs(1) - 1)
  def _():
    y_ref[...] = _rms(h_ref[...] + acc_ref[...], fg_ref[...])


def _peer_dense(xn2, u_bf, v_bf, gmat, h, final_g):
  t = xn2.shape[0]
  tm = _pick_block(t, 640, LANES)
  te = 512
  return pl.pallas_call(
      _peer_dense_body,
      grid=(t // tm, N_EXPERTS // te),
      in_specs=[
          pl.BlockSpec((tm, D_MODEL), lambda i, c: (i, 0)),
          pl.BlockSpec((te, D_MODEL), lambda i, c: (c, 0)),
          pl.BlockSpec((te, D_MODEL), lambda i, c: (c, 0)),
          pl.BlockSpec((tm, te), lambda i, c: (i, c)),
          pl.BlockSpec((tm, D_MODEL), lambda i, c: (i, 0)),
          pl.BlockSpec((1, D_MODEL), lambda i, c: (0, 0)),
      ],
      out_specs=pl.BlockSpec((tm, D_MODEL), lambda i, c: (i, 0)),
      out_shape=jax.ShapeDtypeStruct((t, D_MODEL), F32),
      scratch_shapes=[pltpu.VMEM((tm, D_MODEL), F32)],
      compiler_params=_params(("parallel", "arbitrary")),
      name="peer_dense",
  )(xn2, u_bf, v_bf, gmat, h, final_g)


def _rope_tables(pos):
  half = ROT_DIM // 2
  inv_freq = jnp.float32(ROPE_THETA) ** (-jnp.arange(half, dtype=F32) * 2.0 / ROT_DIM)
  ang = pos.astype(F32)[:, None] * inv_freq[None, :]
  cos, sin = jnp.cos(ang), jnp.sin(ang)
  n = pos.shape[0]
  ones = jnp.ones((n, HEAD_DIM - ROT_DIM), F32)
  zeros = jnp.zeros((n, HEAD_DIM - ROT_DIM), F32)
  zh = jnp.zeros((n, half), F32)
  cos_c = jnp.concatenate([cos, cos, ones], axis=1)
  sa_c = jnp.concatenate([-sin, zh, zeros], axis=1)
  sb_c = jnp.concatenate([zh, sin, zeros], axis=1)
  two = lambda a: jnp.concatenate([a, a], axis=1)
  return two(cos_c), two(sa_c), two(sb_c)


def kernel(x_prompt, x_sample, cache_k, cache_v, state_conv, state_h, page_table, norm1_g, w_in, lambda_q1, lambda_k1, lambda_q2, lambda_k2, subln_g, conv_w, conv_b, lru_wa, lru_ba, lru_wx, lru_bx, lru_lambda, w_att_up, w_lru_up, w_out, norm2_g, peer_wq, peer_k1, peer_k2, peer_u, peer_v, final_g):
  batch, seq, _ = x_prompt.shape
  dec_batch, dec_seq, _ = x_sample.shape
  n_pages = page_table.shape[1]
  past_len = n_pages * PAGE_SIZE
  tp = batch * seq
  ts = dec_batch * dec_seq
  assert w_in.shape[0] == 1, "one layer"

  x_all = jnp.concatenate([x_prompt.reshape(tp, D_MODEL), x_sample.reshape(ts, D_MODEL)], axis=0)
  pos = jnp.concatenate([
      jnp.tile(jnp.arange(seq, dtype=jnp.int32), batch),
      jnp.tile(past_len + jnp.arange(dec_seq, dtype=jnp.int32), dec_batch)])
  cos_t, sa_t, sb_t = _rope_tables(pos)
  lam = (jnp.exp(jnp.sum(lambda_q1[0].astype(F32) * lambda_k1[0].astype(F32)))
         - jnp.exp(jnp.sum(lambda_q2[0].astype(F32) * lambda_k2[0].astype(F32))) + LAM_INIT).reshape(1)
  row = lambda a: a.reshape(1, -1)

  proj = _inproj(x_all, row(norm1_g[0]), w_in[0].astype(BF16), cos_t, sa_t, sb_t)

  att_p = _prompt_attention(proj, lam, row(subln_g[0]), batch, seq)
  proj_s = proj[tp:]
  q_s = proj_s[:, :OFF_Q].reshape(dec_batch, dec_seq, N_KV_HEADS, 2, 2, HEAD_DIM)
  q_s = q_s.transpose(0, 2, 4, 3, 1, 5).reshape(dec_batch, N_KV_HEADS, 2, 2 * dec_seq, HEAD_DIM)
  wq = jnp.einsum("bgcnd,gh,ce->bgcnhed", q_s, jnp.eye(N_KV_HEADS, dtype=F32), jnp.eye(2, dtype=F32))
  wq = wq.reshape(dec_batch, N_KV_HEADS * 2 * 2 * dec_seq, KV_WIDTH).astype(BF16)
  pad_new = lambda a: jnp.pad(a.reshape(dec_batch, dec_seq, KV_WIDTH),
                              ((0, 0), (0, NEW_ROWS - dec_seq), (0, 0))).astype(BF16)
  k_new, v_new = proj_s[:, OFF_Q:OFF_K], proj_s[:, OFF_K:OFF_V]
  n_pool = cache_k.shape[1]
  att_s = _sample_attention(page_table, lam, wq, pad_new(k_new), pad_new(v_new), row(subln_g[0]),
                            cache_k[0].reshape(n_pool, PAGE_SIZE, KV_WIDTH),
                            cache_v[0].reshape(n_pool, PAGE_SIZE, KV_WIDTH), dec_seq)
  att_s = att_s.reshape(dec_batch, N_KV_HEADS, 2, dec_seq, HEAD_W).transpose(0, 3, 1, 2, 4)
  att_all = jnp.concatenate([att_p, att_s.reshape(ts, ATT_WIDTH).astype(BF16)], axis=0)

  lru_w = (conv_w[0], row(conv_b[0]), lru_wa[0], row(lru_ba[0]), lru_wx[0], row(lru_bx[0]), row(lru_lambda[0]))
  lru_p, h_p = _lru_prompt(proj, lru_w, batch, seq)
  xl_s = proj_s[:, OFF_V:OFF_L].reshape(dec_batch, dec_seq, LRU_WIDTH)
  lru_s, h_s = _lru_sample(xl_s.transpose(1, 0, 2), state_conv[0].transpose(1, 0, 2), state_h[0], lru_w)
  lru_all = jnp.concatenate([lru_p, lru_s.transpose(1, 0, 2).reshape(ts, LRU_WIDTH)], axis=0)

  h_all, xn2 = _merge(att_all, lru_all, proj, x_all, w_att_up[0].astype(BF16), w_lru_up[0].astype(BF16),
                      w_out[0].astype(BF16), row(norm2_g[0]))
  qp = _matmul(xn2, peer_wq[0].astype(BF16))
  eid, gate = _peer_topk(qp, peer_k1[0], peer_k2[0])
  gmat = _gate_matrix(eid.T, gate.T)
  y_all = _peer_dense(xn2, peer_u[0].astype(BF16), peer_v[0].astype(BF16), gmat, h_all, row(final_g))

  kv_shape_p = (1, batch, seq, N_KV_HEADS, HEAD_W)
  kv_shape_s = (1, dec_batch, dec_seq, N_KV_HEADS, HEAD_W)
  xl_p = proj[:tp, OFF_V:OFF_L].reshape(batch, seq, LRU_WIDTH)
  conv_s = jnp.concatenate([state_conv[0].astype(F32), xl_s], axis=1)[:, -(CONV_W - 1):]
  return (
      y_all[:tp].reshape(batch, seq, D_MODEL),
      y_all[tp:].reshape(dec_batch, dec_seq, D_MODEL),
      proj[:tp, OFF_Q:OFF_K].reshape(kv_shape_p),
      proj[:tp, OFF_K:OFF_V].reshape(kv_shape_p),
      xl_p[:, -(CONV_W - 1):][None],
      h_p.reshape(1, batch, LRU_WIDTH),
      k_new.reshape(kv_shape_s),
      v_new.reshape(kv_shape_s),
      conv_s[None],
      h_s[None],
  )
```

```python
import functools
import math

import jax
import jax.numpy as jnp
from jax import lax
from jax.experimental import pallas as pl
from jax.experimental.pallas import tpu as pltpu

F32 = jnp.float32
BF16 = jnp.bfloat16

D_MODEL = 2048
N_HEADS = 8
N_KV_HEADS = 4
HEAD_DIM = 64
HEAD_W = 2 * HEAD_DIM
ROT_DIM = HEAD_DIM // 4
ROPE_THETA = 500000.0
ATT_WIDTH = N_HEADS * HEAD_W
KV_WIDTH = N_KV_HEADS * HEAD_W
LRU_WIDTH = 1024
LRU_BLOCKS = 8
LRU_BLOCK_W = LRU_WIDTH // LRU_BLOCKS
CONV_W = 4
LRU_C = 8.0
N_KEYS = 128
KEY_BITS = 7
N_EXPERTS = N_KEYS * N_KEYS
PEER_HEADS = 8
PEER_TOPK = 16
PEER_SEL = PEER_HEADS * PEER_TOPK
PAGE_SIZE = 128
RMS_EPS = 1e-6
OFF_Q = ATT_WIDTH
OFF_K = OFF_Q + KV_WIDTH
OFF_V = OFF_K + KV_WIDTH
OFF_L = OFF_V + LRU_WIDTH
OFF_GA = OFF_L + D_MODEL
IN_WIDTH = OFF_GA + D_MODEL
LAM_INIT = 0.8 - 0.6 * math.exp(0.0)

LANES = 128
SUBLANES = 8
VMEM_LIMIT_BYTES = 56 * 1024 * 1024

NEG_INF = float("-inf")
NT_DIMS = (((1,), (1,)), ((), ()))


def _pick_block(total, cap, quantum):
  best = None
  b = quantum
  while b <= min(cap, total):
    if total % b == 0:
      best = b
    b += quantum
  assert best is not None, (total, cap, quantum)
  return best


def _params(sem, vmem=VMEM_LIMIT_BYTES):
  return pltpu.CompilerParams(dimension_semantics=sem, vmem_limit_bytes=vmem)


def _rms(x, g):
  var = jnp.mean(x * x, axis=-1, keepdims=True)
  return x * lax.rsqrt(var + RMS_EPS) * g


def _inproj_body(x_ref, g_ref, w_ref, cos_ref, sa_ref, sb_ref, o_ref, xn_ref, *, tn, n_q_blocks, n_rope_blocks):
  j = pl.program_id(1)

  @pl.when(j == 0)
  def _():
    xn_ref[...] = _rms(x_ref[...], g_ref[...]).astype(BF16)

  acc = jnp.dot(xn_ref[...], w_ref[...], preferred_element_type=F32)

  @pl.when(j < n_rope_blocks)
  def _():
    reps = tn // LANES
    cos = jnp.concatenate([cos_ref[...]] * reps, axis=1)
    sa = jnp.concatenate([sa_ref[...]] * reps, axis=1)
    sb = jnp.concatenate([sb_ref[...]] * reps, axis=1)
    half = ROT_DIM // 2
    rot = acc * cos + pltpu.roll(acc, tn - half, 1) * sa + pltpu.roll(acc, half, 1) * sb
    scale = jnp.where(j < n_q_blocks, HEAD_DIM ** -0.5, 1.0).astype(F32)
    o_ref[...] = rot * scale

  @pl.when(j >= n_rope_blocks)
  def _():
    o_ref[...] = acc


def _inproj(x_all, norm_g, w_in_bf, cos_t, sa_t, sb_t):
  t = x_all.shape[0]
  tm = _pick_block(t, 640, LANES)
  tn = 512
  body = functools.partial(_inproj_body, tn=tn, n_q_blocks=OFF_Q // tn, n_rope_blocks=OFF_K // tn)
  return pl.pallas_call(
      body,
      grid=(t // tm, IN_WIDTH // tn),
      in_specs=[
          pl.BlockSpec((tm, D_MODEL), lambda i, j: (i, 0)),
          pl.BlockSpec((1, D_MODEL), lambda i, j: (0, 0)),
          pl.BlockSpec((D_MODEL, tn), lambda i, j: (0, j)),
          pl.BlockSpec((tm, LANES), lambda i, j: (i, 0)),
          pl.BlockSpec((tm, LANES), lambda i, j: (i, 0)),
          pl.BlockSpec((tm, LANES), lambda i, j: (i, 0)),
      ],
      out_specs=pl.BlockSpec((tm, tn), lambda i, j: (i, j)),
      out_shape=jax.ShapeDtypeStruct((t, IN_WIDTH), F32),
      scratch_shapes=[pltpu.VMEM((tm, D_MODEL), BF16)],
      compiler_params=_params(("parallel", "arbitrary")),
      name="inproj",
  )(x_all, norm_g, w_in_bf, cos_t, sa_t, sb_t)


def _subln(o, g):
  var = jnp.mean(o * o, axis=-1, keepdims=True)
  return o * lax.rsqrt(var + RMS_EPS) * g * (1.0 - LAM_INIT)


def _pattn_body(lam_ref, q_ref, k_ref, v_ref, g_ref, o_ref, q4_ref, m_ref, l_ref, acc_ref, *, tq, tk):
  qi = pl.program_id(2)
  ki = pl.program_id(3)
  cols = 2 * 2 * tq

  @pl.when(ki == 0)
  def _():
    m_ref[...] = jnp.full((1, cols), NEG_INF, F32)
    l_ref[...] = jnp.zeros((1, cols), F32)
    acc_ref[...] = jnp.zeros((HEAD_W, cols), F32)
    lane = lax.broadcasted_iota(jnp.int32, (tq, HEAD_W), 1)
    for r in range(2):
      qh = q_ref[:, r * HEAD_W:(r + 1) * HEAD_W]
      q4_ref[(2 * r) * tq:(2 * r + 1) * tq, :] = jnp.where(lane < HEAD_DIM, qh, 0.0).astype(BF16)
      q4_ref[(2 * r + 1) * tq:(2 * r + 2) * tq, :] = jnp.where(lane >= HEAD_DIM, qh, 0.0).astype(BF16)

  def update(diagonal):
    k = k_ref[...].astype(BF16)
    s = lax.dot_general(k, q4_ref[...], NT_DIMS, preferred_element_type=F32)
    if diagonal:
      visible = (lax.broadcasted_iota(jnp.int32, (tk, tq), 0) <= lax.broadcasted_iota(jnp.int32, (tk, tq), 1))
      s = jnp.concatenate([jnp.where(visible, s[:, j * tq:(j + 1) * tq], NEG_INF) for j in range(4)], axis=1)
    m_old = m_ref[...]
    m_new = jnp.maximum(m_old, jnp.max(s, axis=0, keepdims=True))
    alpha = jnp.exp(m_old - m_new)
    p = jnp.exp(s - m_new)
    l_ref[...] = alpha * l_ref[...] + jnp.sum(p, axis=0, keepdims=True)
    vt = v_ref[...].T.astype(BF16)
    acc_ref[...] = alpha * acc_ref[...] + jnp.dot(vt, p.astype(BF16), preferred_element_type=F32)
    m_ref[...] = m_new

  @pl.when(ki < qi)
  def _():
    update(False)

  @pl.when(ki == qi)
  def _():
    update(True)
    lam = lam_ref[0]
    o = acc_ref[...] * (1.0 / l_ref[...])
    for r in range(2):
      d = o[:, (2 * r) * tq:(2 * r + 1) * tq] - lam * o[:, (2 * r + 1) * tq:(2 * r + 2) * tq]
      var = jnp.mean(d * d, axis=0, keepdims=True)
      dn = d * (lax.rsqrt(var + RMS_EPS) * (1.0 - LAM_INIT))
      o_ref[:, r * HEAD_W:(r + 1) * HEAD_W] = (dn.T * g_ref[...]).astype(o_ref.dtype)


def _prompt_attention(proj, lam, subln_g, batch, seq):
  tq = tk = _pick_block(seq, 512, LANES)
  nq = seq // tq
  gw = 2 * HEAD_W
  body = functools.partial(_pattn_body, tq=tq, tk=tk)
  return pl.pallas_call(
      body,
      grid=(batch, N_KV_HEADS, nq, nq),
      in_specs=[
          pl.BlockSpec(memory_space=pltpu.SMEM),
          pl.BlockSpec((tq, gw), lambda b, g, qi, ki: (b * nq + qi, g)),
          pl.BlockSpec((tk, HEAD_W), lambda b, g, qi, ki: (b * nq + jnp.minimum(ki, qi), OFF_Q // HEAD_W + g)),
          pl.BlockSpec((tk, HEAD_W), lambda b, g, qi, ki: (b * nq + jnp.minimum(ki, qi), OFF_K // HEAD_W + g)),
          pl.BlockSpec((1, HEAD_W), lambda b, g, qi, ki: (0, 0)),
      ],
      out_specs=pl.BlockSpec((tq, gw), lambda b, g, qi, ki: (b * nq + qi, g)),
      out_shape=jax.ShapeDtypeStruct((batch * seq, ATT_WIDTH), BF16),
      scratch_shapes=[
          pltpu.VMEM((4 * tq, HEAD_W), BF16),
          pltpu.VMEM((1, 4 * tq), F32),
          pltpu.VMEM((1, 4 * tq), F32),
          pltpu.VMEM((HEAD_W, 4 * tq), F32),
      ],
      compiler_params=_params(("parallel", "parallel", "parallel", "arbitrary")),
      name="prompt_attention",
  )(lam, proj, proj, proj, subln_g)


PAGES_PER_CHUNK = 8
PAGE_ROWS = PAGE_SIZE * N_KV_HEADS


def _sattn_body(pt_ref, lam_ref, wq_ref, kn_ref, vn_ref, g_ref, ck_ref, cv_ref, o_ref,
                kbuf, vbuf, bias_ref, sem, *, n_chunks, dec_seq):
  b = pl.program_id(0)
  nb = pl.num_programs(0)
  rows = wq_ref.shape[1]
  gr = rows // N_KV_HEADS
  cols = PAGES_PER_CHUNK * PAGE_ROWS

  def copies(bb, c, slot):
    out = []
    for p in range(PAGES_PER_CHUNK):
      page = pt_ref[bb, c * PAGES_PER_CHUNK + p]
      out.append(pltpu.make_async_copy(ck_ref.at[page], kbuf.at[slot, p], sem.at[0, slot]))
      out.append(pltpu.make_async_copy(cv_ref.at[page], vbuf.at[slot, p], sem.at[1, slot]))
    return out

  def start(bb, c, slot):
    for cp in copies(bb, c, slot):
      cp.start()

  def same_head(shape):
    head_of_col = jnp.bitwise_and(lax.broadcasted_iota(jnp.int32, shape, 1), N_KV_HEADS - 1)
    head_of_row = lax.broadcasted_iota(jnp.int32, shape, 0) // gr
    return head_of_col == head_of_row

  @pl.when(b == 0)
  def _():
    start(0, 0, 0)
    bias_ref[...] = jnp.where(same_head((rows, cols)), 0.0, NEG_INF)

  wq = wq_ref[0]

  def softmax_step(carry, s, v):
    m_old, l_old, acc = carry
    m_new = jnp.maximum(m_old, jnp.max(s, axis=1, keepdims=True))
    alpha = jnp.exp(m_old - m_new)
    p = jnp.exp(s - m_new)
    l_new = alpha * l_old + jnp.sum(p, axis=1, keepdims=True)
    acc = alpha * acc + jnp.dot(p.astype(BF16), v, preferred_element_type=F32)
    return m_new, l_new, acc

  def chunk(c, carry):
    slot = c % 2

    @pl.when(c + 1 < n_chunks)
    def _():
      start(b, c + 1, 1 - slot)

    @pl.when(jnp.logical_and(c + 1 == n_chunks, b + 1 < nb))
    def _():
      start(b + 1, 0, 1 - slot)

    for cp in copies(b, c, slot):
      cp.wait()
    kc = kbuf[slot].reshape(cols, HEAD_W).astype(BF16)
    vc = vbuf[slot].reshape(cols, HEAD_W).astype(BF16)
    s = lax.dot_general(wq, kc, NT_DIMS, preferred_element_type=F32) + bias_ref[...]
    return softmax_step(carry, s, vc)

  init = (jnp.full((rows, 1), NEG_INF, F32), jnp.zeros((rows, 1), F32), jnp.zeros((rows, HEAD_W), F32))
  carry = lax.fori_loop(0, n_chunks, chunk, init)

  new_rows = kn_ref.shape[1]
  s = lax.dot_general(wq, kn_ref[0], NT_DIMS, preferred_element_type=F32)
  t_of_row = lax.broadcasted_iota(jnp.int32, (rows, new_rows), 0) % dec_seq
  t_of_col = lax.broadcasted_iota(jnp.int32, (rows, new_rows), 1) // N_KV_HEADS
  visible = jnp.logical_and(same_head((rows, new_rows)), t_of_col <= t_of_row)
  _, l_fin, acc = softmax_step(carry, jnp.where(visible, s, NEG_INF), vn_ref[0])

  o = acc / l_fin
  lam = lam_ref[0]
  for g in range(N_KV_HEADS):
    blk = o[g * gr:(g + 1) * gr]
    d = blk[:gr // 2] - lam * blk[gr // 2:]
    o_ref[0, g * (gr // 2):(g + 1) * (gr // 2), :] = _subln(d, g_ref[...])


def _sample_attention(page_table, lam, wq, k_new, v_new, subln_g, cache_k, cache_v, dec_seq):
  dec_batch, n_pages = page_table.shape
  assert n_pages % (2 * PAGES_PER_CHUNK) == 0
  n_chunks = n_pages // PAGES_PER_CHUNK
  rows = wq.shape[1]
  new_rows = k_new.shape[1]
  body = functools.partial(_sattn_body, n_chunks=n_chunks, dec_seq=dec_seq)
  grid_spec = pltpu.PrefetchScalarGridSpec(
      num_scalar_prefetch=1,
      grid=(dec_batch,),
      in_specs=[
          pl.BlockSpec(memory_space=pltpu.SMEM),
          pl.BlockSpec((1, rows, HEAD_W), lambda b, pt: (b, 0, 0)),
          pl.BlockSpec((1, new_rows, HEAD_W), lambda b, pt: (b, 0, 0)),
          pl.BlockSpec((1, new_rows, HEAD_W), lambda b, pt: (b, 0, 0)),
          pl.BlockSpec((1, HEAD_W), lambda b, pt: (0, 0)),
          pl.BlockSpec(memory_space=pl.ANY),
          pl.BlockSpec(memory_space=pl.ANY),
      ],
      out_specs=pl.BlockSpec((1, rows // 2, HEAD_W), lambda b, pt: (b, 0, 0)),
      scratch_shapes=[
          pltpu.VMEM((2, PAGES_PER_CHUNK, PAGE_ROWS, HEAD_W), F32),
          pltpu.VMEM((2, PAGES_PER_CHUNK, PAGE_ROWS, HEAD_W), F32),
          pltpu.VMEM((rows, PAGES_PER_CHUNK * PAGE_ROWS), F32),
          pltpu.SemaphoreType.DMA((2, 2)),
      ],
  )
  return pl.pallas_call(
      body,
      grid_spec=grid_spec,
      out_shape=jax.ShapeDtypeStruct((dec_batch, rows // 2, HEAD_W), F32),
      compiler_params=_params(("arbitrary",)),
      name="sample_attention",
  )(page_table, lam, wq, k_new, v_new, subln_g, cache_k, cache_v)


def _lru_gates(xc, wa_ref, ba, wx_ref, bx, lam):
  ra, ix = [], []
  for n in range(LRU_BLOCKS):
    xb = xc[:, n * LRU_BLOCK_W:(n + 1) * LRU_BLOCK_W]
    ra.append(jnp.dot(xb, wa_ref[n], preferred_element_type=F32, precision=lax.Precision.HIGHEST))
    ix.append(jnp.dot(xb, wx_ref[n], preferred_element_type=F32, precision=lax.Precision.HIGHEST))
  r = jax.nn.sigmoid(jnp.concatenate(ra, axis=1) + ba)
  i = jax.nn.sigmoid(jnp.concatenate(ix, axis=1) + bx)
  neg = -lam
  softplus = jnp.maximum(neg, 0.0) + jnp.log1p(jnp.exp(-jnp.abs(neg)))
  log_a = -LRU_C * r * softplus
  a = jnp.exp(log_a)
  u = jnp.sqrt(1.0 - jnp.exp(2.0 * log_a)) * (i * xc)
  return a, u


def _conv(rows_of, cw_ref, cb):
  out = rows_of(0) * cw_ref[0:1, :]
  for j in range(1, CONV_W):
    out = out + rows_of(j) * cw_ref[j:j + 1, :]
  return out + cb


def _lru_prompt_body(xl_ref, cw_ref, cb_ref, wa_ref, ba_ref, wx_ref, bx_ref, lam_ref, y_ref, hl_ref,
                     ext_ref, a_ref, u_ref, hs_ref, h_ref, *, tt):
  ti = pl.program_id(1)
  head = SUBLANES

  @pl.when(ti == 0)
  def _():
    ext_ref[0:head, :] = jnp.zeros((head, LRU_WIDTH), F32)
    h_ref[...] = jnp.zeros((1, LRU_WIDTH), F32)

  ext_ref[head:head + tt, :] = xl_ref[...]
  xc = _conv(lambda j: ext_ref[head - (CONV_W - 1) + j:head - (CONV_W - 1) + j + tt, :], cw_ref, cb_ref[...])
  a, u = _lru_gates(xc, wa_ref, ba_ref[...], wx_ref, bx_ref[...], lam_ref[...])
  a_ref[...] = a
  u_ref[...] = u

  def step(t, h):
    h = a_ref[pl.ds(t, 1), :] * h + u_ref[pl.ds(t, 1), :]
    hs_ref[pl.ds(t, 1), :] = h
    return h

  h_fin = lax.fori_loop(0, tt, step, h_ref[...], unroll=8)
  h_ref[...] = h_fin
  y_ref[...] = hs_ref[...].astype(y_ref.dtype)
  ext_ref[0:head, :] = ext_ref[tt:tt + head, :]

  @pl.when(ti == pl.num_programs(1) - 1)
  def _():
    hl_ref[0] = h_fin


def _lru_weight_specs(nidx):
  zero2 = (lambda *a: (0, 0))
  zero3 = (lambda *a: (0, 0, 0))
  del nidx
  return [
      pl.BlockSpec((CONV_W, LRU_WIDTH), zero2),
      pl.BlockSpec((1, LRU_WIDTH), zero2),
      pl.BlockSpec((LRU_BLOCKS, LRU_BLOCK_W, LRU_BLOCK_W), zero3),
      pl.BlockSpec((1, LRU_WIDTH), zero2),
      pl.BlockSpec((LRU_BLOCKS, LRU_BLOCK_W, LRU_BLOCK_W), zero3),
      pl.BlockSpec((1, LRU_WIDTH), zero2),
      pl.BlockSpec((1, LRU_WIDTH), zero2),
  ]


def _lru_prompt(proj, lru_w, batch, seq):
  tt = _pick_block(seq, 512, LANES)
  nt = seq // tt
  body = functools.partial(_lru_prompt_body, tt=tt)
  return pl.pallas_call(
      body,
      grid=(batch, nt),
      in_specs=[pl.BlockSpec((tt, LRU_WIDTH), lambda b, ti: (b * nt + ti, OFF_V // LRU_WIDTH))]
      + _lru_weight_specs(2),
      out_specs=[
          pl.BlockSpec((tt, LRU_WIDTH), lambda b, ti: (b * nt + ti, 0)),
          pl.BlockSpec((1, 1, LRU_WIDTH), lambda b, ti: (b, 0, 0)),
      ],
      out_shape=[
          jax.ShapeDtypeStruct((batch * seq, LRU_WIDTH), BF16),
          jax.ShapeDtypeStruct((batch, 1, LRU_WIDTH), F32),
      ],
      scratch_shapes=[
          pltpu.VMEM((tt + 2 * SUBLANES, LRU_WIDTH), F32),
          pltpu.VMEM((tt, LRU_WIDTH), F32),
          pltpu.VMEM((tt, LRU_WIDTH), F32),
          pltpu.VMEM((tt, LRU_WIDTH), F32),
          pltpu.VMEM((1, LRU_WIDTH), F32),
      ],
      compiler_params=_params(("parallel", "arbitrary")),
      name="lru_prompt",
  )(proj, *lru_w)


def _lru_sample_body(xl_ref, cbuf_ref, h0_ref, cw_ref, cb_ref, wa_ref, ba_ref, wx_ref, bx_ref, lam_ref,
                     y_ref, hl_ref, *, dec_seq):
  ext = [cbuf_ref[j] for j in range(CONV_W - 1)] + [xl_ref[t] for t in range(dec_seq)]
  h = h0_ref[...]
  for t in range(dec_seq):
    xc = _conv(lambda j: ext[t + j], cw_ref, cb_ref[...])
    a, u = _lru_gates(xc, wa_ref, ba_ref[...], wx_ref, bx_ref[...], lam_ref[...])
    h = a * h + u
    y_ref[t] = h.astype(y_ref.dtype)
  hl_ref[...] = h


def _lru_sample(xl_t, cbuf_t, h0, lru_w):
  dec_seq, dec_batch, _ = xl_t.shape
  body = functools.partial(_lru_sample_body, dec_seq=dec_seq)
  return pl.pallas_call(
      body,
      out_shape=[
          jax.ShapeDtypeStruct((dec_seq, dec_batch, LRU_WIDTH), BF16),
          jax.ShapeDtypeStruct((dec_batch, LRU_WIDTH), F32),
      ],
      name="lru_sample",
  )(xl_t, cbuf_t, h0, *lru_w)


def _merge_body(att_ref, lru_ref, ga0_ref, ga1_ref, gl0_ref, gl1_ref, x_ref, wa_ref, wl_ref, wo_ref, g2_ref,
                h_ref, xn_ref):
  a1 = jnp.dot(att_ref[...], wa_ref[...], preferred_element_type=F32)
  a2 = jnp.dot(lru_ref[...], wl_ref[...], preferred_element_type=F32)
  ga = jnp.concatenate([ga0_ref[...], ga1_ref[...]], axis=1)
  gl = jnp.concatenate([gl0_ref[...], gl1_ref[...]], axis=1)
  m = jax.nn.sigmoid(ga) * a1 + jax.nn.sigmoid(gl) * a2
  h = x_ref[...] + jnp.dot(m.astype(BF16), wo_ref[...], preferred_element_type=F32)
  h_ref[...] = h
  xn_ref[...] = _rms(h, g2_ref[...]).astype(BF16)


def _merge(att, lru, proj, x_all, wa, wl, wo, g2):
  t = x_all.shape[0]
  tm = _pick_block(t, 320, 64)
  half = D_MODEL // 2
  const = lambda shape: pl.BlockSpec(shape, lambda i: (0, 0), pipeline_mode=pl.Buffered(1))
  gate = lambda blk: pl.BlockSpec((tm, half), lambda i: (i, blk))
  return pl.pallas_call(
      _merge_body,
      grid=(t // tm,),
      in_specs=[
          pl.BlockSpec((tm, ATT_WIDTH), lambda i: (i, 0)),
          pl.BlockSpec((tm, LRU_WIDTH), lambda i: (i, 0)),
          gate(OFF_L // half), gate(OFF_L // half + 1), gate(OFF_GA // half), gate(OFF_GA // half + 1),
          pl.BlockSpec((tm, D_MODEL), lambda i: (i, 0)),
          const((ATT_WIDTH, D_MODEL)), const((LRU_WIDTH, D_MODEL)), const((D_MODEL, D_MODEL)),
          const((1, D_MODEL)),
      ],
      out_specs=[
          pl.BlockSpec((tm, D_MODEL), lambda i: (i, 0)),
          pl.BlockSpec((tm, D_MODEL), lambda i: (i, 0)),
      ],
      out_shape=[
          jax.ShapeDtypeStruct((t, D_MODEL), F32),
          jax.ShapeDtypeStruct((t, D_MODEL), BF16),
      ],
      compiler_params=_params(("parallel",)),
      name="merge",
  )(att, lru, proj, proj, proj, proj, x_all, wa, wl, wo, g2)


def _mm_body(x_ref, w_ref, o_ref):
  o_ref[...] = jnp.dot(x_ref[...], w_ref[...], preferred_element_type=F32)


def _matmul(x, w):
  t, kdim = x.shape
  n = w.shape[1]
  tm = _pick_block(t, 640, LANES)
  tn = 512
  return pl.pallas_call(
      _mm_body,
      grid=(t // tm, n // tn),
      in_specs=[pl.BlockSpec((tm, kdim), lambda i, j: (i, 0)), pl.BlockSpec((kdim, tn), lambda i, j: (0, j))],
      out_specs=pl.BlockSpec((tm, tn), lambda i, j: (i, j)),
      out_shape=jax.ShapeDtypeStruct((t, n), F32),
      compiler_params=_params(("parallel", "arbitrary")),
      name="peer_query",
  )(x, w)


def _top16_rows(s, val_ref, idx_ref, row0):
  n = s.shape[0]
  sub = lax.broadcasted_iota(jnp.int32, s.shape, 0).astype(F32)
  for k in range(PEER_TOPK):
    m = jnp.max(s, axis=0, keepdims=True)
    idx = jnp.min(jnp.where(s == m, sub, float(n)), axis=0, keepdims=True)
    val_ref[row0 + k:row0 + k + 1, :] = m
    idx_ref[row0 + k:row0 + k + 1, :] = idx
    s = jnp.where(sub == idx, NEG_INF, s)


def _topk_body(q_ref, k1_ref, k2_ref, eid_ref, gate_ref, val_ref, idx_ref, best_ref, sel_ref):
  tb = q_ref.shape[0]
  kk = PEER_TOPK
  half_w = N_KEYS

  def head(h, carry):
    for c, key_ref in enumerate((k1_ref, k2_ref)):
      col = pl.multiple_of((2 * h + c) * half_w, half_w)
      qh = q_ref[:, pl.ds(col, half_w)]
      s = lax.dot_general(key_ref[...], qh, NT_DIMS, preferred_element_type=F32,
                          precision=lax.Precision.HIGHEST)
      _top16_rows(s, val_ref, idx_ref, c * kk)
    v1, v2 = val_ref[0:kk, :], val_ref[kk:2 * kk, :]
    i1, i2 = idx_ref[0:kk, :], idx_ref[kk:2 * kk, :]
    b16 = lax.broadcasted_iota(jnp.int32, (kk, tb), 0).astype(F32)
    b8 = lax.broadcasted_iota(jnp.int32, (SUBLANES, tb), 0).astype(F32)
    vals = [v1[0:1] + v2]
    flat = [b16]
    code = [i1[0:1] * N_KEYS + i2]
    for a in range(1, SUBLANES):
      vals.append(v1[a:a + 1] + v2[0:SUBLANES])
      flat.append(a * kk + b8)
      code.append(i1[a:a + 1] * N_KEYS + i2[0:SUBLANES])
    vals.append(v1[SUBLANES:kk] + v2[0:1])
    flat.append((b8 + SUBLANES) * kk)
    code.append(i1[SUBLANES:kk] * N_KEYS + i2[0:1])
    cand = jnp.concatenate(vals, axis=0)
    flat = jnp.concatenate(flat, axis=0)
    code = jnp.concatenate(code, axis=0)
    for k in range(kk):
      m = jnp.max(cand, axis=0, keepdims=True)
      fsel = jnp.min(jnp.where(cand == m, flat, float(kk * kk)), axis=0, keepdims=True)
      hit = flat == fsel
      best_ref[k:k + 1, :] = m
      sel_ref[k:k + 1, :] = jnp.max(jnp.where(hit, code, -1.0), axis=0, keepdims=True)
      cand = jnp.where(hit, NEG_INF, cand)
    best = best_ref[...]
    e = jnp.exp(best - best[0:1])
    row = pl.multiple_of(h * kk, kk)
    gate_ref[pl.ds(row, kk), :] = e / jnp.sum(e, axis=0, keepdims=True)
    eid_ref[pl.ds(row, kk), :] = sel_ref[...].astype(jnp.int32)
    return carry

  lax.fori_loop(0, PEER_HEADS, head, 0)


def _peer_topk(qp, k1, k2):
  t = qp.shape[0]
  tb = LANES
  return pl.pallas_call(
      _topk_body,
      grid=(t // tb,),
      in_specs=[
          pl.BlockSpec((tb, qp.shape[1]), lambda i: (i, 0)),
          pl.BlockSpec((N_KEYS, N_KEYS), lambda i: (0, 0)),
          pl.BlockSpec((N_KEYS, N_KEYS), lambda i: (0, 0)),
      ],
      out_specs=[
          pl.BlockSpec((PEER_SEL, tb), lambda i: (0, i)),
          pl.BlockSpec((PEER_SEL, tb), lambda i: (0, i)),
      ],
      out_shape=[
          jax.ShapeDtypeStruct((PEER_SEL, t), jnp.int32),
          jax.ShapeDtypeStruct((PEER_SEL, t), F32),
      ],
      scratch_shapes=[
          pltpu.VMEM((2 * PEER_TOPK, tb), F32),
          pltpu.VMEM((2 * PEER_TOPK, tb), F32),
          pltpu.VMEM((PEER_TOPK, tb), F32),
          pltpu.VMEM((PEER_TOPK, tb), F32),
      ],
      compiler_params=_params(("parallel",)),
      name="peer_topk",
  )(qp, k1, k2)


def _gate_matrix_body(eid_ref, gate_ref, g_ref):
  tb = eid_ref.shape[0]
  sub = lax.broadcasted_iota(jnp.int32, (N_KEYS, PEER_SEL), 0)

  def token(t, carry):
    e = eid_ref[pl.ds(t, 1), :]
    gt = gate_ref[pl.ds(t, 1), :]
    at = jnp.where(sub == lax.shift_right_logical(e, KEY_BITS), gt, 0.0).astype(BF16)
    bt = jnp.where(sub == jnp.bitwise_and(e, N_KEYS - 1), 1.0, 0.0).astype(BF16)
    g_ref[t] = lax.dot_general(at, bt, NT_DIMS, preferred_element_type=F32)
    return carry

  lax.fori_loop(0, tb, token, 0, unroll=8)


def _gate_matrix(eid_t, gate_t):
  t = eid_t.shape[0]
  tb = _pick_block(t, 64, SUBLANES)
  return pl.pallas_call(
      _gate_matrix_body,
      grid=(t // tb,),
      in_specs=[pl.BlockSpec((tb, PEER_SEL), lambda i: (i, 0)), pl.BlockSpec((tb, PEER_SEL), lambda i: (i, 0))],
      out_specs=pl.BlockSpec((tb, N_KEYS, N_KEYS), lambda i: (i, 0, 0)),
      out_shape=jax.ShapeDtypeStruct((t, N_KEYS, N_KEYS), F32),
      compiler_params=_params(("parallel",)),
      name="peer_gate_matrix",
  )(eid_t, gate_t)


PEER_I1_PER_STEP = 8


def _peer_dense_body(x_ref, u_ref, v_ref, g_ref, h_ref, fg_ref, y_ref):
  c = pl.program_id(1)
  s = lax.dot_general(x_ref[...], u_ref[...], NT_DIMS, preferred_element_type=F32)
  act = 0.5 * s * (1.0 + lax.erf(s * (2.0 ** -0.5)))
  coef = jnp.concatenate(
      [g_ref[:, j, :] * act[:, j * N_KEYS:(j + 1) * N_KEYS] for j in range(PEER_I1_PER_STEP)], axis=1)
  contrib = jnp.dot(coef.astype(BF16), v_ref[...], preferred_element_type=F32)

  @pl.when(c == 0)
  def _():
    y_ref[...] = contrib

  @pl.when(c > 0)
  def _():
    y_ref[...] += contrib

  @pl.when(c == pl.num_programs(1) - 1)
  def _():
    y_ref[...] = _rms(h_ref[...] + y_ref[...], fg_ref[...])


def _peer_dense(xn2, u_bf, v_bf, gmat, h, final_g):
  t = xn2.shape[0]
  tm = _pick_block(t, 640, LANES)
  te = PEER_I1_PER_STEP * N_KEYS
  return pl.pallas_call(
      _peer_dense_body,
      grid=(t // tm, N_EXPERTS // te),
      in_specs=[
          pl.BlockSpec((tm, D_MODEL), lambda i, c: (i, 0)),
          pl.BlockSpec((te, D_MODEL), lambda i, c: (c, 0)),
          pl.BlockSpec((te, D_MODEL), lambda i, c: (c, 0)),
          pl.BlockSpec((tm, PEER_I1_PER_STEP, N_KEYS), lambda i, c: (i, c, 0)),
          pl.BlockSpec((tm, D_MODEL), lambda i, c: (i, 0), pipeline_mode=pl.Buffered(1)),
          pl.BlockSpec((1, D_MODEL), lambda i, c: (0, 0)),
      ],
      out_specs=pl.BlockSpec((tm, D_MODEL), lambda i, c: (i, 0)),
      out_shape=jax.ShapeDtypeStruct((t, D_MODEL), F32),
      compiler_params=_params(("parallel", "arbitrary")),
      name="peer_dense",
  )(xn2, u_bf, v_bf, gmat, h, final_g)


def _rope_tables(pos):
  half = ROT_DIM // 2
  inv_freq = jnp.float32(ROPE_THETA) ** (-jnp.arange(half, dtype=F32) * 2.0 / ROT_DIM)
  ang = pos.astype(F32)[:, None] * inv_freq[None, :]
  cos, sin = jnp.cos(ang), jnp.sin(ang)
  n = pos.shape[0]
  ones = jnp.ones((n, HEAD_DIM - ROT_DIM), F32)
  zeros = jnp.zeros((n, HEAD_DIM - ROT_DIM), F32)
  zh = jnp.zeros((n, half), F32)
  cos_c = jnp.concatenate([cos, cos, ones], axis=1)
  sa_c = jnp.concatenate([-sin, zh, zeros], axis=1)
  sb_c = jnp.concatenate([zh, sin, zeros], axis=1)
  two = lambda a: jnp.concatenate([a, a], axis=1)
  return two(cos_c), two(sa_c), two(sb_c)


def kernel(x_prompt, x_sample, cache_k, cache_v, state_conv, state_h, page_table, norm1_g, w_in, lambda_q1, lambda_k1, lambda_q2, lambda_k2, subln_g, conv_w, conv_b, lru_wa, lru_ba, lru_wx, lru_bx, lru_lambda, w_att_up, w_lru_up, w_out, norm2_g, peer_wq, peer_k1, peer_k2, peer_u, peer_v, final_g):
  batch, seq, _ = x_prompt.shape
  dec_batch, dec_seq, _ = x_sample.shape
  n_pages = page_table.shape[1]
  past_len = n_pages * PAGE_SIZE
  tp = batch * seq
  ts = dec_batch * dec_seq
  assert w_in.shape[0] == 1, "one layer"

  x_all = jnp.concatenate([x_prompt.reshape(tp, D_MODEL), x_sample.reshape(ts, D_MODEL)], axis=0)
  pos = jnp.concatenate([
      jnp.tile(jnp.arange(seq, dtype=jnp.int32), batch),
      jnp.tile(past_len + jnp.arange(dec_seq, dtype=jnp.int32), dec_batch)])
  cos_t, sa_t, sb_t = _rope_tables(pos)
  lam = (jnp.exp(jnp.sum(lambda_q1[0].astype(F32) * lambda_k1[0].astype(F32)))
         - jnp.exp(jnp.sum(lambda_q2[0].astype(F32) * lambda_k2[0].astype(F32))) + LAM_INIT).reshape(1)
  row = lambda a: a.reshape(1, -1)

  proj = _inproj(x_all, row(norm1_g[0]), w_in[0].astype(BF16), cos_t, sa_t, sb_t)

  att_p = _prompt_attention(proj, lam, row(subln_g[0]), batch, seq)
  proj_s = proj[tp:]
  q_s = proj_s[:, :OFF_Q].reshape(dec_batch, dec_seq, N_KV_HEADS, 2, 2, HEAD_DIM)
  q_s = q_s.transpose(0, 2, 4, 3, 1, 5).reshape(dec_batch, N_KV_HEADS, 2, 2 * dec_seq, HEAD_DIM)
  wq = jnp.einsum("bgcnd,ce->bgcned", q_s, jnp.eye(2, dtype=F32))
  wq = wq.reshape(dec_batch, N_KV_HEADS * 2 * 2 * dec_seq, HEAD_W).astype(BF16)
  new_rows = lambda a: a.reshape(dec_batch, dec_seq * N_KV_HEADS, HEAD_W).astype(BF16)
  k_new, v_new = proj_s[:, OFF_Q:OFF_K], proj_s[:, OFF_K:OFF_V]
  n_pool = cache_k.shape[1]
  att_s = _sample_attention(page_table, lam, wq, new_rows(k_new), new_rows(v_new), row(subln_g[0]),
                            cache_k.reshape(n_pool, PAGE_ROWS, HEAD_W),
                            cache_v.reshape(n_pool, PAGE_ROWS, HEAD_W), dec_seq)
  att_s = att_s.reshape(dec_batch, N_KV_HEADS, 2, dec_seq, HEAD_W).transpose(0, 3, 1, 2, 4)
  att_all = jnp.concatenate([att_p, att_s.reshape(ts, ATT_WIDTH).astype(BF16)], axis=0)

  lru_w = (conv_w[0], row(conv_b[0]), lru_wa[0], row(lru_ba[0]), lru_wx[0], row(lru_bx[0]), row(lru_lambda[0]))
  lru_p, h_p = _lru_prompt(proj, lru_w, batch, seq)
  xl_s = proj_s[:, OFF_V:OFF_L].reshape(dec_batch, dec_seq, LRU_WIDTH)
  lru_s, h_s = _lru_sample(xl_s.transpose(1, 0, 2), state_conv[0].transpose(1, 0, 2), state_h[0], lru_w)
  lru_all = jnp.concatenate([lru_p, lru_s.transpose(1, 0, 2).reshape(ts, LRU_WIDTH)], axis=0)

  h_all, xn2 = _merge(att_all, lru_all, proj, x_all, w_att_up[0].astype(BF16), w_lru_up[0].astype(BF16),
                      w_out[0].astype(BF16), row(norm2_g[0]))
  qp = _matmul(xn2, peer_wq[0].astype(BF16))
  eid, gate = _peer_topk(qp, peer_k1[0], peer_k2[0])
  gmat = _gate_matrix(eid.T, gate.T)
  y_all = _peer_dense(xn2, peer_u[0].astype(BF16), peer_v[0].astype(BF16), gmat, h_all, row(final_g))

  kv_shape_p = (1, batch, seq, N_KV_HEADS, HEAD_W)
  kv_shape_s = (1, dec_batch, dec_seq, N_KV_HEADS, HEAD_W)
  xl_p = proj[:tp, OFF_V:OFF_L].reshape(batch, seq, LRU_WIDTH)
  conv_s = jnp.concatenate([state_conv[0].astype(F32), xl_s], axis=1)[:, -(CONV_W - 1):]
  return (
      y_all[:tp].reshape(batch, seq, D_MODEL),
      y_all[tp:].reshape(dec_batch, dec_seq, D_MODEL),
      proj[:tp, OFF_Q:OFF_K].reshape(kv_shape_p),
      proj[:tp, OFF_K:OFF_V].reshape(kv_shape_p),
      xl_p[:, -(CONV_W - 1):][None],
      h_p.reshape(1, batch, LRU_WIDTH),
      k_new.reshape(kv_shape_s),
      v_new.reshape(kv_shape_s),
      conv_s[None],
      h_s[None],
  )
```

```python
import functools
import math

import jax
import jax.numpy as jnp
from jax import lax
from jax.experimental import pallas as pl
from jax.experimental.pallas import tpu as pltpu

F32 = jnp.float32
BF16 = jnp.bfloat16

D_MODEL = 2048
N_HEADS = 8
N_KV_HEADS = 4
HEAD_DIM = 64
HEAD_W = 2 * HEAD_DIM
ROT_DIM = HEAD_DIM // 4
ROPE_THETA = 500000.0
ATT_WIDTH = N_HEADS * HEAD_W
KV_WIDTH = N_KV_HEADS * HEAD_W
LRU_WIDTH = 1024
LRU_BLOCKS = 8
LRU_BLOCK_W = LRU_WIDTH // LRU_BLOCKS
CONV_W = 4
LRU_C = 8.0
N_KEYS = 128
KEY_BITS = 7
N_EXPERTS = N_KEYS * N_KEYS
PEER_HEADS = 8
PEER_TOPK = 16
PEER_SEL = PEER_HEADS * PEER_TOPK
PAGE_SIZE = 128
RMS_EPS = 1e-6
OFF_Q = ATT_WIDTH
OFF_K = OFF_Q + KV_WIDTH
OFF_V = OFF_K + KV_WIDTH
OFF_L = OFF_V + LRU_WIDTH
OFF_GA = OFF_L + D_MODEL
IN_WIDTH = OFF_GA + D_MODEL
LAM_INIT = 0.8 - 0.6 * math.exp(0.0)

LANES = 128
SUBLANES = 8
VMEM_LIMIT_BYTES = 56 * 1024 * 1024

NEG_INF = float("-inf")
NT_DIMS = (((1,), (1,)), ((), ()))


def _pick_block(total, cap, quantum):
  best = None
  b = quantum
  while b <= min(cap, total):
    if total % b == 0:
      best = b
    b += quantum
  assert best is not None, (total, cap, quantum)
  return best


def _params(sem, vmem=VMEM_LIMIT_BYTES):
  return pltpu.CompilerParams(dimension_semantics=sem, vmem_limit_bytes=vmem)


def _rms(x, g):
  var = jnp.mean(x * x, axis=-1, keepdims=True)
  return x * lax.rsqrt(var + RMS_EPS) * g


def _inproj_body(x_ref, g_ref, w_ref, cos_ref, sa_ref, sb_ref, o_ref, xn_ref, *, tn, n_rope_blocks):
  j = pl.program_id(1)

  @pl.when(j == 0)
  def _():
    xn_ref[...] = _rms(x_ref[...], g_ref[...]).astype(BF16)

  acc = jnp.dot(xn_ref[...], w_ref[...], preferred_element_type=F32)

  @pl.when(j < n_rope_blocks)
  def _():
    reps = tn // LANES
    cos = jnp.concatenate([cos_ref[...]] * reps, axis=1)
    sa = jnp.concatenate([sa_ref[...]] * reps, axis=1)
    sb = jnp.concatenate([sb_ref[...]] * reps, axis=1)
    half = ROT_DIM // 2
    rot = acc * cos + pltpu.roll(acc, tn - half, 1) * sa + pltpu.roll(acc, half, 1) * sb
    col = j * tn + lax.broadcasted_iota(jnp.int32, acc.shape, 1)
    o_ref[...] = jnp.where(col < OFF_Q, rot * (HEAD_DIM ** -0.5), jnp.where(col < OFF_K, rot, acc))

  @pl.when(j >= n_rope_blocks)
  def _():
    o_ref[...] = acc


def _inproj(x_all, norm_g, w_in_bf, cos_t, sa_t, sb_t):
  t = x_all.shape[0]
  tm = _pick_block(t, 640, LANES)
  tn = 1024
  body = functools.partial(_inproj_body, tn=tn, n_rope_blocks=pl.cdiv(OFF_K, tn))
  return pl.pallas_call(
      body,
      grid=(t // tm, IN_WIDTH // tn),
      in_specs=[
          pl.BlockSpec((tm, D_MODEL), lambda i, j: (i, 0)),
          pl.BlockSpec((1, D_MODEL), lambda i, j: (0, 0)),
          pl.BlockSpec((D_MODEL, tn), lambda i, j: (0, j)),
          pl.BlockSpec((tm, LANES), lambda i, j: (i, 0)),
          pl.BlockSpec((tm, LANES), lambda i, j: (i, 0)),
          pl.BlockSpec((tm, LANES), lambda i, j: (i, 0)),
      ],
      out_specs=pl.BlockSpec((tm, tn), lambda i, j: (i, j)),
      out_shape=jax.ShapeDtypeStruct((t, IN_WIDTH), F32),
      scratch_shapes=[pltpu.VMEM((tm, D_MODEL), BF16)],
      compiler_params=_params(("parallel", "arbitrary")),
      name="inproj",
  )(x_all, norm_g, w_in_bf, cos_t, sa_t, sb_t)


def _subln(o, g):
  var = jnp.mean(o * o, axis=-1, keepdims=True)
  return o * lax.rsqrt(var + RMS_EPS) * g * (1.0 - LAM_INIT)


def _pattn_body(qi_ref, ki_ref, lam_ref, q_ref, k_ref, v_ref, g_ref, o_ref, q4_ref, m_ref, l_ref, acc_ref,
                *, tq, tk):
  qi = qi_ref[pl.program_id(2)]
  ki = ki_ref[pl.program_id(2)]
  cols = 2 * 2 * tq

  @pl.when(ki == 0)
  def _():
    m_ref[...] = jnp.full((1, cols), NEG_INF, F32)
    l_ref[...] = jnp.zeros((1, cols), F32)
    acc_ref[...] = jnp.zeros((HEAD_W, cols), F32)
    lane = lax.broadcasted_iota(jnp.int32, (tq, HEAD_W), 1)
    for r in range(2):
      qh = q_ref[:, r * HEAD_W:(r + 1) * HEAD_W]
      q4_ref[(2 * r) * tq:(2 * r + 1) * tq, :] = jnp.where(lane < HEAD_DIM, qh, 0.0).astype(BF16)
      q4_ref[(2 * r + 1) * tq:(2 * r + 2) * tq, :] = jnp.where(lane >= HEAD_DIM, qh, 0.0).astype(BF16)

  def update(diagonal):
    k = k_ref[...].astype(BF16)
    s = lax.dot_general(k, q4_ref[...], NT_DIMS, preferred_element_type=F32)
    if diagonal:
      visible = (lax.broadcasted_iota(jnp.int32, (tk, tq), 0) <= lax.broadcasted_iota(jnp.int32, (tk, tq), 1))
      s = jnp.concatenate([jnp.where(visible, s[:, j * tq:(j + 1) * tq], NEG_INF) for j in range(4)], axis=1)
    m_old = m_ref[...]
    m_new = jnp.maximum(m_old, jnp.max(s, axis=0, keepdims=True))
    alpha = jnp.exp(m_old - m_new)
    p = jnp.exp(s - m_new)
    l_ref[...] = alpha * l_ref[...] + jnp.sum(p, axis=0, keepdims=True)
    vt = v_ref[...].T.astype(BF16)
    acc_ref[...] = alpha * acc_ref[...] + jnp.dot(vt, p.astype(BF16), preferred_element_type=F32)
    m_ref[...] = m_new

  @pl.when(ki < qi)
  def _():
    update(False)

  @pl.when(ki == qi)
  def _():
    update(True)
    lam = lam_ref[0]
    o = acc_ref[...] * (1.0 / l_ref[...])
    for r in range(2):
      d = o[:, (2 * r) * tq:(2 * r + 1) * tq] - lam * o[:, (2 * r + 1) * tq:(2 * r + 2) * tq]
      var = jnp.mean(d * d, axis=0, keepdims=True)
      dn = d * (lax.rsqrt(var + RMS_EPS) * (1.0 - LAM_INIT))
      o_ref[:, r * HEAD_W:(r + 1) * HEAD_W] = (dn.T * g_ref[...]).astype(o_ref.dtype)


def _prompt_attention(proj, lam, subln_g, batch, seq):
  tq = tk = _pick_block(seq, 512, LANES)
  nq = seq // tq
  gw = 2 * HEAD_W
  body = functools.partial(_pattn_body, tq=tq, tk=tk)
  pairs = [(qi, ki) for qi in range(nq) for ki in range(qi + 1)]
  qi_of = jnp.asarray([p[0] for p in pairs], jnp.int32)
  ki_of = jnp.asarray([p[1] for p in pairs], jnp.int32)
  grid_spec = pltpu.PrefetchScalarGridSpec(
      num_scalar_prefetch=2,
      grid=(batch, N_KV_HEADS, len(pairs)),
      in_specs=[
          pl.BlockSpec(memory_space=pltpu.SMEM),
          pl.BlockSpec((tq, gw), lambda b, g, p, qo, ko: (b * nq + qo[p], g)),
          pl.BlockSpec((tk, HEAD_W), lambda b, g, p, qo, ko: (b * nq + ko[p], OFF_Q // HEAD_W + g)),
          pl.BlockSpec((tk, HEAD_W), lambda b, g, p, qo, ko: (b * nq + ko[p], OFF_K // HEAD_W + g)),
          pl.BlockSpec((1, HEAD_W), lambda b, g, p, qo, ko: (0, 0)),
      ],
      out_specs=pl.BlockSpec((tq, gw), lambda b, g, p, qo, ko: (b * nq + qo[p], g)),
      scratch_shapes=[
          pltpu.VMEM((4 * tq, HEAD_W), BF16),
          pltpu.VMEM((1, 4 * tq), F32),
          pltpu.VMEM((1, 4 * tq), F32),
          pltpu.VMEM((HEAD_W, 4 * tq), F32),
      ],
  )
  return pl.pallas_call(
      body,
      grid_spec=grid_spec,
      out_shape=jax.ShapeDtypeStruct((batch * seq, ATT_WIDTH), BF16),
      compiler_params=_params(("parallel", "parallel", "arbitrary")),
      name="prompt_attention",
  )(qi_of, ki_of, lam, proj, proj, proj, subln_g)


PAGES_PER_CHUNK = 8
PAGE_ROWS = PAGE_SIZE * N_KV_HEADS


def _sattn_body(pt_ref, lam_ref, wq_ref, kn_ref, vn_ref, g_ref, ck_ref, cv_ref, o_ref,
                kbuf, vbuf, bias_ref, sem, *, n_chunks, dec_seq):
  b = pl.program_id(0)
  nb = pl.num_programs(0)
  rows = wq_ref.shape[1]
  gr = rows // N_KV_HEADS
  cols = PAGES_PER_CHUNK * PAGE_ROWS

  def copies(bb, c, slot):
    out = []
    for p in range(PAGES_PER_CHUNK):
      page = pt_ref[bb, c * PAGES_PER_CHUNK + p]
      out.append(pltpu.make_async_copy(ck_ref.at[page], kbuf.at[slot, p], sem.at[0, slot]))
      out.append(pltpu.make_async_copy(cv_ref.at[page], vbuf.at[slot, p], sem.at[1, slot]))
    return out

  def start(bb, c, slot):
    for n, cp in enumerate(copies(bb, c, slot)):
      cp.start(priority=n % 2)

  def same_head(shape):
    head_of_col = jnp.bitwise_and(lax.broadcasted_iota(jnp.int32, shape, 1), N_KV_HEADS - 1)
    head_of_row = lax.broadcasted_iota(jnp.int32, shape, 0) // gr
    return head_of_col == head_of_row

  @pl.when(b == 0)
  def _():
    start(0, 0, 0)
    bias_ref[...] = jnp.where(same_head((rows, cols)), 0.0, NEG_INF)

  wq = wq_ref[0]

  def softmax_step(carry, s, v):
    m_old, l_old, acc = carry
    m_new = jnp.maximum(m_old, jnp.max(s, axis=1, keepdims=True))
    alpha = jnp.exp(m_old - m_new)
    p = jnp.exp(s - m_new)
    l_new = alpha * l_old + jnp.sum(p, axis=1, keepdims=True)
    acc = alpha * acc + jnp.dot(p.astype(BF16), v, preferred_element_type=F32)
    return m_new, l_new, acc

  def chunk(c, carry):
    slot = c % 2

    @pl.when(c + 1 < n_chunks)
    def _():
      start(b, c + 1, 1 - slot)

    @pl.when(jnp.logical_and(c + 1 == n_chunks, b + 1 < nb))
    def _():
      start(b + 1, 0, 1 - slot)

    for cp in copies(b, c, slot):
      cp.wait()
    kc = kbuf[slot].reshape(cols, HEAD_W).astype(BF16)
    vc = vbuf[slot].reshape(cols, HEAD_W).astype(BF16)
    s = lax.dot_general(wq, kc, NT_DIMS, preferred_element_type=F32) + bias_ref[...]
    return softmax_step(carry, s, vc)

  init = (jnp.full((rows, 1), NEG_INF, F32), jnp.zeros((rows, 1), F32), jnp.zeros((rows, HEAD_W), F32))
  carry = lax.fori_loop(0, n_chunks, chunk, init)

  new_rows = kn_ref.shape[1]
  s = lax.dot_general(wq, kn_ref[0], NT_DIMS, preferred_element_type=F32)
  t_of_row = lax.broadcasted_iota(jnp.int32, (rows, new_rows), 0) % dec_seq
  t_of_col = lax.broadcasted_iota(jnp.int32, (rows, new_rows), 1) // N_KV_HEADS
  visible = jnp.logical_and(same_head((rows, new_rows)), t_of_col <= t_of_row)
  _, l_fin, acc = softmax_step(carry, jnp.where(visible, s, NEG_INF), vn_ref[0])

  o = acc / l_fin
  lam = lam_ref[0]
  for g in range(N_KV_HEADS):
    blk = o[g * gr:(g + 1) * gr]
    d = blk[:gr // 2] - lam * blk[gr // 2:]
    o_ref[0, g * (gr // 2):(g + 1) * (gr // 2), :] = _subln(d, g_ref[...])


def _sample_attention(page_table, lam, wq, k_new, v_new, subln_g, cache_k, cache_v, dec_seq):
  dec_batch, n_pages = page_table.shape
  assert n_pages % (2 * PAGES_PER_CHUNK) == 0
  n_chunks = n_pages // PAGES_PER_CHUNK
  rows = wq.shape[1]
  new_rows = k_new.shape[1]
  body = functools.partial(_sattn_body, n_chunks=n_chunks, dec_seq=dec_seq)
  grid_spec = pltpu.PrefetchScalarGridSpec(
      num_scalar_prefetch=1,
      grid=(dec_batch,),
      in_specs=[
          pl.BlockSpec(memory_space=pltpu.SMEM),
          pl.BlockSpec((1, rows, HEAD_W), lambda b, pt: (b, 0, 0)),
          pl.BlockSpec((1, new_rows, HEAD_W), lambda b, pt: (b, 0, 0)),
          pl.BlockSpec((1, new_rows, HEAD_W), lambda b, pt: (b, 0, 0)),
          pl.BlockSpec((1, HEAD_W), lambda b, pt: (0, 0)),
          pl.BlockSpec(memory_space=pl.ANY),
          pl.BlockSpec(memory_space=pl.ANY),
      ],
      out_specs=pl.BlockSpec((1, rows // 2, HEAD_W), lambda b, pt: (b, 0, 0)),
      scratch_shapes=[
          pltpu.VMEM((2, PAGES_PER_CHUNK, PAGE_ROWS, HEAD_W), F32),
          pltpu.VMEM((2, PAGES_PER_CHUNK, PAGE_ROWS, HEAD_W), F32),
          pltpu.VMEM((rows, PAGES_PER_CHUNK * PAGE_ROWS), F32),
          pltpu.SemaphoreType.DMA((2, 2)),
      ],
  )
  return pl.pallas_call(
      body,
      grid_spec=grid_spec,
      out_shape=jax.ShapeDtypeStruct((dec_batch, rows // 2, HEAD_W), F32),
      compiler_params=_params(("arbitrary",)),
      name="sample_attention",
  )(page_table, lam, wq, k_new, v_new, subln_g, cache_k, cache_v)


def _lru_gates(xc, wa_ref, ba, wx_ref, bx, lam):
  ra, ix = [], []
  for n in range(LRU_BLOCKS):
    xb = xc[:, n * LRU_BLOCK_W:(n + 1) * LRU_BLOCK_W]
    ra.append(jnp.dot(xb, wa_ref[n], preferred_element_type=F32, precision=lax.Precision.HIGHEST))
    ix.append(jnp.dot(xb, wx_ref[n], preferred_element_type=F32, precision=lax.Precision.HIGHEST))
  r = jax.nn.sigmoid(jnp.concatenate(ra, axis=1) + ba)
  i = jax.nn.sigmoid(jnp.concatenate(ix, axis=1) + bx)
  neg = -lam
  softplus = jnp.maximum(neg, 0.0) + jnp.log1p(jnp.exp(-jnp.abs(neg)))
  log_a = -LRU_C * r * softplus
  a = jnp.exp(log_a)
  u = jnp.sqrt(1.0 - jnp.exp(2.0 * log_a)) * (i * xc)
  return a, u


def _conv(rows_of, cw_ref, cb):
  out = rows_of(0) * cw_ref[0:1, :]
  for j in range(1, CONV_W):
    out = out + rows_of(j) * cw_ref[j:j + 1, :]
  return out + cb


def _lru_prompt_body(xl_ref, cw_ref, cb_ref, wa_ref, ba_ref, wx_ref, bx_ref, lam_ref, y_ref, hl_ref,
                     ext_ref, a_ref, u_ref, hs_ref, h_ref, *, tt):
  ti = pl.program_id(1)
  head = SUBLANES

  @pl.when(ti == 0)
  def _():
    ext_ref[0:head, :] = jnp.zeros((head, LRU_WIDTH), F32)
    h_ref[...] = jnp.zeros((1, LRU_WIDTH), F32)

  ext_ref[head:head + tt, :] = xl_ref[...]
  xc = _conv(lambda j: ext_ref[head - (CONV_W - 1) + j:head - (CONV_W - 1) + j + tt, :], cw_ref, cb_ref[...])
  a, u = _lru_gates(xc, wa_ref, ba_ref[...], wx_ref, bx_ref[...], lam_ref[...])
  a_ref[...] = a
  u_ref[...] = u

  def step(t, h):
    h = a_ref[pl.ds(t, 1), :] * h + u_ref[pl.ds(t, 1), :]
    hs_ref[pl.ds(t, 1), :] = h
    return h

  h_fin = lax.fori_loop(0, tt, step, h_ref[...], unroll=8)
  h_ref[...] = h_fin
  y_ref[...] = hs_ref[...].astype(y_ref.dtype)
  ext_ref[0:head, :] = ext_ref[tt:tt + head, :]

  @pl.when(ti == pl.num_programs(1) - 1)
  def _():
    hl_ref[0] = h_fin


def _lru_weight_specs(nidx):
  zero2 = (lambda *a: (0, 0))
  zero3 = (lambda *a: (0, 0, 0))
  del nidx
  return [
      pl.BlockSpec((CONV_W, LRU_WIDTH), zero2),
      pl.BlockSpec((1, LRU_WIDTH), zero2),
      pl.BlockSpec((LRU_BLOCKS, LRU_BLOCK_W, LRU_BLOCK_W), zero3),
      pl.BlockSpec((1, LRU_WIDTH), zero2),
      pl.BlockSpec((LRU_BLOCKS, LRU_BLOCK_W, LRU_BLOCK_W), zero3),
      pl.BlockSpec((1, LRU_WIDTH), zero2),
      pl.BlockSpec((1, LRU_WIDTH), zero2),
  ]


def _lru_prompt(proj, lru_w, batch, seq):
  tt = _pick_block(seq, 512, LANES)
  nt = seq // tt
  body = functools.partial(_lru_prompt_body, tt=tt)
  return pl.pallas_call(
      body,
      grid=(batch, nt),
      in_specs=[pl.BlockSpec((tt, LRU_WIDTH), lambda b, ti: (b * nt + ti, OFF_V // LRU_WIDTH))]
      + _lru_weight_specs(2),
      out_specs=[
          pl.BlockSpec((tt, LRU_WIDTH), lambda b, ti: (b * nt + ti, 0)),
          pl.BlockSpec((1, 1, LRU_WIDTH), lambda b, ti: (b, 0, 0)),
      ],
      out_shape=[
          jax.ShapeDtypeStruct((batch * seq, LRU_WIDTH), BF16),
          jax.ShapeDtypeStruct((batch, 1, LRU_WIDTH), F32),
      ],
      scratch_shapes=[
          pltpu.VMEM((tt + 2 * SUBLANES, LRU_WIDTH), F32),
          pltpu.VMEM((tt, LRU_WIDTH), F32),
          pltpu.VMEM((tt, LRU_WIDTH), F32),
          pltpu.VMEM((tt, LRU_WIDTH), F32),
          pltpu.VMEM((1, LRU_WIDTH), F32),
      ],
      compiler_params=_params(("parallel", "arbitrary")),
      name="lru_prompt",
  )(proj, *lru_w)


def _lru_sample_body(xl_ref, cbuf_ref, h0_ref, cw_ref, cb_ref, wa_ref, ba_ref, wx_ref, bx_ref, lam_ref,
                     y_ref, hl_ref, *, dec_seq):
  ext = [cbuf_ref[j] for j in range(CONV_W - 1)] + [xl_ref[t] for t in range(dec_seq)]
  h = h0_ref[...]
  for t in range(dec_seq):
    xc = _conv(lambda j: ext[t + j], cw_ref, cb_ref[...])
    a, u = _lru_gates(xc, wa_ref, ba_ref[...], wx_ref, bx_ref[...], lam_ref[...])
    h = a * h + u
    y_ref[t] = h.astype(y_ref.dtype)
  hl_ref[...] = h


def _lru_sample(xl_t, cbuf_t, h0, lru_w):
  dec_seq, dec_batch, _ = xl_t.shape
  body = functools.partial(_lru_sample_body, dec_seq=dec_seq)
  return pl.pallas_call(
      body,
      out_shape=[
          jax.ShapeDtypeStruct((dec_seq, dec_batch, LRU_WIDTH), BF16),
          jax.ShapeDtypeStruct((dec_batch, LRU_WIDTH), F32),
      ],
      name="lru_sample",
  )(xl_t, cbuf_t, h0, *lru_w)


def _merge_body(att_ref, lru_ref, ga0_ref, ga1_ref, gl0_ref, gl1_ref, x_ref, wa_ref, wl_ref, wo_ref, g2_ref,
                h_ref, xn_ref):
  a1 = jnp.dot(att_ref[...], wa_ref[...], preferred_element_type=F32)
  a2 = jnp.dot(lru_ref[...], wl_ref[...], preferred_element_type=F32)
  ga = jnp.concatenate([ga0_ref[...], ga1_ref[...]], axis=1)
  gl = jnp.concatenate([gl0_ref[...], gl1_ref[...]], axis=1)
  m = jax.nn.sigmoid(ga) * a1 + jax.nn.sigmoid(gl) * a2
  h = x_ref[...] + jnp.dot(m.astype(BF16), wo_ref[...], preferred_element_type=F32)
  h_ref[...] = h
  xn_ref[...] = _rms(h, g2_ref[...]).astype(BF16)


def _merge(att, lru, proj, x_all, wa, wl, wo, g2):
  t = x_all.shape[0]
  tm = _pick_block(t, 320, 64)
  half = D_MODEL // 2
  const = lambda shape: pl.BlockSpec(shape, lambda i: (0, 0), pipeline_mode=pl.Buffered(1))
  gate = lambda blk: pl.BlockSpec((tm, half), lambda i: (i, blk))
  return pl.pallas_call(
      _merge_body,
      grid=(t // tm,),
      in_specs=[
          pl.BlockSpec((tm, ATT_WIDTH), lambda i: (i, 0)),
          pl.BlockSpec((tm, LRU_WIDTH), lambda i: (i, 0)),
          gate(OFF_L // half), gate(OFF_L // half + 1), gate(OFF_GA // half), gate(OFF_GA // half + 1),
          pl.BlockSpec((tm, D_MODEL), lambda i: (i, 0)),
          const((ATT_WIDTH, D_MODEL)), const((LRU_WIDTH, D_MODEL)), const((D_MODEL, D_MODEL)),
          const((1, D_MODEL)),
      ],
      out_specs=[
          pl.BlockSpec((tm, D_MODEL), lambda i: (i, 0)),
          pl.BlockSpec((tm, D_MODEL), lambda i: (i, 0)),
      ],
      out_shape=[
          jax.ShapeDtypeStruct((t, D_MODEL), F32),
          jax.ShapeDtypeStruct((t, D_MODEL), BF16),
      ],
      compiler_params=_params(("parallel",)),
      name="merge",
  )(att, lru, proj, proj, proj, proj, x_all, wa, wl, wo, g2)


def _mm_body(x_ref, w_ref, o_ref):
  o_ref[...] = jnp.dot(x_ref[...], w_ref[...], preferred_element_type=F32)


def _matmul(x, w):
  t, kdim = x.shape
  n = w.shape[1]
  tm = _pick_block(t, 640, LANES)
  tn = 512
  return pl.pallas_call(
      _mm_body,
      grid=(t // tm, n // tn),
      in_specs=[pl.BlockSpec((tm, kdim), lambda i, j: (i, 0)), pl.BlockSpec((kdim, tn), lambda i, j: (0, j))],
      out_specs=pl.BlockSpec((tm, tn), lambda i, j: (i, j)),
      out_shape=jax.ShapeDtypeStruct((t, n), F32),
      compiler_params=_params(("parallel", "arbitrary")),
      name="peer_query",
  )(x, w)


def _top16_rows(s, val_ref, idx_ref, lane0):
  n, w = s.shape
  sub = lax.broadcasted_iota(jnp.int32, s.shape, 0).astype(F32)
  for k in range(PEER_TOPK):
    m = jnp.max(s, axis=0, keepdims=True)
    idx = jnp.min(jnp.where(s == m, sub, float(n)), axis=0, keepdims=True)
    val_ref[k:k + 1, lane0:lane0 + w] = m
    idx_ref[k:k + 1, lane0:lane0 + w] = idx
    s = jnp.where(sub == idx, NEG_INF, s)


PEER_HEADS_PER_ITER = 2


def _topk_body(q_ref, k1_ref, k2_ref, eid_ref, gate_ref, val_ref, idx_ref, best_ref, sel_ref):
  tb = q_ref.shape[0]
  kk = PEER_TOPK
  half_w = N_KEYS
  hp = PEER_HEADS_PER_ITER
  wide = hp * tb

  def head_group(hg, carry):
    for hh in range(hp):
      scores = []
      for c, key_ref in enumerate((k1_ref, k2_ref)):
        col = pl.multiple_of((2 * (hg * hp + hh) + c) * half_w, half_w)
        qh = q_ref[:, pl.ds(col, half_w)]
        scores.append(lax.dot_general(key_ref[...], qh, NT_DIMS, preferred_element_type=F32,
                                      precision=lax.Precision.HIGHEST))
      _top16_rows(jnp.concatenate(scores, axis=1), val_ref, idx_ref, hh * 2 * tb)

    def pick(ref, c):
      return jnp.concatenate([ref[:, (2 * hh + c) * tb:(2 * hh + c + 1) * tb] for hh in range(hp)], axis=1)

    v1, v2 = pick(val_ref, 0), pick(val_ref, 1)
    i1, i2 = pick(idx_ref, 0), pick(idx_ref, 1)
    b16 = lax.broadcasted_iota(jnp.int32, (kk, wide), 0).astype(F32)
    b8 = lax.broadcasted_iota(jnp.int32, (SUBLANES, wide), 0).astype(F32)
    vals = [v1[0:1] + v2]
    flat = [b16]
    code = [i1[0:1] * N_KEYS + i2]
    for a in range(1, SUBLANES):
      vals.append(v1[a:a + 1] + v2[0:SUBLANES])
      flat.append(a * kk + b8)
      code.append(i1[a:a + 1] * N_KEYS + i2[0:SUBLANES])
    vals.append(v1[SUBLANES:kk] + v2[0:1])
    flat.append((b8 + SUBLANES) * kk)
    code.append(i1[SUBLANES:kk] * N_KEYS + i2[0:1])
    cand = jnp.concatenate(vals, axis=0)
    flat = jnp.concatenate(flat, axis=0)
    code = jnp.concatenate(code, axis=0)
    for k in range(kk):
      m = jnp.max(cand, axis=0, keepdims=True)
      fsel = jnp.min(jnp.where(cand == m, flat, float(kk * kk)), axis=0, keepdims=True)
      hit = flat == fsel
      best_ref[k:k + 1, :] = m
      sel_ref[k:k + 1, :] = jnp.max(jnp.where(hit, code, -1.0), axis=0, keepdims=True)
      cand = jnp.where(hit, NEG_INF, cand)
    best = best_ref[...]
    e = jnp.exp(best - best[0:1])
    gate = e / jnp.sum(e, axis=0, keepdims=True)
    eid = sel_ref[...].astype(jnp.int32)
    for hh in range(hp):
      row = pl.multiple_of((hg * hp + hh) * kk, kk)
      gate_ref[pl.ds(row, kk), :] = gate[:, hh * tb:(hh + 1) * tb]
      eid_ref[pl.ds(row, kk), :] = eid[:, hh * tb:(hh + 1) * tb]
    return carry

  lax.fori_loop(0, PEER_HEADS // hp, head_group, 0)


def _peer_topk(qp, k1, k2):
  t = qp.shape[0]
  tb = LANES
  return pl.pallas_call(
      _topk_body,
      grid=(t // tb,),
      in_specs=[
          pl.BlockSpec((tb, qp.shape[1]), lambda i: (i, 0)),
          pl.BlockSpec((N_KEYS, N_KEYS), lambda i: (0, 0)),
          pl.BlockSpec((N_KEYS, N_KEYS), lambda i: (0, 0)),
      ],
      out_specs=[
          pl.BlockSpec((PEER_SEL, tb), lambda i: (0, i)),
          pl.BlockSpec((PEER_SEL, tb), lambda i: (0, i)),
      ],
      out_shape=[
          jax.ShapeDtypeStruct((PEER_SEL, t), jnp.int32),
          jax.ShapeDtypeStruct((PEER_SEL, t), F32),
      ],
      scratch_shapes=[
          pltpu.VMEM((PEER_TOPK, 2 * PEER_HEADS_PER_ITER * tb), F32),
          pltpu.VMEM((PEER_TOPK, 2 * PEER_HEADS_PER_ITER * tb), F32),
          pltpu.VMEM((PEER_TOPK, PEER_HEADS_PER_ITER * tb), F32),
          pltpu.VMEM((PEER_TOPK, PEER_HEADS_PER_ITER * tb), F32),
      ],
      compiler_params=_params(("parallel",)),
      name="peer_topk",
  )(qp, k1, k2)


def _gate_matrix_body(eid_ref, gate_ref, g_ref):
  tb = eid_ref.shape[0]
  sub = lax.broadcasted_iota(jnp.int32, (N_KEYS, PEER_SEL), 0)

  def token(t, carry):
    e = eid_ref[pl.ds(t, 1), :]
    gt = gate_ref[pl.ds(t, 1), :]
    at = jnp.where(sub == lax.shift_right_logical(e, KEY_BITS), gt, 0.0).astype(BF16)
    bt = jnp.where(sub == jnp.bitwise_and(e, N_KEYS - 1), 1.0, 0.0).astype(BF16)
    g_ref[t] = lax.dot_general(at, bt, NT_DIMS, preferred_element_type=F32)
    return carry

  lax.fori_loop(0, tb, token, 0, unroll=8)


def _gate_matrix(eid_t, gate_t):
  t = eid_t.shape[0]
  tb = _pick_block(t, 64, SUBLANES)
  return pl.pallas_call(
      _gate_matrix_body,
      grid=(t // tb,),
      in_specs=[pl.BlockSpec((tb, PEER_SEL), lambda i: (i, 0)), pl.BlockSpec((tb, PEER_SEL), lambda i: (i, 0))],
      out_specs=pl.BlockSpec((tb, N_KEYS, N_KEYS), lambda i: (i, 0, 0)),
      out_shape=jax.ShapeDtypeStruct((t, N_KEYS, N_KEYS), F32),
      compiler_params=_params(("parallel",)),
      name="peer_gate_matrix",
  )(eid_t, gate_t)


PEER_I1_PER_STEP = 8


def _peer_dense_body(x_ref, u_ref, v_ref, g_ref, h_ref, fg_ref, y_ref):
  c = pl.program_id(1)
  s = lax.dot_general(x_ref[...], u_ref[...], NT_DIMS, preferred_element_type=F32)
  act = 0.5 * s * (1.0 + lax.erf(s * (2.0 ** -0.5)))
  tm = g_ref.shape[0]
  g = jnp.swapaxes(g_ref[...].reshape(tm // SUBLANES, SUBLANES, PEER_I1_PER_STEP, N_KEYS), 1, 2)
  coef = jnp.concatenate(
      [g[:, j].reshape(tm, N_KEYS) * act[:, j * N_KEYS:(j + 1) * N_KEYS] for j in range(PEER_I1_PER_STEP)],
      axis=1)
  contrib = jnp.dot(coef.astype(BF16), v_ref[...], preferred_element_type=F32)

  @pl.when(c == 0)
  def _():
    y_ref[...] = contrib

  @pl.when(c > 0)
  def _():
    y_ref[...] += contrib

  @pl.when(c == pl.num_programs(1) - 1)
  def _():
    y_ref[...] = _rms(h_ref[...] + y_ref[...], fg_ref[...])


def _peer_dense(xn2, u_bf, v_bf, gmat, h, final_g):
  t = xn2.shape[0]
  tm = _pick_block(t, 640, LANES)
  te = PEER_I1_PER_STEP * N_KEYS
  return pl.pallas_call(
      _peer_dense_body,
      grid=(t // tm, N_EXPERTS // te),
      in_specs=[
          pl.BlockSpec((tm, D_MODEL), lambda i, c: (i, 0)),
          pl.BlockSpec((te, D_MODEL), lambda i, c: (c, 0)),
          pl.BlockSpec((te, D_MODEL), lambda i, c: (c, 0)),
          pl.BlockSpec((tm, PEER_I1_PER_STEP, N_KEYS), lambda i, c: (i, c, 0)),
          pl.BlockSpec((tm, D_MODEL), lambda i, c: (i, 0), pipeline_mode=pl.Buffered(1)),
          pl.BlockSpec((1, D_MODEL), lambda i, c: (0, 0)),
      ],
      out_specs=pl.BlockSpec((tm, D_MODEL), lambda i, c: (i, 0)),
      out_shape=jax.ShapeDtypeStruct((t, D_MODEL), F32),
      compiler_params=_params(("parallel", "arbitrary")),
      name="peer_dense",
  )(xn2, u_bf, v_bf, gmat, h, final_g)


def _rope_tables(pos):
  half = ROT_DIM // 2
  inv_freq = jnp.float32(ROPE_THETA) ** (-jnp.arange(half, dtype=F32) * 2.0 / ROT_DIM)
  ang = pos.astype(F32)[:, None] * inv_freq[None, :]
  cos, sin = jnp.cos(ang), jnp.sin(ang)
  n = pos.shape[0]
  ones = jnp.ones((n, HEAD_DIM - ROT_DIM), F32)
  zeros = jnp.zeros((n, HEAD_DIM - ROT_DIM), F32)
  zh = jnp.zeros((n, half), F32)
  cos_c = jnp.concatenate([cos, cos, ones], axis=1)
  sa_c = jnp.concatenate([-sin, zh, zeros], axis=1)
  sb_c = jnp.concatenate([zh, sin, zeros], axis=1)
  two = lambda a: jnp.concatenate([a, a], axis=1)
  return two(cos_c), two(sa_c), two(sb_c)


def kernel(x_prompt, x_sample, cache_k, cache_v, state_conv, state_h, page_table, norm1_g, w_in, lambda_q1, lambda_k1, lambda_q2, lambda_k2, subln_g, conv_w, conv_b, lru_wa, lru_ba, lru_wx, lru_bx, lru_lambda, w_att_up, w_lru_up, w_out, norm2_g, peer_wq, peer_k1, peer_k2, peer_u, peer_v, final_g):
  batch, seq, _ = x_prompt.shape
  dec_batch, dec_seq, _ = x_sample.shape
  n_pages = page_table.shape[1]
  past_len = n_pages * PAGE_SIZE
  tp = batch * seq
  ts = dec_batch * dec_seq
  assert w_in.shape[0] == 1, "one layer"

  x_all = jnp.concatenate([x_prompt.reshape(tp, D_MODEL), x_sample.reshape(ts, D_MODEL)], axis=0)
  pos = jnp.concatenate([
      jnp.tile(jnp.arange(seq, dtype=jnp.int32), batch),
      jnp.tile(past_len + jnp.arange(dec_seq, dtype=jnp.int32), dec_batch)])
  cos_t, sa_t, sb_t = _rope_tables(pos)
  lam = (jnp.exp(jnp.sum(lambda_q1[0].astype(F32) * lambda_k1[0].astype(F32)))
         - jnp.exp(jnp.sum(lambda_q2[0].astype(F32) * lambda_k2[0].astype(F32))) + LAM_INIT).reshape(1)
  row = lambda a: a.reshape(1, -1)

  proj = _inproj(x_all, row(norm1_g[0]), w_in[0].astype(BF16), cos_t, sa_t, sb_t)

  att_p = _prompt_attention(proj, lam, row(subln_g[0]), batch, seq)
  proj_s = proj[tp:]
  q_s = proj_s[:, :OFF_Q].reshape(dec_batch, dec_seq, N_KV_HEADS, 2, 2, HEAD_DIM)
  q_s = q_s.transpose(0, 2, 4, 3, 1, 5).reshape(dec_batch, N_KV_HEADS, 2, 2 * dec_seq, HEAD_DIM)
  wq = jnp.einsum("bgcnd,ce->bgcned", q_s, jnp.eye(2, dtype=F32))
  wq = wq.reshape(dec_batch, N_KV_HEADS * 2 * 2 * dec_seq, HEAD_W).astype(BF16)
  new_rows = lambda a: a.reshape(dec_batch, dec_seq * N_KV_HEADS, HEAD_W).astype(BF16)
  k_new, v_new = proj_s[:, OFF_Q:OFF_K], proj_s[:, OFF_K:OFF_V]
  n_pool = cache_k.shape[1]
  att_s = _sample_attention(page_table, lam, wq, new_rows(k_new), new_rows(v_new), row(subln_g[0]),
                            cache_k.reshape(n_pool, PAGE_ROWS, HEAD_W),
                            cache_v.reshape(n_pool, PAGE_ROWS, HEAD_W), dec_seq)
  att_s = att_s.reshape(dec_batch, N_KV_HEADS, 2, dec_seq, HEAD_W).transpose(0, 3, 1, 2, 4)
  att_all = jnp.concatenate([att_p, att_s.reshape(ts, ATT_WIDTH).astype(BF16)], axis=0)

  lru_w = (conv_w[0], row(conv_b[0]), lru_wa[0], row(lru_ba[0]), lru_wx[0], row(lru_bx[0]), row(lru_lambda[0]))
  lru_p, h_p = _lru_prompt(proj, lru_w, batch, seq)
  xl_s = proj_s[:, OFF_V:OFF_L].reshape(dec_batch, dec_seq, LRU_WIDTH)
  lru_s, h_s = _lru_sample(xl_s.transpose(1, 0, 2), state_conv[0].transpose(1, 0, 2), state_h[0], lru_w)
  lru_all = jnp.concatenate([lru_p, lru_s.transpose(1, 0, 2).reshape(ts, LRU_WIDTH)], axis=0)

  h_all, xn2 = _merge(att_all, lru_all, proj, x_all, w_att_up[0].astype(BF16), w_lru_up[0].astype(BF16),
                      w_out[0].astype(BF16), row(norm2_g[0]))
  qp = _matmul(xn2, peer_wq[0].astype(BF16))
  eid, gate = _peer_topk(qp, peer_k1[0], peer_k2[0])
  gmat = _gate_matrix(eid.T, gate.T)
  y_all = _peer_dense(xn2, peer_u[0].astype(BF16), peer_v[0].astype(BF16), gmat, h_all, row(final_g))

  kv_shape_p = (1, batch, seq, N_KV_HEADS, HEAD_W)
  kv_shape_s = (1, dec_batch, dec_seq, N_KV_HEADS, HEAD_W)
  xl_p = proj[:tp, OFF_V:OFF_L].reshape(batch, seq, LRU_WIDTH)
  conv_s = jnp.concatenate([state_conv[0].astype(F32), xl_s], axis=1)[:, -(CONV_W - 1):]
  return (
      y_all[:tp].reshape(batch, seq, D_MODEL),
      y_all[tp:].reshape(dec_batch, dec_seq, D_MODEL),
      proj[:tp, OFF_Q:OFF_K].reshape(kv_shape_p),
      proj[:tp, OFF_K:OFF_V].reshape(kv_shape_p),
      xl_p[:, -(CONV_W - 1):][None],
      h_p.reshape(1, batch, LRU_WIDTH),
      k_new.reshape(kv_shape_s),
      v_new.reshape(kv_shape_s),
      conv_s[None],
      h_s[None],
  )
```

```python
import functools
import math

import jax
import jax.numpy as jnp
from jax import lax
from jax.experimental import pallas as pl
from jax.experimental.pallas import tpu as pltpu

F32 = jnp.float32
BF16 = jnp.bfloat16

D_MODEL = 2048
N_HEADS = 8
N_KV_HEADS = 4
HEAD_DIM = 64
HEAD_W = 2 * HEAD_DIM
ROT_DIM = HEAD_DIM // 4
ROPE_THETA = 500000.0
ATT_WIDTH = N_HEADS * HEAD_W
KV_WIDTH = N_KV_HEADS * HEAD_W
LRU_WIDTH = 1024
LRU_BLOCKS = 8
LRU_BLOCK_W = LRU_WIDTH // LRU_BLOCKS
CONV_W = 4
LRU_C = 8.0
N_KEYS = 128
KEY_BITS = 7
N_EXPERTS = N_KEYS * N_KEYS
PEER_HEADS = 8
PEER_TOPK = 16
PEER_SEL = PEER_HEADS * PEER_TOPK
PAGE_SIZE = 128
RMS_EPS = 1e-6
OFF_Q = ATT_WIDTH
OFF_K = OFF_Q + KV_WIDTH
OFF_V = OFF_K + KV_WIDTH
OFF_L = OFF_V + LRU_WIDTH
OFF_GA = OFF_L + D_MODEL
IN_WIDTH = OFF_GA + D_MODEL
LAM_INIT = 0.8 - 0.6 * math.exp(0.0)

LANES = 128
SUBLANES = 8
VMEM_LIMIT_BYTES = 56 * 1024 * 1024

NEG_INF = float("-inf")
LOG2_E = math.log2(math.e)
NT_DIMS = (((1,), (1,)), ((), ()))


def _pick_block(total, cap, quantum):
  best = None
  b = quantum
  while b <= min(cap, total):
    if total % b == 0:
      best = b
    b += quantum
  assert best is not None, (total, cap, quantum)
  return best


def _params(sem, vmem=VMEM_LIMIT_BYTES):
  return pltpu.CompilerParams(dimension_semantics=sem, vmem_limit_bytes=vmem)


def _rms(x, g):
  var = jnp.mean(x * x, axis=-1, keepdims=True)
  return x * lax.rsqrt(var + RMS_EPS) * g


def _tail_rows(tp, ts, tm):
  assert (tp + ts) % tm == 0 and 0 < ts < tm, (tp, ts, tm)
  return tm - ts


def _prompt_rows_spec(tm, width, tp):
  last_prompt_block = pl.cdiv(tp, tm) - 1
  return pl.BlockSpec((tm, width), lambda i, *_: (jnp.minimum(i, last_prompt_block), 0))


def _sample_rows_spec(ts, width):
  return pl.BlockSpec((ts, width), lambda i, *_: (0, 0))


def _stacked_rows(i, n_blocks, rem, p_ref, s_ref, emit):
  tm = rem + s_ref.shape[0]

  @pl.when(i < n_blocks - 1)
  def _():
    emit(slice(0, tm), p_ref[...])

  @pl.when(i == n_blocks - 1)
  def _():
    emit(slice(0, rem), p_ref[0:rem, :])
    emit(slice(rem, tm), s_ref[...])


def _inproj_body(xp_ref, xs_ref, g_ref, w_ref, cos_ref, sa_ref, sb_ref, o_ref, xn_ref, *, tn, n_rope_blocks, rem):
  i = pl.program_id(0)
  j = pl.program_id(1)

  @pl.when(j == 0)
  def _():
    def emit(rows, x):
      xn_ref[rows, :] = _rms(x, g_ref[...]).astype(BF16)
    _stacked_rows(i, pl.num_programs(0), rem, xp_ref, xs_ref, emit)

  acc = jnp.dot(xn_ref[...], w_ref[...], preferred_element_type=F32)

  @pl.when(j < n_rope_blocks)
  def _():
    reps = tn // LANES
    cos = jnp.concatenate([cos_ref[...]] * reps, axis=1)
    sa = jnp.concatenate([sa_ref[...]] * reps, axis=1)
    sb = jnp.concatenate([sb_ref[...]] * reps, axis=1)
    half = ROT_DIM // 2
    rot = acc * cos + pltpu.roll(acc, tn - half, 1) * sa + pltpu.roll(acc, half, 1) * sb
    col = j * tn + lax.broadcasted_iota(jnp.int32, acc.shape, 1)
    o_ref[...] = jnp.where(col < OFF_Q, rot * (HEAD_DIM ** -0.5), jnp.where(col < OFF_K, rot, acc))

  @pl.when(j >= n_rope_blocks)
  def _():
    o_ref[...] = acc


def _inproj(x_p, x_s, norm_g, w_in_bf, cos_t, sa_t, sb_t):
  tp, ts = x_p.shape[0], x_s.shape[0]
  t = tp + ts
  tm = _pick_block(t, 640, LANES)
  tn = 1024
  body = functools.partial(_inproj_body, tn=tn, n_rope_blocks=pl.cdiv(OFF_K, tn), rem=_tail_rows(tp, ts, tm))
  return pl.pallas_call(
      body,
      grid=(t // tm, IN_WIDTH // tn),
      in_specs=[
          _prompt_rows_spec(tm, D_MODEL, tp),
          _sample_rows_spec(ts, D_MODEL),
          pl.BlockSpec((1, D_MODEL), lambda i, j: (0, 0)),
          pl.BlockSpec((D_MODEL, tn), lambda i, j: (0, j)),
          pl.BlockSpec((tm, LANES), lambda i, j: (i, 0)),
          pl.BlockSpec((tm, LANES), lambda i, j: (i, 0)),
          pl.BlockSpec((tm, LANES), lambda i, j: (i, 0)),
      ],
      out_specs=pl.BlockSpec((tm, tn), lambda i, j: (i, j)),
      out_shape=jax.ShapeDtypeStruct((t, IN_WIDTH), F32),
      scratch_shapes=[pltpu.VMEM((tm, D_MODEL), BF16)],
      compiler_params=_params(("parallel", "arbitrary")),
      name="inproj",
  )(x_p, x_s, norm_g, w_in_bf, cos_t, sa_t, sb_t)


def _subln(o, g):
  var = jnp.mean(o * o, axis=-1, keepdims=True)
  return o * lax.rsqrt(var + RMS_EPS) * g * (1.0 - LAM_INIT)


def _pattn_body(qi_ref, ki_ref, lam_ref, q_ref, k_ref, v_ref, g_ref, o_ref, q4_ref, m_ref, l_ref, acc_ref,
                *, tq, tk):
  qi = qi_ref[pl.program_id(2)]
  ki = ki_ref[pl.program_id(2)]
  cols = 2 * 2 * tq

  @pl.when(ki == 0)
  def _():
    m_ref[...] = jnp.full((1, cols), NEG_INF, F32)
    l_ref[...] = jnp.zeros((1, cols), F32)
    acc_ref[...] = jnp.zeros((HEAD_W, cols), F32)
    lane = lax.broadcasted_iota(jnp.int32, (tq, HEAD_W), 1)
    for r in range(2):
      qh = q_ref[:, r * HEAD_W:(r + 1) * HEAD_W] * LOG2_E
      q4_ref[(2 * r) * tq:(2 * r + 1) * tq, :] = jnp.where(lane < HEAD_DIM, qh, 0.0).astype(BF16)
      q4_ref[(2 * r + 1) * tq:(2 * r + 2) * tq, :] = jnp.where(lane >= HEAD_DIM, qh, 0.0).astype(BF16)

  def update(diagonal):
    k = k_ref[...].astype(BF16)
    s = lax.dot_general(k, q4_ref[...], NT_DIMS, preferred_element_type=F32)
    if diagonal:
      visible = (lax.broadcasted_iota(jnp.int32, (tk, tq), 0) <= lax.broadcasted_iota(jnp.int32, (tk, tq), 1))
      s = jnp.concatenate([jnp.where(visible, s[:, j * tq:(j + 1) * tq], NEG_INF) for j in range(4)], axis=1)
    m_old = m_ref[...]
    m_new = jnp.maximum(m_old, jnp.max(s, axis=0, keepdims=True))
    alpha = jnp.exp2(m_old - m_new)
    p = jnp.exp2(s - m_new)
    l_ref[...] = alpha * l_ref[...] + jnp.sum(p, axis=0, keepdims=True)
    vt = v_ref[...].T.astype(BF16)
    acc_ref[...] = alpha * acc_ref[...] + jnp.dot(vt, p.astype(BF16), preferred_element_type=F32)
    m_ref[...] = m_new

  @pl.when(ki < qi)
  def _():
    update(False)

  @pl.when(ki == qi)
  def _():
    update(True)
    lam = lam_ref[0]
    o = acc_ref[...] * (1.0 / l_ref[...])
    for r in range(2):
      d = o[:, (2 * r) * tq:(2 * r + 1) * tq] - lam * o[:, (2 * r + 1) * tq:(2 * r + 2) * tq]
      var = jnp.mean(d * d, axis=0, keepdims=True)
      dn = d * (lax.rsqrt(var + RMS_EPS) * (1.0 - LAM_INIT))
      o_ref[:, r * HEAD_W:(r + 1) * HEAD_W] = (dn.T * g_ref[...]).astype(o_ref.dtype)


def _prompt_attention(proj, lam, subln_g, batch, seq):
  tq = tk = _pick_block(seq, 512, LANES)
  nq = seq // tq
  gw = 2 * HEAD_W
  body = functools.partial(_pattn_body, tq=tq, tk=tk)
  pairs = [(qi, ki) for qi in range(nq) for ki in range(qi + 1)]
  qi_of = jnp.asarray([p[0] for p in pairs], jnp.int32)
  ki_of = jnp.asarray([p[1] for p in pairs], jnp.int32)
  grid_spec = pltpu.PrefetchScalarGridSpec(
      num_scalar_prefetch=2,
      grid=(batch, N_KV_HEADS, len(pairs)),
      in_specs=[
          pl.BlockSpec(memory_space=pltpu.SMEM),
          pl.BlockSpec((tq, gw), lambda b, g, p, qo, ko: (b * nq + qo[p], g)),
          pl.BlockSpec((tk, HEAD_W), lambda b, g, p, qo, ko: (b * nq + ko[p], OFF_Q // HEAD_W + g)),
          pl.BlockSpec((tk, HEAD_W), lambda b, g, p, qo, ko: (b * nq + ko[p], OFF_K // HEAD_W + g)),
          pl.BlockSpec((1, HEAD_W), lambda b, g, p, qo, ko: (0, 0)),
      ],
      out_specs=pl.BlockSpec((tq, gw), lambda b, g, p, qo, ko: (b * nq + qo[p], g)),
      scratch_shapes=[
          pltpu.VMEM((4 * tq, HEAD_W), BF16),
          pltpu.VMEM((1, 4 * tq), F32),
          pltpu.VMEM((1, 4 * tq), F32),
          pltpu.VMEM((HEAD_W, 4 * tq), F32),
      ],
  )
  return pl.pallas_call(
      body,
      grid_spec=grid_spec,
      out_shape=jax.ShapeDtypeStruct((batch * seq, ATT_WIDTH), BF16),
      compiler_params=_params(("parallel", "parallel", "arbitrary")),
      name="prompt_attention",
  )(qi_of, ki_of, lam, proj, proj, proj, subln_g)


PAGES_PER_CHUNK = 8
SATTN_SLOTS = 3
PAGE_ROWS = PAGE_SIZE * N_KV_HEADS


def _sattn_body(pt_ref, lam_ref, wq_ref, kn_ref, vn_ref, g_ref, ck_ref, cv_ref, o_ref,
                kbuf, vbuf, bias_ref, sem, *, n_chunks, dec_seq):
  b = pl.program_id(0)
  nb = pl.num_programs(0)
  rows = wq_ref.shape[1]
  gr = rows // N_KV_HEADS
  cols = PAGES_PER_CHUNK * PAGE_ROWS

  def copies(bb, c, slot):
    out = []
    for p in range(PAGES_PER_CHUNK):
      page = pt_ref[bb, c * PAGES_PER_CHUNK + p]
      out.append(pltpu.make_async_copy(ck_ref.at[page], kbuf.at[slot, p], sem.at[0, slot]))
      out.append(pltpu.make_async_copy(cv_ref.at[page], vbuf.at[slot, p], sem.at[1, slot]))
    return out

  def start(bb, c, slot):
    for n, cp in enumerate(copies(bb, c, slot)):
      cp.start(priority=n % 2)

  def same_head(shape):
    head_of_col = jnp.bitwise_and(lax.broadcasted_iota(jnp.int32, shape, 1), N_KV_HEADS - 1)
    head_of_row = lax.broadcasted_iota(jnp.int32, shape, 0) // gr
    return head_of_col == head_of_row

  ahead = SATTN_SLOTS - 1

  def slot_of(bb, c):
    return lax.rem(bb * n_chunks + c, SATTN_SLOTS)

  @pl.when(b == 0)
  def _():
    for c in range(ahead):
      start(0, c, c)
    bias_ref[...] = jnp.where(same_head((rows, cols)), 0.0, NEG_INF)

  wq = wq_ref[0]

  def softmax_step(carry, s, v):
    m_old, l_old, acc = carry
    m_new = jnp.maximum(m_old, jnp.max(s, axis=1, keepdims=True))
    alpha = jnp.exp(m_old - m_new)
    p = jnp.exp(s - m_new)
    l_new = alpha * l_old + jnp.sum(p, axis=1, keepdims=True)
    acc = alpha * acc + jnp.dot(p.astype(BF16), v, preferred_element_type=F32)
    return m_new, l_new, acc

  def chunk(c, carry):
    slot = slot_of(b, c)

    @pl.when(c + ahead < n_chunks)
    def _():
      start(b, c + ahead, slot_of(b, c + ahead))

    @pl.when(jnp.logical_and(c + ahead >= n_chunks, b + 1 < nb))
    def _():
      start(b + 1, c + ahead - n_chunks, slot_of(b + 1, c + ahead - n_chunks))

    for cp in copies(b, c, slot):
      cp.wait()
    kc = kbuf[slot].reshape(cols, HEAD_W).astype(BF16)
    vc = vbuf[slot].reshape(cols, HEAD_W).astype(BF16)
    s = lax.dot_general(wq, kc, NT_DIMS, preferred_element_type=F32) + bias_ref[...]
    return softmax_step(carry, s, vc)

  init = (jnp.full((rows, 1), NEG_INF, F32), jnp.zeros((rows, 1), F32), jnp.zeros((rows, HEAD_W), F32))
  carry = lax.fori_loop(0, n_chunks, chunk, init)

  new_rows = kn_ref.shape[1]
  s = lax.dot_general(wq, kn_ref[0], NT_DIMS, preferred_element_type=F32)
  t_of_row = lax.broadcasted_iota(jnp.int32, (rows, new_rows), 0) % dec_seq
  t_of_col = lax.broadcasted_iota(jnp.int32, (rows, new_rows), 1) // N_KV_HEADS
  visible = jnp.logical_and(same_head((rows, new_rows)), t_of_col <= t_of_row)
  _, l_fin, acc = softmax_step(carry, jnp.where(visible, s, NEG_INF), vn_ref[0])

  o = acc / l_fin
  lam = lam_ref[0]
  for g in range(N_KV_HEADS):
    blk = o[g * gr:(g + 1) * gr]
    d = blk[:gr // 2] - lam * blk[gr // 2:]
    o_ref[0, g * (gr // 2):(g + 1) * (gr // 2), :] = _subln(d, g_ref[...])


def _sample_attention(page_table, lam, wq, k_new, v_new, subln_g, cache_k, cache_v, dec_seq):
  dec_batch, n_pages = page_table.shape
  assert n_pages % PAGES_PER_CHUNK == 0
  n_chunks = n_pages // PAGES_PER_CHUNK
  assert n_chunks >= SATTN_SLOTS - 1
  rows = wq.shape[1]
  new_rows = k_new.shape[1]
  body = functools.partial(_sattn_body, n_chunks=n_chunks, dec_seq=dec_seq)
  grid_spec = pltpu.PrefetchScalarGridSpec(
      num_scalar_prefetch=1,
      grid=(dec_batch,),
      in_specs=[
          pl.BlockSpec(memory_space=pltpu.SMEM),
          pl.BlockSpec((1, rows, HEAD_W), lambda b, pt: (b, 0, 0)),
          pl.BlockSpec((1, new_rows, HEAD_W), lambda b, pt: (b, 0, 0)),
          pl.BlockSpec((1, new_rows, HEAD_W), lambda b, pt: (b, 0, 0)),
          pl.BlockSpec((1, HEAD_W), lambda b, pt: (0, 0)),
          pl.BlockSpec(memory_space=pl.ANY),
          pl.BlockSpec(memory_space=pl.ANY),
      ],
      out_specs=pl.BlockSpec((1, rows // 2, HEAD_W), lambda b, pt: (b, 0, 0)),
      scratch_shapes=[
          pltpu.VMEM((SATTN_SLOTS, PAGES_PER_CHUNK, PAGE_ROWS, HEAD_W), F32),
          pltpu.VMEM((SATTN_SLOTS, PAGES_PER_CHUNK, PAGE_ROWS, HEAD_W), F32),
          pltpu.VMEM((rows, PAGES_PER_CHUNK * PAGE_ROWS), F32),
          pltpu.SemaphoreType.DMA((2, SATTN_SLOTS)),
      ],
  )
  return pl.pallas_call(
      body,
      grid_spec=grid_spec,
      out_shape=jax.ShapeDtypeStruct((dec_batch, rows // 2, HEAD_W), F32),
      compiler_params=_params(("arbitrary",)),
      name="sample_attention",
  )(page_table, lam, wq, k_new, v_new, subln_g, cache_k, cache_v)


def _lru_gates(xc, wa_ref, ba, wx_ref, bx, lam):
  ra, ix = [], []
  for n in range(LRU_BLOCKS):
    xb = xc[:, n * LRU_BLOCK_W:(n + 1) * LRU_BLOCK_W]
    ra.append(jnp.dot(xb, wa_ref[n], preferred_element_type=F32, precision=lax.Precision.HIGHEST))
    ix.append(jnp.dot(xb, wx_ref[n], preferred_element_type=F32, precision=lax.Precision.HIGHEST))
  r = jax.nn.sigmoid(jnp.concatenate(ra, axis=1) + ba)
  i = jax.nn.sigmoid(jnp.concatenate(ix, axis=1) + bx)
  neg = -lam
  softplus = jnp.maximum(neg, 0.0) + jnp.log1p(jnp.exp(-jnp.abs(neg)))
  log_a = -LRU_C * r * softplus
  a = jnp.exp(log_a)
  u = jnp.sqrt(1.0 - jnp.exp(2.0 * log_a)) * (i * xc)
  return a, u


def _conv(rows_of, cw_ref, cb):
  out = rows_of(0) * cw_ref[0:1, :]
  for j in range(1, CONV_W):
    out = out + rows_of(j) * cw_ref[j:j + 1, :]
  return out + cb


def _lru_prompt_body(xl_ref, cw_ref, cb_ref, wa_ref, ba_ref, wx_ref, bx_ref, lam_ref, y_ref, hl_ref,
                     ext_ref, a_ref, u_ref, hs_ref, h_ref, *, tt):
  ti = pl.program_id(1)
  head = SUBLANES

  @pl.when(ti == 0)
  def _():
    ext_ref[0:head, :] = jnp.zeros((head, LRU_WIDTH), F32)
    h_ref[...] = jnp.zeros((1, LRU_WIDTH), F32)

  ext_ref[head:head + tt, :] = xl_ref[...]
  xc = _conv(lambda j: ext_ref[head - (CONV_W - 1) + j:head - (CONV_W - 1) + j + tt, :], cw_ref, cb_ref[...])
  a, u = _lru_gates(xc, wa_ref, ba_ref[...], wx_ref, bx_ref[...], lam_ref[...])
  a_ref[...] = a
  u_ref[...] = u

  def step(t, h):
    h = a_ref[pl.ds(t, 1), :] * h + u_ref[pl.ds(t, 1), :]
    hs_ref[pl.ds(t, 1), :] = h
    return h

  h_fin = lax.fori_loop(0, tt, step, h_ref[...], unroll=8)
  h_ref[...] = h_fin
  y_ref[...] = hs_ref[...].astype(y_ref.dtype)
  ext_ref[0:head, :] = ext_ref[tt:tt + head, :]

  @pl.when(ti == pl.num_programs(1) - 1)
  def _():
    hl_ref[0] = h_fin


def _lru_weight_specs(nidx):
  zero2 = (lambda *a: (0, 0))
  zero3 = (lambda *a: (0, 0, 0))
  del nidx
  return [
      pl.BlockSpec((CONV_W, LRU_WIDTH), zero2),
      pl.BlockSpec((1, LRU_WIDTH), zero2),
      pl.BlockSpec((LRU_BLOCKS, LRU_BLOCK_W, LRU_BLOCK_W), zero3),
      pl.BlockSpec((1, LRU_WIDTH), zero2),
      pl.BlockSpec((LRU_BLOCKS, LRU_BLOCK_W, LRU_BLOCK_W), zero3),
      pl.BlockSpec((1, LRU_WIDTH), zero2),
      pl.BlockSpec((1, LRU_WIDTH), zero2),
  ]


def _lru_prompt(proj, lru_w, batch, seq):
  tt = _pick_block(seq, 512, LANES)
  nt = seq // tt
  body = functools.partial(_lru_prompt_body, tt=tt)
  return pl.pallas_call(
      body,
      grid=(batch, nt),
      in_specs=[pl.BlockSpec((tt, LRU_WIDTH), lambda b, ti: (b * nt + ti, OFF_V // LRU_WIDTH))]
      + _lru_weight_specs(2),
      out_specs=[
          pl.BlockSpec((tt, LRU_WIDTH), lambda b, ti: (b * nt + ti, 0)),
          pl.BlockSpec((1, 1, LRU_WIDTH), lambda b, ti: (b, 0, 0)),
      ],
      out_shape=[
          jax.ShapeDtypeStruct((batch * seq, LRU_WIDTH), BF16),
          jax.ShapeDtypeStruct((batch, 1, LRU_WIDTH), F32),
      ],
      scratch_shapes=[
          pltpu.VMEM((tt + 2 * SUBLANES, LRU_WIDTH), F32),
          pltpu.VMEM((tt, LRU_WIDTH), F32),
          pltpu.VMEM((tt, LRU_WIDTH), F32),
          pltpu.VMEM((tt, LRU_WIDTH), F32),
          pltpu.VMEM((1, LRU_WIDTH), F32),
      ],
      compiler_params=_params(("parallel", "arbitrary")),
      name="lru_prompt",
  )(proj, *lru_w)


def _lru_sample_body(xl_ref, cbuf_ref, h0_ref, cw_ref, cb_ref, wa_ref, ba_ref, wx_ref, bx_ref, lam_ref,
                     y_ref, hl_ref, *, dec_seq):
  ext = [cbuf_ref[j] for j in range(CONV_W - 1)] + [xl_ref[t] for t in range(dec_seq)]
  h = h0_ref[...]
  for t in range(dec_seq):
    xc = _conv(lambda j: ext[t + j], cw_ref, cb_ref[...])
    a, u = _lru_gates(xc, wa_ref, ba_ref[...], wx_ref, bx_ref[...], lam_ref[...])
    h = a * h + u
    y_ref[t] = h.astype(y_ref.dtype)
  hl_ref[...] = h


def _lru_sample(xl_t, cbuf_t, h0, lru_w):
  dec_seq, dec_batch, _ = xl_t.shape
  body = functools.partial(_lru_sample_body, dec_seq=dec_seq)
  return pl.pallas_call(
      body,
      out_shape=[
          jax.ShapeDtypeStruct((dec_seq, dec_batch, LRU_WIDTH), BF16),
          jax.ShapeDtypeStruct((dec_batch, LRU_WIDTH), F32),
      ],
      name="lru_sample",
  )(xl_t, cbuf_t, h0, *lru_w)


def _merge_body(attp_ref, atts_ref, lrup_ref, lrus_ref, ga0_ref, ga1_ref, gl0_ref, gl1_ref, xp_ref, xs_ref,
                wa_ref, wl_ref, wo_ref, g2_ref, h_ref, xn_ref, att_buf, lru_buf, x_buf, *, rem):
  i = pl.program_id(0)
  n = pl.num_programs(0)

  def fill(buf):
    def emit(rows, v):
      buf[rows, :] = v
    return emit

  _stacked_rows(i, n, rem, attp_ref, atts_ref, fill(att_buf))
  _stacked_rows(i, n, rem, lrup_ref, lrus_ref, fill(lru_buf))
  _stacked_rows(i, n, rem, xp_ref, xs_ref, fill(x_buf))
  a1 = jnp.dot(att_buf[...], wa_ref[...], preferred_element_type=F32)
  a2 = jnp.dot(lru_buf[...], wl_ref[...], preferred_element_type=F32)
  ga = jnp.concatenate([ga0_ref[...], ga1_ref[...]], axis=1)
  gl = jnp.concatenate([gl0_ref[...], gl1_ref[...]], axis=1)
  m = jax.nn.sigmoid(ga) * a1 + jax.nn.sigmoid(gl) * a2
  h = x_buf[...] + jnp.dot(m.astype(BF16), wo_ref[...], preferred_element_type=F32)
  h_ref[...] = h
  xn_ref[...] = _rms(h, g2_ref[...]).astype(BF16)


def _merge(att_p, att_s, lru_p, lru_s, proj, x_p, x_s, wa, wl, wo, g2):
  tp, ts = x_p.shape[0], x_s.shape[0]
  t = tp + ts
  tm = _pick_block(t, 320, 64)
  half = D_MODEL // 2
  const = lambda shape: pl.BlockSpec(shape, lambda i: (0, 0), pipeline_mode=pl.Buffered(1))
  gate = lambda blk: pl.BlockSpec((tm, half), lambda i: (i, blk))
  return pl.pallas_call(
      functools.partial(_merge_body, rem=_tail_rows(tp, ts, tm)),
      grid=(t // tm,),
      in_specs=[
          _prompt_rows_spec(tm, ATT_WIDTH, tp), _sample_rows_spec(ts, ATT_WIDTH),
          _prompt_rows_spec(tm, LRU_WIDTH, tp), _sample_rows_spec(ts, LRU_WIDTH),
          gate(OFF_L // half), gate(OFF_L // half + 1), gate(OFF_GA // half), gate(OFF_GA // half + 1),
          _prompt_rows_spec(tm, D_MODEL, tp), _sample_rows_spec(ts, D_MODEL),
          const((ATT_WIDTH, D_MODEL)), const((LRU_WIDTH, D_MODEL)), const((D_MODEL, D_MODEL)),
          const((1, D_MODEL)),
      ],
      scratch_shapes=[
          pltpu.VMEM((tm, ATT_WIDTH), BF16),
          pltpu.VMEM((tm, LRU_WIDTH), BF16),
          pltpu.VMEM((tm, D_MODEL), F32),
      ],
      out_specs=[
          pl.BlockSpec((tm, D_MODEL), lambda i: (i, 0)),
          pl.BlockSpec((tm, D_MODEL), lambda i: (i, 0)),
      ],
      out_shape=[
          jax.ShapeDtypeStruct((t, D_MODEL), F32),
          jax.ShapeDtypeStruct((t, D_MODEL), BF16),
      ],
      compiler_params=_params(("parallel",)),
      name="merge",
  )(att_p, att_s, lru_p, lru_s, proj, proj, proj, proj, x_p, x_s, wa, wl, wo, g2)


def _mm_body(x_ref, w_ref, o_ref):
  o_ref[...] = jnp.dot(x_ref[...], w_ref[...], preferred_element_type=F32)


def _matmul(x, w):
  t, kdim = x.shape
  n = w.shape[1]
  tm = _pick_block(t, 640, LANES)
  tn = 512
  return pl.pallas_call(
      _mm_body,
      grid=(t // tm, n // tn),
      in_specs=[pl.BlockSpec((tm, kdim), lambda i, j: (i, 0)), pl.BlockSpec((kdim, tn), lambda i, j: (0, j))],
      out_specs=pl.BlockSpec((tm, tn), lambda i, j: (i, j)),
      out_shape=jax.ShapeDtypeStruct((t, n), F32),
      compiler_params=_params(("parallel", "arbitrary")),
      name="peer_query",
  )(x, w)


def _top16_rows(s, val_ref, idx_ref, lane0):
  n, w = s.shape
  sub = lax.broadcasted_iota(jnp.int32, s.shape, 0).astype(F32)
  for k in range(PEER_TOPK):
    m = jnp.max(s, axis=0, keepdims=True)
    idx = jnp.min(jnp.where(s == m, sub, float(n)), axis=0, keepdims=True)
    val_ref[k:k + 1, lane0:lane0 + w] = m
    idx_ref[k:k + 1, lane0:lane0 + w] = idx
    s = jnp.where(sub == idx, NEG_INF, s)


PEER_HEADS_PER_ITER = 2


def _topk_body(q_ref, k1_ref, k2_ref, eid_ref, gate_ref, val_ref, idx_ref, best_ref, sel_ref, eid_s, gate_s):
  tb = q_ref.shape[0]
  kk = PEER_TOPK
  half_w = N_KEYS
  hp = PEER_HEADS_PER_ITER
  wide = hp * tb

  def head_group(hg, carry):
    for hh in range(hp):
      scores = []
      for c, key_ref in enumerate((k1_ref, k2_ref)):
        col = pl.multiple_of((2 * (hg * hp + hh) + c) * half_w, half_w)
        qh = q_ref[:, pl.ds(col, half_w)]
        scores.append(lax.dot_general(key_ref[...], qh, NT_DIMS, preferred_element_type=F32,
                                      precision=lax.Precision.HIGHEST))
      _top16_rows(jnp.concatenate(scores, axis=1), val_ref, idx_ref, hh * 2 * tb)

    def pick(ref, c):
      return jnp.concatenate([ref[:, (2 * hh + c) * tb:(2 * hh + c + 1) * tb] for hh in range(hp)], axis=1)

    v1, v2 = pick(val_ref, 0), pick(val_ref, 1)
    i1, i2 = pick(idx_ref, 0), pick(idx_ref, 1)
    b16 = lax.broadcasted_iota(jnp.int32, (kk, wide), 0).astype(F32)
    b8 = lax.broadcasted_iota(jnp.int32, (SUBLANES, wide), 0).astype(F32)
    vals = [v1[0:1] + v2]
    flat = [b16]
    code = [i1[0:1] * N_KEYS + i2]
    for a in range(1, SUBLANES):
      vals.append(v1[a:a + 1] + v2[0:SUBLANES])
      flat.append(a * kk + b8)
      code.append(i1[a:a + 1] * N_KEYS + i2[0:SUBLANES])
    vals.append(v1[SUBLANES:kk] + v2[0:1])
    flat.append((b8 + SUBLANES) * kk)
    code.append(i1[SUBLANES:kk] * N_KEYS + i2[0:1])
    cand = jnp.concatenate(vals, axis=0)
    flat = jnp.concatenate(flat, axis=0)
    code = jnp.concatenate(code, axis=0)
    for k in range(kk):
      m = jnp.max(cand, axis=0, keepdims=True)
      fsel = jnp.min(jnp.where(cand == m, flat, float(kk * kk)), axis=0, keepdims=True)
      hit = flat == fsel
      best_ref[k:k + 1, :] = m
      sel_ref[k:k + 1, :] = jnp.max(jnp.where(hit, code, -1.0), axis=0, keepdims=True)
      cand = jnp.where(hit, NEG_INF, cand)
    best = best_ref[...]
    e = jnp.exp(best - best[0:1])
    gate = e / jnp.sum(e, axis=0, keepdims=True)
    eid = sel_ref[...].astype(jnp.int32)
    for hh in range(hp):
      row = pl.multiple_of((hg * hp + hh) * kk, kk)
      gate_s[pl.ds(row, kk), :] = gate[:, hh * tb:(hh + 1) * tb]
      eid_s[pl.ds(row, kk), :] = eid[:, hh * tb:(hh + 1) * tb]
    return carry

  lax.fori_loop(0, PEER_HEADS // hp, head_group, 0)
  gate_ref[...] = gate_s[...].T
  eid_ref[...] = eid_s[...].T


def _peer_topk(qp, k1, k2):
  t = qp.shape[0]
  tb = LANES
  return pl.pallas_call(
      _topk_body,
      grid=(t // tb,),
      in_specs=[
          pl.BlockSpec((tb, qp.shape[1]), lambda i: (i, 0)),
          pl.BlockSpec((N_KEYS, N_KEYS), lambda i: (0, 0)),
          pl.BlockSpec((N_KEYS, N_KEYS), lambda i: (0, 0)),
      ],
      out_specs=[
          pl.BlockSpec((tb, PEER_SEL), lambda i: (i, 0)),
          pl.BlockSpec((tb, PEER_SEL), lambda i: (i, 0)),
      ],
      out_shape=[
          jax.ShapeDtypeStruct((t, PEER_SEL), jnp.int32),
          jax.ShapeDtypeStruct((t, PEER_SEL), F32),
      ],
      scratch_shapes=[
          pltpu.VMEM((PEER_TOPK, 2 * PEER_HEADS_PER_ITER * tb), F32),
          pltpu.VMEM((PEER_TOPK, 2 * PEER_HEADS_PER_ITER * tb), F32),
          pltpu.VMEM((PEER_TOPK, PEER_HEADS_PER_ITER * tb), F32),
          pltpu.VMEM((PEER_TOPK, PEER_HEADS_PER_ITER * tb), F32),
          pltpu.VMEM((PEER_SEL, tb), jnp.int32),
          pltpu.VMEM((PEER_SEL, tb), F32),
      ],
      compiler_params=_params(("parallel",)),
      name="peer_topk",
  )(qp, k1, k2)


def _gate_matrix_body(eid_ref, gate_ref, g_ref):
  tb = eid_ref.shape[0]
  sub = lax.broadcasted_iota(jnp.int32, (N_KEYS, PEER_SEL), 0)

  def token_group(gi, carry):
    tiles = []
    for tau in range(SUBLANES):
      t = gi * SUBLANES + tau
      e = eid_ref[pl.ds(t, 1), :]
      gt = gate_ref[pl.ds(t, 1), :]
      at = jnp.where(sub == lax.shift_right_logical(e, KEY_BITS), gt, 0.0).astype(BF16)
      bt = jnp.where(sub == jnp.bitwise_and(e, N_KEYS - 1), 1.0, 0.0).astype(BF16)
      tiles.append(lax.dot_general(at, bt, NT_DIMS, preferred_element_type=F32))
    blocks = jnp.stack([jnp.stack([tile[h * SUBLANES:(h + 1) * SUBLANES, :] for tile in tiles])
                        for h in range(N_KEYS // SUBLANES)])
    g_ref[gi] = jnp.swapaxes(blocks, 1, 2).reshape(N_KEYS, SUBLANES, N_KEYS)
    return carry

  lax.fori_loop(0, tb // SUBLANES, token_group, 0)


def _gate_matrix(eid_t, gate_t):
  t = eid_t.shape[0]
  tb = _pick_block(t, 64, SUBLANES)
  return pl.pallas_call(
      _gate_matrix_body,
      grid=(t // tb,),
      in_specs=[pl.BlockSpec((tb, PEER_SEL), lambda i: (i, 0)), pl.BlockSpec((tb, PEER_SEL), lambda i: (i, 0))],
      out_specs=pl.BlockSpec((tb // SUBLANES, N_KEYS, SUBLANES, N_KEYS), lambda i: (i, 0, 0, 0)),
      out_shape=jax.ShapeDtypeStruct((t // SUBLANES, N_KEYS, SUBLANES, N_KEYS), F32),
      compiler_params=_params(("parallel",)),
      name="peer_gate_matrix",
  )(eid_t, gate_t)


PEER_I1_PER_STEP = 8


def _peer_dense_body(x_ref, u_ref, v_ref, g_ref, h_ref, fg_ref, y_ref, ys_ref, *, rem):
  c = pl.program_id(1)
  s = lax.dot_general(x_ref[...], u_ref[...], NT_DIMS, preferred_element_type=F32)
  act = 0.5 * s * (1.0 + lax.erf(s * (2.0 ** -0.5)))
  tm = x_ref.shape[0]
  coef = jnp.concatenate(
      [g_ref[:, j].reshape(tm, N_KEYS) * act[:, j * N_KEYS:(j + 1) * N_KEYS] for j in range(PEER_I1_PER_STEP)],
      axis=1)
  contrib = jnp.dot(coef.astype(BF16), v_ref[...], preferred_element_type=F32)

  @pl.when(c == 0)
  def _():
    y_ref[...] = contrib

  @pl.when(c > 0)
  def _():
    y_ref[...] += contrib

  @pl.when(c == pl.num_programs(1) - 1)
  def _():
    y = _rms(h_ref[...] + y_ref[...], fg_ref[...])
    y_ref[...] = y

    @pl.when(pl.program_id(0) == pl.num_programs(0) - 1)
    def _():
      ys_ref[...] = y[rem:, :]


def _peer_dense(xn2, u_bf, v_bf, gmat, h, final_g, tp):
  t = xn2.shape[0]
  ts = t - tp
  tm = _pick_block(t, 640, LANES)
  te = PEER_I1_PER_STEP * N_KEYS
  return pl.pallas_call(
      functools.partial(_peer_dense_body, rem=_tail_rows(tp, ts, tm)),
      grid=(t // tm, N_EXPERTS // te),
      in_specs=[
          pl.BlockSpec((tm, D_MODEL), lambda i, c: (i, 0)),
          pl.BlockSpec((te, D_MODEL), lambda i, c: (c, 0)),
          pl.BlockSpec((te, D_MODEL), lambda i, c: (c, 0)),
          pl.BlockSpec((tm // SUBLANES, PEER_I1_PER_STEP, SUBLANES, N_KEYS), lambda i, c: (i, c, 0, 0)),
          pl.BlockSpec((tm, D_MODEL), lambda i, c: (i, 0), pipeline_mode=pl.Buffered(1)),
          pl.BlockSpec((1, D_MODEL), lambda i, c: (0, 0)),
      ],
      out_specs=[_prompt_rows_spec(tm, D_MODEL, tp), _sample_rows_spec(ts, D_MODEL)],
      out_shape=[jax.ShapeDtypeStruct((tp, D_MODEL), F32), jax.ShapeDtypeStruct((ts, D_MODEL), F32)],
      compiler_params=_params(("arbitrary", "arbitrary")),
      name="peer_dense",
  )(xn2, u_bf, v_bf, gmat, h, final_g)


def _rope_tables(pos):
  half = ROT_DIM // 2
  inv_freq = jnp.float32(ROPE_THETA) ** (-jnp.arange(half, dtype=F32) * 2.0 / ROT_DIM)
  ang = pos.astype(F32)[:, None] * inv_freq[None, :]
  cos, sin = jnp.cos(ang), jnp.sin(ang)
  n = pos.shape[0]
  ones = jnp.ones((n, HEAD_DIM - ROT_DIM), F32)
  zeros = jnp.zeros((n, HEAD_DIM - ROT_DIM), F32)
  zh = jnp.zeros((n, half), F32)
  cos_c = jnp.concatenate([cos, cos, ones], axis=1)
  sa_c = jnp.concatenate([-sin, zh, zeros], axis=1)
  sb_c = jnp.concatenate([zh, sin, zeros], axis=1)
  two = lambda a: jnp.concatenate([a, a], axis=1)
  return two(cos_c), two(sa_c), two(sb_c)


def kernel(x_prompt, x_sample, cache_k, cache_v, state_conv, state_h, page_table, norm1_g, w_in, lambda_q1, lambda_k1, lambda_q2, lambda_k2, subln_g, conv_w, conv_b, lru_wa, lru_ba, lru_wx, lru_bx, lru_lambda, w_att_up, w_lru_up, w_out, norm2_g, peer_wq, peer_k1, peer_k2, peer_u, peer_v, final_g):
  batch, seq, _ = x_prompt.shape
  dec_batch, dec_seq, _ = x_sample.shape
  n_pages = page_table.shape[1]
  past_len = n_pages * PAGE_SIZE
  tp = batch * seq
  ts = dec_batch * dec_seq
  assert w_in.shape[0] == 1, "one layer"

  x_p = x_prompt.reshape(tp, D_MODEL)
  x_s = x_sample.reshape(ts, D_MODEL)
  tabs_p = _rope_tables(jnp.arange(seq, dtype=jnp.int32))
  tabs_s = _rope_tables(past_len + jnp.arange(dec_seq, dtype=jnp.int32))
  cos_t, sa_t, sb_t = [jnp.concatenate([jnp.tile(a, (batch, 1)), jnp.tile(b, (dec_batch, 1))], axis=0)
                       for a, b in zip(tabs_p, tabs_s)]
  lam = (jnp.exp(jnp.sum(lambda_q1[0].astype(F32) * lambda_k1[0].astype(F32)))
         - jnp.exp(jnp.sum(lambda_q2[0].astype(F32) * lambda_k2[0].astype(F32))) + LAM_INIT).reshape(1)
  row = lambda a: a.reshape(1, -1)

  proj = _inproj(x_p, x_s, row(norm1_g[0]), w_in[0].astype(BF16), cos_t, sa_t, sb_t)

  att_p = _prompt_attention(proj, lam, row(subln_g[0]), batch, seq)
  proj_s = proj[tp:]
  q_s = proj_s[:, :OFF_Q].reshape(dec_batch, dec_seq, N_KV_HEADS, 2, 2, HEAD_DIM)
  q_s = q_s.transpose(0, 2, 4, 3, 1, 5).reshape(dec_batch, N_KV_HEADS, 2, 2 * dec_seq, HEAD_DIM)
  wq = jnp.einsum("bgcnd,ce->bgcned", q_s, jnp.eye(2, dtype=F32))
  wq = wq.reshape(dec_batch, N_KV_HEADS * 2 * 2 * dec_seq, HEAD_W).astype(BF16)
  new_rows = lambda a: a.reshape(dec_batch, dec_seq * N_KV_HEADS, HEAD_W).astype(BF16)
  k_new, v_new = proj_s[:, OFF_Q:OFF_K], proj_s[:, OFF_K:OFF_V]
  n_pool = cache_k.shape[1]
  att_s = _sample_attention(page_table, lam, wq, new_rows(k_new), new_rows(v_new), row(subln_g[0]),
                            cache_k.reshape(n_pool, PAGE_ROWS, HEAD_W),
                            cache_v.reshape(n_pool, PAGE_ROWS, HEAD_W), dec_seq)
  att_s = att_s.reshape(dec_batch, N_KV_HEADS, 2, dec_seq, HEAD_W).transpose(0, 3, 1, 2, 4)
  att_s = att_s.reshape(ts, ATT_WIDTH).astype(BF16)

  lru_w = (conv_w[0], row(conv_b[0]), lru_wa[0], row(lru_ba[0]), lru_wx[0], row(lru_bx[0]), row(lru_lambda[0]))
  lru_p, h_p = _lru_prompt(proj, lru_w, batch, seq)
  xl_s = proj_s[:, OFF_V:OFF_L].reshape(dec_batch, dec_seq, LRU_WIDTH)
  lru_s, h_s = _lru_sample(xl_s.transpose(1, 0, 2), state_conv[0].transpose(1, 0, 2), state_h[0], lru_w)
  lru_s = lru_s.transpose(1, 0, 2).reshape(ts, LRU_WIDTH)

  h_all, xn2 = _merge(att_p, att_s, lru_p, lru_s, proj, x_p, x_s, w_att_up[0].astype(BF16),
                      w_lru_up[0].astype(BF16), w_out[0].astype(BF16), row(norm2_g[0]))
  qp = _matmul(xn2, peer_wq[0].astype(BF16))
  eid_t, gate_t = _peer_topk(qp, peer_k1[0], peer_k2[0])
  gmat = _gate_matrix(eid_t, gate_t)
  y_p, y_s = _peer_dense(xn2, peer_u[0].astype(BF16), peer_v[0].astype(BF16), gmat, h_all, row(final_g), tp)

  kv_shape_p = (1, batch, seq, N_KV_HEADS, HEAD_W)
  kv_shape_s = (1, dec_batch, dec_seq, N_KV_HEADS, HEAD_W)
  tail = CONV_W - 1
  conv_p = jnp.stack([proj[(b + 1) * seq - tail:(b + 1) * seq, OFF_V:OFF_L] for b in range(batch)])
  conv_s = jnp.concatenate([state_conv[0].astype(F32), xl_s], axis=1)[:, -tail:]
  return (
      y_p.reshape(batch, seq, D_MODEL),
      y_s.reshape(dec_batch, dec_seq, D_MODEL),
      proj[:tp, OFF_Q:OFF_K].reshape(kv_shape_p),
      proj[:tp, OFF_K:OFF_V].reshape(kv_shape_p),
      conv_p[None],
      h_p.reshape(1, batch, LRU_WIDTH),
      k_new.reshape(kv_shape_s),
      v_new.reshape(kv_shape_s),
      conv_s[None],
      h_s[None],
  )
```

```python
import functools
import math

import jax
import jax.numpy as jnp
from jax import lax
from jax.experimental import pallas as pl
from jax.experimental.pallas import tpu as pltpu

F32 = jnp.float32
BF16 = jnp.bfloat16

D_MODEL = 2048
N_HEADS = 8
N_KV_HEADS = 4
HEAD_DIM = 64
HEAD_W = 2 * HEAD_DIM
ROT_DIM = HEAD_DIM // 4
ROPE_THETA = 500000.0
ATT_WIDTH = N_HEADS * HEAD_W
KV_WIDTH = N_KV_HEADS * HEAD_W
LRU_WIDTH = 1024
LRU_BLOCKS = 8
LRU_BLOCK_W = LRU_WIDTH // LRU_BLOCKS
CONV_W = 4
LRU_C = 8.0
N_KEYS = 128
KEY_BITS = 7
N_EXPERTS = N_KEYS * N_KEYS
PEER_HEADS = 8
PEER_TOPK = 16
PEER_SEL = PEER_HEADS * PEER_TOPK
PAGE_SIZE = 128
RMS_EPS = 1e-6
OFF_Q = ATT_WIDTH
OFF_K = OFF_Q + KV_WIDTH
OFF_V = OFF_K + KV_WIDTH
OFF_L = OFF_V + LRU_WIDTH
OFF_GA = OFF_L + D_MODEL
IN_WIDTH = OFF_GA + D_MODEL
LAM_INIT = 0.8 - 0.6 * math.exp(0.0)

LANES = 128
SUBLANES = 8
VMEM_LIMIT_BYTES = 56 * 1024 * 1024

NEG_INF = float("-inf")
LOG2_E = math.log2(math.e)
NT_DIMS = (((1,), (1,)), ((), ()))


def _pick_block(total, cap, quantum):
  best = None
  b = quantum
  while b <= min(cap, total):
    if total % b == 0:
      best = b
    b += quantum
  assert best is not None, (total, cap, quantum)
  return best


def _params(sem, vmem=VMEM_LIMIT_BYTES):
  return pltpu.CompilerParams(dimension_semantics=sem, vmem_limit_bytes=vmem)


def _rms(x, g):
  var = jnp.mean(x * x, axis=-1, keepdims=True)
  return x * lax.rsqrt(var + RMS_EPS) * g


def _tail_rows(tp, ts, tm):
  assert (tp + ts) % tm == 0 and 0 < ts < tm, (tp, ts, tm)
  return tm - ts


def _prompt_rows_spec(tm, width, tp):
  last_prompt_block = pl.cdiv(tp, tm) - 1
  return pl.BlockSpec((tm, width), lambda i, *_: (jnp.minimum(i, last_prompt_block), 0))


def _sample_rows_spec(ts, width):
  return pl.BlockSpec((ts, width), lambda i, *_: (0, 0))


def _stacked_rows(i, n_blocks, rem, p_ref, s_ref, emit):
  tm = rem + s_ref.shape[0]

  @pl.when(i < n_blocks - 1)
  def _():
    emit(slice(0, tm), p_ref[...])

  @pl.when(i == n_blocks - 1)
  def _():
    emit(slice(0, rem), p_ref[0:rem, :])
    emit(slice(rem, tm), s_ref[...])


def _inproj_body(xp_ref, xs_ref, g_ref, w_ref, cos_ref, sa_ref, sb_ref, o_ref, xn_ref, *, tn, n_rope_blocks, rem):
  i = pl.program_id(0)
  j = pl.program_id(1)

  @pl.when(j == 0)
  def _():
    def emit(rows, x):
      xn_ref[rows, :] = _rms(x, g_ref[...]).astype(BF16)
    _stacked_rows(i, pl.num_programs(0), rem, xp_ref, xs_ref, emit)

  acc = jnp.dot(xn_ref[...], w_ref[...], preferred_element_type=F32)

  @pl.when(j < n_rope_blocks)
  def _():
    reps = tn // LANES
    cos = jnp.concatenate([cos_ref[...]] * reps, axis=1)
    sa = jnp.concatenate([sa_ref[...]] * reps, axis=1)
    sb = jnp.concatenate([sb_ref[...]] * reps, axis=1)
    half = ROT_DIM // 2
    rot = acc * cos + pltpu.roll(acc, tn - half, 1) * sa + pltpu.roll(acc, half, 1) * sb
    col = j * tn + lax.broadcasted_iota(jnp.int32, acc.shape, 1)
    o_ref[...] = jnp.where(col < OFF_Q, rot * (HEAD_DIM ** -0.5), jnp.where(col < OFF_K, rot, acc))

  @pl.when(j >= n_rope_blocks)
  def _():
    o_ref[...] = acc


def _inproj(x_p, x_s, norm_g, w_in_bf, cos_t, sa_t, sb_t):
  tp, ts = x_p.shape[0], x_s.shape[0]
  t = tp + ts
  tm = _pick_block(t, 832, 64)
  tn = 1024
  body = functools.partial(_inproj_body, tn=tn, n_rope_blocks=pl.cdiv(OFF_K, tn), rem=_tail_rows(tp, ts, tm))
  return pl.pallas_call(
      body,
      grid=(t // tm, IN_WIDTH // tn),
      in_specs=[
          _prompt_rows_spec(tm, D_MODEL, tp),
          _sample_rows_spec(ts, D_MODEL),
          pl.BlockSpec((1, D_MODEL), lambda i, j: (0, 0)),
          pl.BlockSpec((D_MODEL, tn), lambda i, j: (0, j)),
          pl.BlockSpec((tm, LANES), lambda i, j: (i, 0)),
          pl.BlockSpec((tm, LANES), lambda i, j: (i, 0)),
          pl.BlockSpec((tm, LANES), lambda i, j: (i, 0)),
      ],
      out_specs=pl.BlockSpec((tm, tn), lambda i, j: (i, j)),
      out_shape=jax.ShapeDtypeStruct((t, IN_WIDTH), F32),
      scratch_shapes=[pltpu.VMEM((tm, D_MODEL), BF16)],
      compiler_params=_params(("parallel", "arbitrary")),
      name="inproj",
  )(x_p, x_s, norm_g, w_in_bf, cos_t, sa_t, sb_t)


def _subln(o, g):
  var = jnp.mean(o * o, axis=-1, keepdims=True)
  return o * lax.rsqrt(var + RMS_EPS) * g * (1.0 - LAM_INIT)


def _pattn_body(qi_ref, ki_ref, lam_ref, q_ref, k_ref, v_ref, g_ref, o_ref, q4_ref, m_ref, l_ref, acc_ref,
                *, tq, tk):
  qi = qi_ref[pl.program_id(2)]
  ki = ki_ref[pl.program_id(2)]
  cols = 2 * 2 * tq

  @pl.when(ki == 0)
  def _():
    m_ref[...] = jnp.full((1, cols), NEG_INF, F32)
    l_ref[...] = jnp.zeros((1, cols), F32)
    acc_ref[...] = jnp.zeros((HEAD_W, cols), F32)
    lane = lax.broadcasted_iota(jnp.int32, (tq, HEAD_W), 1)
    for r in range(2):
      qh = q_ref[:, r * HEAD_W:(r + 1) * HEAD_W] * LOG2_E
      q4_ref[(2 * r) * tq:(2 * r + 1) * tq, :] = jnp.where(lane < HEAD_DIM, qh, 0.0).astype(BF16)
      q4_ref[(2 * r + 1) * tq:(2 * r + 2) * tq, :] = jnp.where(lane >= HEAD_DIM, qh, 0.0).astype(BF16)

  def update(diagonal):
    k = k_ref[...].astype(BF16)
    s = lax.dot_general(k, q4_ref[...], NT_DIMS, preferred_element_type=F32)
    if diagonal:
      visible = (lax.broadcasted_iota(jnp.int32, (tk, tq), 0) <= lax.broadcasted_iota(jnp.int32, (tk, tq), 1))
      s = jnp.concatenate([jnp.where(visible, s[:, j * tq:(j + 1) * tq], NEG_INF) for j in range(4)], axis=1)
    m_old = m_ref[...]
    m_new = jnp.maximum(m_old, jnp.max(s, axis=0, keepdims=True))
    alpha = jnp.exp2(m_old - m_new)
    p = jnp.exp2(s - m_new)
    l_ref[...] = alpha * l_ref[...] + jnp.sum(p, axis=0, keepdims=True)
    vt = v_ref[...].T.astype(BF16)
    acc_ref[...] = alpha * acc_ref[...] + jnp.dot(vt, p.astype(BF16), preferred_element_type=F32)
    m_ref[...] = m_new

  @pl.when(ki < qi)
  def _():
    update(False)

  @pl.when(ki == qi)
  def _():
    update(True)
    lam = lam_ref[0]
    o = acc_ref[...] * (1.0 / l_ref[...])
    for r in range(2):
      d = o[:, (2 * r) * tq:(2 * r + 1) * tq] - lam * o[:, (2 * r + 1) * tq:(2 * r + 2) * tq]
      var = jnp.mean(d * d, axis=0, keepdims=True)
      dn = d * (lax.rsqrt(var + RMS_EPS) * (1.0 - LAM_INIT))
      o_ref[:, r * HEAD_W:(r + 1) * HEAD_W] = (dn.T * g_ref[...]).astype(o_ref.dtype)


def _prompt_attention(proj, lam, subln_g, batch, seq):
  tq = tk = _pick_block(seq, 512, LANES)
  nq = seq // tq
  gw = 2 * HEAD_W
  body = functools.partial(_pattn_body, tq=tq, tk=tk)
  pairs = [(qi, ki) for qi in range(nq) for ki in range(qi + 1)]
  qi_of = jnp.asarray([p[0] for p in pairs], jnp.int32)
  ki_of = jnp.asarray([p[1] for p in pairs], jnp.int32)
  grid_spec = pltpu.PrefetchScalarGridSpec(
      num_scalar_prefetch=2,
      grid=(batch, N_KV_HEADS, len(pairs)),
      in_specs=[
          pl.BlockSpec(memory_space=pltpu.SMEM),
          pl.BlockSpec((tq, gw), lambda b, g, p, qo, ko: (b * nq + qo[p], g)),
          pl.BlockSpec((tk, HEAD_W), lambda b, g, p, qo, ko: (b * nq + ko[p], OFF_Q // HEAD_W + g)),
          pl.BlockSpec((tk, HEAD_W), lambda b, g, p, qo, ko: (b * nq + ko[p], OFF_K // HEAD_W + g)),
          pl.BlockSpec((1, HEAD_W), lambda b, g, p, qo, ko: (0, 0)),
      ],
      out_specs=pl.BlockSpec((tq, gw), lambda b, g, p, qo, ko: (b * nq + qo[p], g)),
      scratch_shapes=[
          pltpu.VMEM((4 * tq, HEAD_W), BF16),
          pltpu.VMEM((1, 4 * tq), F32),
          pltpu.VMEM((1, 4 * tq), F32),
          pltpu.VMEM((HEAD_W, 4 * tq), F32),
      ],
  )
  return pl.pallas_call(
      body,
      grid_spec=grid_spec,
      out_shape=jax.ShapeDtypeStruct((batch * seq, ATT_WIDTH), BF16),
      compiler_params=_params(("parallel", "parallel", "arbitrary")),
      name="prompt_attention",
  )(qi_of, ki_of, lam, proj, proj, proj, subln_g)


PAGES_PER_CHUNK = 8
SATTN_SLOTS = 3
PAGE_ROWS = PAGE_SIZE * N_KV_HEADS


def _sattn_body(pt_ref, lam_ref, wq_ref, kn_ref, vn_ref, g_ref, ck_ref, cv_ref, o_ref,
                kbuf, vbuf, bias_ref, sem, *, n_chunks, dec_seq):
  b = pl.program_id(0)
  nb = pl.num_programs(0)
  rows = wq_ref.shape[1]
  gr = rows // N_KV_HEADS
  cols = PAGES_PER_CHUNK * PAGE_ROWS

  def copies(bb, c, slot):
    out = []
    for p in range(PAGES_PER_CHUNK):
      page = pt_ref[bb, c * PAGES_PER_CHUNK + p]
      out.append(pltpu.make_async_copy(ck_ref.at[page], kbuf.at[slot, p], sem.at[0, slot]))
      out.append(pltpu.make_async_copy(cv_ref.at[page], vbuf.at[slot, p], sem.at[1, slot]))
    return out

  def start(bb, c, slot):
    for n, cp in enumerate(copies(bb, c, slot)):
      cp.start(priority=n % 2)

  def same_head(shape):
    head_of_col = jnp.bitwise_and(lax.broadcasted_iota(jnp.int32, shape, 1), N_KV_HEADS - 1)
    head_of_row = lax.broadcasted_iota(jnp.int32, shape, 0) // gr
    return head_of_col == head_of_row

  ahead = SATTN_SLOTS - 1

  def slot_of(bb, c):
    return lax.rem(bb * n_chunks + c, SATTN_SLOTS)

  @pl.when(b == 0)
  def _():
    for c in range(ahead):
      start(0, c, c)
    bias_ref[...] = jnp.where(same_head((rows, cols)), 0.0, NEG_INF)

  wq = wq_ref[0]

  def softmax_step(carry, s, v):
    m_old, l_old, acc = carry
    m_new = jnp.maximum(m_old, jnp.max(s, axis=1, keepdims=True))
    alpha = jnp.exp(m_old - m_new)
    p = jnp.exp(s - m_new)
    l_new = alpha * l_old + jnp.sum(p, axis=1, keepdims=True)
    acc = alpha * acc + jnp.dot(p.astype(BF16), v, preferred_element_type=F32)
    return m_new, l_new, acc

  def chunk(c, carry):
    slot = slot_of(b, c)

    @pl.when(c + ahead < n_chunks)
    def _():
      start(b, c + ahead, slot_of(b, c + ahead))

    @pl.when(jnp.logical_and(c + ahead >= n_chunks, b + 1 < nb))
    def _():
      start(b + 1, c + ahead - n_chunks, slot_of(b + 1, c + ahead - n_chunks))

    for cp in copies(b, c, slot):
      cp.wait()
    kc = kbuf[slot].reshape(cols, HEAD_W).astype(BF16)
    vc = vbuf[slot].reshape(cols, HEAD_W).astype(BF16)
    s = lax.dot_general(wq, kc, NT_DIMS, preferred_element_type=F32) + bias_ref[...]
    return softmax_step(carry, s, vc)

  init = (jnp.full((rows, 1), NEG_INF, F32), jnp.zeros((rows, 1), F32), jnp.zeros((rows, HEAD_W), F32))
  carry = lax.fori_loop(0, n_chunks, chunk, init)

  new_rows = kn_ref.shape[1]
  s = lax.dot_general(wq, kn_ref[0], NT_DIMS, preferred_element_type=F32)
  t_of_row = lax.broadcasted_iota(jnp.int32, (rows, new_rows), 0) % dec_seq
  t_of_col = lax.broadcasted_iota(jnp.int32, (rows, new_rows), 1) // N_KV_HEADS
  visible = jnp.logical_and(same_head((rows, new_rows)), t_of_col <= t_of_row)
  _, l_fin, acc = softmax_step(carry, jnp.where(visible, s, NEG_INF), vn_ref[0])

  o = acc / l_fin
  lam = lam_ref[0]
  for g in range(N_KV_HEADS):
    blk = o[g * gr:(g + 1) * gr]
    d = blk[:gr // 2] - lam * blk[gr // 2:]
    o_ref[0, g * (gr // 2):(g + 1) * (gr // 2), :] = _subln(d, g_ref[...])


def _sample_attention(page_table, lam, wq, k_new, v_new, subln_g, cache_k, cache_v, dec_seq):
  dec_batch, n_pages = page_table.shape
  assert n_pages % PAGES_PER_CHUNK == 0
  n_chunks = n_pages // PAGES_PER_CHUNK
  assert n_chunks >= SATTN_SLOTS - 1
  rows = wq.shape[1]
  new_rows = k_new.shape[1]
  body = functools.partial(_sattn_body, n_chunks=n_chunks, dec_seq=dec_seq)
  grid_spec = pltpu.PrefetchScalarGridSpec(
      num_scalar_prefetch=1,
      grid=(dec_batch,),
      in_specs=[
          pl.BlockSpec(memory_space=pltpu.SMEM),
          pl.BlockSpec((1, rows, HEAD_W), lambda b, pt: (b, 0, 0)),
          pl.BlockSpec((1, new_rows, HEAD_W), lambda b, pt: (b, 0, 0)),
          pl.BlockSpec((1, new_rows, HEAD_W), lambda b, pt: (b, 0, 0)),
          pl.BlockSpec((1, HEAD_W), lambda b, pt: (0, 0)),
          pl.BlockSpec(memory_space=pl.ANY),
          pl.BlockSpec(memory_space=pl.ANY),
      ],
      out_specs=pl.BlockSpec((1, rows // 2, HEAD_W), lambda b, pt: (b, 0, 0)),
      scratch_shapes=[
          pltpu.VMEM((SATTN_SLOTS, PAGES_PER_CHUNK, PAGE_ROWS, HEAD_W), F32),
          pltpu.VMEM((SATTN_SLOTS, PAGES_PER_CHUNK, PAGE_ROWS, HEAD_W), F32),
          pltpu.VMEM((rows, PAGES_PER_CHUNK * PAGE_ROWS), F32),
          pltpu.SemaphoreType.DMA((2, SATTN_SLOTS)),
      ],
  )
  return pl.pallas_call(
      body,
      grid_spec=grid_spec,
      out_shape=jax.ShapeDtypeStruct((dec_batch, rows // 2, HEAD_W), F32),
      compiler_params=_params(("arbitrary",)),
      name="sample_attention",
  )(page_table, lam, wq, k_new, v_new, subln_g, cache_k, cache_v)


def _lru_gates(xc, wa_ref, ba, wx_ref, bx, lam):
  ra, ix = [], []
  for n in range(LRU_BLOCKS):
    xb = xc[:, n * LRU_BLOCK_W:(n + 1) * LRU_BLOCK_W].astype(BF16)
    ra.append(jnp.dot(xb, wa_ref[n].astype(BF16), preferred_element_type=F32))
    ix.append(jnp.dot(xb, wx_ref[n].astype(BF16), preferred_element_type=F32))
  r = jax.nn.sigmoid(jnp.concatenate(ra, axis=1) + ba)
  i = jax.nn.sigmoid(jnp.concatenate(ix, axis=1) + bx)
  neg = -lam
  softplus = jnp.maximum(neg, 0.0) + jnp.log1p(jnp.exp(-jnp.abs(neg)))
  log_a = -LRU_C * r * softplus
  a = jnp.exp(log_a)
  u = jnp.sqrt(1.0 - a * a) * (i * xc)
  return a, u


def _conv(rows_of, cw_ref, cb):
  out = rows_of(0) * cw_ref[0:1, :]
  for j in range(1, CONV_W):
    out = out + rows_of(j) * cw_ref[j:j + 1, :]
  return out + cb


def _lru_prompt_body(xl_ref, cw_ref, cb_ref, wa_ref, ba_ref, wx_ref, bx_ref, lam_ref, y_ref, hl_ref,
                     ext_ref, a_ref, u_ref, hs_ref, h_ref, *, tt):
  ti = pl.program_id(1)
  head = SUBLANES

  @pl.when(ti == 0)
  def _():
    ext_ref[0:head, :] = jnp.zeros((head, LRU_WIDTH), F32)
    h_ref[...] = jnp.zeros((1, LRU_WIDTH), F32)

  ext_ref[head:head + tt, :] = xl_ref[...]
  xc = _conv(lambda j: ext_ref[head - (CONV_W - 1) + j:head - (CONV_W - 1) + j + tt, :], cw_ref, cb_ref[...])
  a, u = _lru_gates(xc, wa_ref, ba_ref[...], wx_ref, bx_ref[...], lam_ref[...])
  a_ref[...] = a
  u_ref[...] = u

  row = lax.broadcasted_iota(jnp.int32, (SUBLANES, LRU_WIDTH), 0)

  def tile_steps(i, h):
    rows = pl.ds(pl.multiple_of(i * SUBLANES, SUBLANES), SUBLANES)
    a_cum, u_cum = a_ref[rows, :], u_ref[rows, :]
    d = 1
    while d < SUBLANES:
      keep = row >= d
      u_cum = jnp.where(keep, a_cum * pltpu.roll(u_cum, d, 0) + u_cum, u_cum)
      a_cum = jnp.where(keep, a_cum * pltpu.roll(a_cum, d, 0), a_cum)
      d *= 2
    hs = a_cum * h + u_cum
    hs_ref[rows, :] = hs
    return hs[SUBLANES - 1:SUBLANES, :]

  h_fin = lax.fori_loop(0, tt // SUBLANES, tile_steps, h_ref[...], unroll=4)
  h_ref[...] = h_fin
  y_ref[...] = hs_ref[...].astype(y_ref.dtype)
  ext_ref[0:head, :] = ext_ref[tt:tt + head, :]

  @pl.when(ti == pl.num_programs(1) - 1)
  def _():
    hl_ref[0] = h_fin


def _lru_weight_specs(nidx):
  zero2 = (lambda *a: (0, 0))
  zero3 = (lambda *a: (0, 0, 0))
  del nidx
  return [
      pl.BlockSpec((CONV_W, LRU_WIDTH), zero2),
      pl.BlockSpec((1, LRU_WIDTH), zero2),
      pl.BlockSpec((LRU_BLOCKS, LRU_BLOCK_W, LRU_BLOCK_W), zero3),
      pl.BlockSpec((1, LRU_WIDTH), zero2),
      pl.BlockSpec((LRU_BLOCKS, LRU_BLOCK_W, LRU_BLOCK_W), zero3),
      pl.BlockSpec((1, LRU_WIDTH), zero2),
      pl.BlockSpec((1, LRU_WIDTH), zero2),
  ]


def _lru_prompt(proj, lru_w, batch, seq):
  tt = _pick_block(seq, 512, LANES)
  nt = seq // tt
  body = functools.partial(_lru_prompt_body, tt=tt)
  return pl.pallas_call(
      body,
      grid=(batch, nt),
      in_specs=[pl.BlockSpec((tt, LRU_WIDTH), lambda b, ti: (b * nt + ti, OFF_V // LRU_WIDTH))]
      + _lru_weight_specs(2),
      out_specs=[
          pl.BlockSpec((tt, LRU_WIDTH), lambda b, ti: (b * nt + ti, 0)),
          pl.BlockSpec((1, 1, LRU_WIDTH), lambda b, ti: (b, 0, 0)),
      ],
      out_shape=[
          jax.ShapeDtypeStruct((batch * seq, LRU_WIDTH), BF16),
          jax.ShapeDtypeStruct((batch, 1, LRU_WIDTH), F32),
      ],
      scratch_shapes=[
          pltpu.VMEM((tt + 2 * SUBLANES, LRU_WIDTH), F32),
          pltpu.VMEM((tt, LRU_WIDTH), F32),
          pltpu.VMEM((tt, LRU_WIDTH), F32),
          pltpu.VMEM((tt, LRU_WIDTH), F32),
          pltpu.VMEM((1, LRU_WIDTH), F32),
      ],
      compiler_params=_params(("parallel", "arbitrary")),
      name="lru_prompt",
  )(proj, *lru_w)


def _lru_sample_body(xl_ref, cbuf_ref, h0_ref, cw_ref, cb_ref, wa_ref, ba_ref, wx_ref, bx_ref, lam_ref,
                     y_ref, hl_ref, *, dec_seq):
  ext = [cbuf_ref[j] for j in range(CONV_W - 1)] + [xl_ref[t] for t in range(dec_seq)]
  h = h0_ref[...]
  for t in range(dec_seq):
    xc = _conv(lambda j: ext[t + j], cw_ref, cb_ref[...])
    a, u = _lru_gates(xc, wa_ref, ba_ref[...], wx_ref, bx_ref[...], lam_ref[...])
    h = a * h + u
    y_ref[t] = h.astype(y_ref.dtype)
  hl_ref[...] = h


def _lru_sample(xl_t, cbuf_t, h0, lru_w):
  dec_seq, dec_batch, _ = xl_t.shape
  body = functools.partial(_lru_sample_body, dec_seq=dec_seq)
  return pl.pallas_call(
      body,
      out_shape=[
          jax.ShapeDtypeStruct((dec_seq, dec_batch, LRU_WIDTH), BF16),
          jax.ShapeDtypeStruct((dec_batch, LRU_WIDTH), F32),
      ],
      name="lru_sample",
  )(xl_t, cbuf_t, h0, *lru_w)


def _merge_body(attp_ref, atts_ref, lrup_ref, lrus_ref, ga0_ref, ga1_ref, gl0_ref, gl1_ref, xp_ref, xs_ref,
                wa_ref, wl_ref, wo_ref, g2_ref, h_ref, xn_ref, att_buf, lru_buf, x_buf, *, rem):
  i = pl.program_id(0)
  n = pl.num_programs(0)

  def fill(buf):
    def emit(rows, v):
      buf[rows, :] = v
    return emit

  _stacked_rows(i, n, rem, attp_ref, atts_ref, fill(att_buf))
  _stacked_rows(i, n, rem, lrup_ref, lrus_ref, fill(lru_buf))
  _stacked_rows(i, n, rem, xp_ref, xs_ref, fill(x_buf))
  a1 = jnp.dot(att_buf[...], wa_ref[...], preferred_element_type=F32)
  a2 = jnp.dot(lru_buf[...], wl_ref[...], preferred_element_type=F32)
  ga = jnp.concatenate([ga0_ref[...], ga1_ref[...]], axis=1)
  gl = jnp.concatenate([gl0_ref[...], gl1_ref[...]], axis=1)
  m = jax.nn.sigmoid(ga) * a1 + jax.nn.sigmoid(gl) * a2
  h = x_buf[...] + jnp.dot(m.astype(BF16), wo_ref[...], preferred_element_type=F32)
  h_ref[...] = h
  xn_ref[...] = _rms(h, g2_ref[...]).astype(BF16)


def _merge(att_p, att_s, lru_p, lru_s, proj, x_p, x_s, wa, wl, wo, g2):
  tp, ts = x_p.shape[0], x_s.shape[0]
  t = tp + ts
  tm = _pick_block(t, 320, 64)
  half = D_MODEL // 2
  const = lambda shape: pl.BlockSpec(shape, lambda i: (0, 0), pipeline_mode=pl.Buffered(1))
  gate = lambda blk: pl.BlockSpec((tm, half), lambda i: (i, blk))
  return pl.pallas_call(
      functools.partial(_merge_body, rem=_tail_rows(tp, ts, tm)),
      grid=(t // tm,),
      in_specs=[
          _prompt_rows_spec(tm, ATT_WIDTH, tp), _sample_rows_spec(ts, ATT_WIDTH),
          _prompt_rows_spec(tm, LRU_WIDTH, tp), _sample_rows_spec(ts, LRU_WIDTH),
          gate(OFF_L // half), gate(OFF_L // half + 1), gate(OFF_GA // half), gate(OFF_GA // half + 1),
          _prompt_rows_spec(tm, D_MODEL, tp), _sample_rows_spec(ts, D_MODEL),
          const((ATT_WIDTH, D_MODEL)), const((LRU_WIDTH, D_MODEL)), const((D_MODEL, D_MODEL)),
          const((1, D_MODEL)),
      ],
      scratch_shapes=[
          pltpu.VMEM((tm, ATT_WIDTH), BF16),
          pltpu.VMEM((tm, LRU_WIDTH), BF16),
          pltpu.VMEM((tm, D_MODEL), F32),
      ],
      out_specs=[
          pl.BlockSpec((tm, D_MODEL), lambda i: (i, 0)),
          pl.BlockSpec((tm, D_MODEL), lambda i: (i, 0)),
      ],
      out_shape=[
          jax.ShapeDtypeStruct((t, D_MODEL), F32),
          jax.ShapeDtypeStruct((t, D_MODEL), BF16),
      ],
      compiler_params=_params(("parallel",)),
      name="merge",
  )(att_p, att_s, lru_p, lru_s, proj, proj, proj, proj, x_p, x_s, wa, wl, wo, g2)


def _mm_body(x_ref, w_ref, o_ref):
  o_ref[...] = jnp.dot(x_ref[...], w_ref[...], preferred_element_type=F32)


def _matmul(x, w):
  t, kdim = x.shape
  n = w.shape[1]
  tm = _pick_block(t, 640, LANES)
  tn = 512
  return pl.pallas_call(
      _mm_body,
      grid=(t // tm, n // tn),
      in_specs=[pl.BlockSpec((tm, kdim), lambda i, j: (i, 0)), pl.BlockSpec((kdim, tn), lambda i, j: (0, j))],
      out_specs=pl.BlockSpec((tm, tn), lambda i, j: (i, j)),
      out_shape=jax.ShapeDtypeStruct((t, n), F32),
      compiler_params=_params(("parallel", "arbitrary")),
      name="peer_query",
  )(x, w)


def _top16_rows(s, val_ref, idx_ref, lane0):
  n, w = s.shape
  sub = lax.broadcasted_iota(jnp.int32, s.shape, 0).astype(F32)
  for k in range(PEER_TOPK):
    m = jnp.max(s, axis=0, keepdims=True)
    idx = jnp.min(jnp.where(s == m, sub, float(n)), axis=0, keepdims=True)
    val_ref[k:k + 1, lane0:lane0 + w] = m
    idx_ref[k:k + 1, lane0:lane0 + w] = idx
    s = jnp.where(sub == idx, NEG_INF, s)


def _odd_even_merge_sort_pairs(n):
  pairs = []
  p = 1
  while p < n:
    k = p
    while k >= 1:
      for j in range(k % p, n - k, 2 * k):
        for i in range(min(k, n - j - k)):
          if (i + j) // (2 * p) == (i + j + k) // (2 * p):
            pairs.append((i + j, i + j + k))
      k //= 2
    p *= 2
  return pairs


def _top16_distinct(s, val_ref, idx_ref, lane0):
  n, w = s.shape
  depth = n // SUBLANES
  assert depth == PEER_TOPK
  sub = lax.broadcasted_iota(jnp.int32, (SUBLANES, w), 0).astype(F32)
  col = [s[g * SUBLANES:(g + 1) * SUBLANES, :] for g in range(depth)]
  cid = [sub + float(g * SUBLANES) for g in range(depth)]
  for a, b in _odd_even_merge_sort_pairs(depth):
    up = col[b] > col[a]
    col[a], col[b] = jnp.where(up, col[b], col[a]), jnp.where(up, col[a], col[b])
    cid[a], cid[b] = jnp.where(up, cid[b], cid[a]), jnp.where(up, cid[a], cid[b])
  vals = []
  for t in range(PEER_TOPK):
    m = jnp.max(col[0], axis=0, keepdims=True)
    hit = col[0] == m
    vals.append(m)
    val_ref[t:t + 1, lane0:lane0 + w] = m
    idx_ref[t:t + 1, lane0:lane0 + w] = jnp.max(jnp.where(hit, cid[0], -1.0), axis=0, keepdims=True)
    for k in range(depth - 1 - t):
      col[k] = jnp.where(hit, col[k + 1], col[k])
      cid[k] = jnp.where(hit, cid[k + 1], cid[k])
  tied = jnp.zeros((1, w), F32)
  for t in range(PEER_TOPK - 1):
    tied = jnp.where(vals[t] == vals[t + 1], 1.0, tied)
  at_least_last = jnp.sum(jnp.where(s >= vals[-1], 1.0, 0.0), axis=0, keepdims=True)
  return jnp.where(at_least_last > float(PEER_TOPK), 1.0, tied)


PEER_HEADS_PER_ITER = 2


def _topk_body(q_ref, k1_ref, k2_ref, eid_ref, gate_ref, val_ref, idx_ref, best_ref, sel_ref, eid_s, gate_s):
  tb = q_ref.shape[0]
  kk = PEER_TOPK
  half_w = N_KEYS
  hp = PEER_HEADS_PER_ITER
  wide = hp * tb

  def head_group(hg, carry):
    def scores(hh, c):
      col = pl.multiple_of((2 * (hg * hp + hh) + c) * half_w, half_w)
      qh = q_ref[:, pl.ds(col, half_w)]
      return lax.dot_general((k1_ref, k2_ref)[c][...], qh, NT_DIMS, preferred_element_type=F32,
                             precision=lax.Precision.HIGHEST)

    halves = [(hh, c) for hh in range(hp) for c in range(2)]
    tied = [_top16_distinct(scores(hh, c), val_ref, idx_ref, (2 * hh + c) * tb) for hh, c in halves]

    @pl.when(jnp.max(jnp.concatenate(tied, axis=1)) > 0.0)
    def _():
      for hh, c in halves:
        _top16_rows(scores(hh, c), val_ref, idx_ref, (2 * hh + c) * tb)

    def pick(ref, c):
      return jnp.concatenate([ref[:, (2 * hh + c) * tb:(2 * hh + c + 1) * tb] for hh in range(hp)], axis=1)

    v1, v2 = pick(val_ref, 0), pick(val_ref, 1)
    i1, i2 = pick(idx_ref, 0), pick(idx_ref, 1)
    b16 = lax.broadcasted_iota(jnp.int32, (kk, wide), 0).astype(F32)
    b8 = lax.broadcasted_iota(jnp.int32, (SUBLANES, wide), 0).astype(F32)
    vals = [v1[0:1] + v2]
    flat = [b16]
    code = [i1[0:1] * N_KEYS + i2]
    for a in range(1, SUBLANES):
      vals.append(v1[a:a + 1] + v2[0:SUBLANES])
      flat.append(a * kk + b8)
      code.append(i1[a:a + 1] * N_KEYS + i2[0:SUBLANES])
    vals.append(v1[SUBLANES:kk] + v2[0:1])
    flat.append((b8 + SUBLANES) * kk)
    code.append(i1[SUBLANES:kk] * N_KEYS + i2[0:1])
    cand = jnp.concatenate(vals, axis=0)
    flat = jnp.concatenate(flat, axis=0)
    code = jnp.concatenate(code, axis=0)
    for k in range(kk):
      m = jnp.max(cand, axis=0, keepdims=True)
      fsel = jnp.min(jnp.where(cand == m, flat, float(kk * kk)), axis=0, keepdims=True)
      hit = flat == fsel
      best_ref[k:k + 1, :] = m
      sel_ref[k:k + 1, :] = jnp.max(jnp.where(hit, code, -1.0), axis=0, keepdims=True)
      cand = jnp.where(hit, NEG_INF, cand)
    best = best_ref[...]
    e = jnp.exp(best - best[0:1])
    gate = e / jnp.sum(e, axis=0, keepdims=True)
    eid = sel_ref[...].astype(jnp.int32)
    for hh in range(hp):
      row = pl.multiple_of((hg * hp + hh) * kk, kk)
      gate_s[pl.ds(row, kk), :] = gate[:, hh * tb:(hh + 1) * tb]
      eid_s[pl.ds(row, kk), :] = eid[:, hh * tb:(hh + 1) * tb]
    return carry

  lax.fori_loop(0, PEER_HEADS // hp, head_group, 0)
  gate_ref[...] = gate_s[...].T
  eid_ref[...] = eid_s[...].T


def _peer_topk(qp, k1, k2):
  t = qp.shape[0]
  tb = LANES
  return pl.pallas_call(
      _topk_body,
      grid=(t // tb,),
      in_specs=[
          pl.BlockSpec((tb, qp.shape[1]), lambda i: (i, 0)),
          pl.BlockSpec((N_KEYS, N_KEYS), lambda i: (0, 0)),
          pl.BlockSpec((N_KEYS, N_KEYS), lambda i: (0, 0)),
      ],
      out_specs=[
          pl.BlockSpec((tb, PEER_SEL), lambda i: (i, 0)),
          pl.BlockSpec((tb, PEER_SEL), lambda i: (i, 0)),
      ],
      out_shape=[
          jax.ShapeDtypeStruct((t, PEER_SEL), jnp.int32),
          jax.ShapeDtypeStruct((t, PEER_SEL), F32),
      ],
      scratch_shapes=[
          pltpu.VMEM((PEER_TOPK, 2 * PEER_HEADS_PER_ITER * tb), F32),
          pltpu.VMEM((PEER_TOPK, 2 * PEER_HEADS_PER_ITER * tb), F32),
          pltpu.VMEM((PEER_TOPK, PEER_HEADS_PER_ITER * tb), F32),
          pltpu.VMEM((PEER_TOPK, PEER_HEADS_PER_ITER * tb), F32),
          pltpu.VMEM((PEER_SEL, tb), jnp.int32),
          pltpu.VMEM((PEER_SEL, tb), F32),
      ],
      compiler_params=_params(("parallel",)),
      name="peer_topk",
  )(qp, k1, k2)


def _gate_matrix_body(eid_ref, gate_ref, g_ref):
  tb = eid_ref.shape[0]
  sub = lax.broadcasted_iota(jnp.int32, (N_KEYS, PEER_SEL), 0)

  def token_group(gi, carry):
    tiles = []
    for tau in range(SUBLANES):
      t = gi * SUBLANES + tau
      e = eid_ref[pl.ds(t, 1), :]
      gt = gate_ref[pl.ds(t, 1), :]
      at = jnp.where(sub == lax.shift_right_logical(e, KEY_BITS), gt, 0.0).astype(BF16)
      bt = jnp.where(sub == jnp.bitwise_and(e, N_KEYS - 1), 1.0, 0.0).astype(BF16)
      tiles.append(lax.dot_general(at, bt, NT_DIMS, preferred_element_type=F32))
    blocks = jnp.stack([jnp.stack([tile[h * SUBLANES:(h + 1) * SUBLANES, :] for tile in tiles])
                        for h in range(N_KEYS // SUBLANES)])
    g_ref[gi] = jnp.swapaxes(blocks, 1, 2).reshape(N_KEYS, SUBLANES, N_KEYS)
    return carry

  lax.fori_loop(0, tb // SUBLANES, token_group, 0)


def _gate_matrix(eid_t, gate_t):
  t = eid_t.shape[0]
  tb = _pick_block(t, 64, SUBLANES)
  return pl.pallas_call(
      _gate_matrix_body,
      grid=(t // tb,),
      in_specs=[pl.BlockSpec((tb, PEER_SEL), lambda i: (i, 0)), pl.BlockSpec((tb, PEER_SEL), lambda i: (i, 0))],
      out_specs=pl.BlockSpec((tb // SUBLANES, N_KEYS, SUBLANES, N_KEYS), lambda i: (i, 0, 0, 0)),
      out_shape=jax.ShapeDtypeStruct((t // SUBLANES, N_KEYS, SUBLANES, N_KEYS), F32),
      compiler_params=_params(("parallel",)),
      name="peer_gate_matrix",
  )(eid_t, gate_t)


PEER_I1_PER_STEP = 8


def _peer_dense_body(x_ref, u_ref, v_ref, g_ref, h_ref, fg_ref, y_ref, ys_ref, *, rem):
  c = pl.program_id(1)
  s = lax.dot_general(x_ref[...], u_ref[...], NT_DIMS, preferred_element_type=F32)
  act = 0.5 * s * (1.0 + lax.erf(s * (2.0 ** -0.5)))
  tm = x_ref.shape[0]
  coef = jnp.concatenate(
      [g_ref[:, j].reshape(tm, N_KEYS) * act[:, j * N_KEYS:(j + 1) * N_KEYS] for j in range(PEER_I1_PER_STEP)],
      axis=1)
  contrib = jnp.dot(coef.astype(BF16), v_ref[...], preferred_element_type=F32)

  @pl.when(c == 0)
  def _():
    y_ref[...] = contrib

  @pl.when(c > 0)
  def _():
    y_ref[...] += contrib

  @pl.when(c == pl.num_programs(1) - 1)
  def _():
    y = _rms(h_ref[...] + y_ref[...], fg_ref[...])
    y_ref[...] = y

    @pl.when(pl.program_id(0) == pl.num_programs(0) - 1)
    def _():
      ys_ref[...] = y[rem:, :]


def _peer_dense(xn2, u_bf, v_bf, gmat, h, final_g, tp):
  t = xn2.shape[0]
  ts = t - tp
  tm = _pick_block(t, 640, LANES)
  te = PEER_I1_PER_STEP * N_KEYS
  return pl.pallas_call(
      functools.partial(_peer_dense_body, rem=_tail_rows(tp, ts, tm)),
      grid=(t // tm, N_EXPERTS // te),
      in_specs=[
          pl.BlockSpec((tm, D_MODEL), lambda i, c: (i, 0)),
          pl.BlockSpec((te, D_MODEL), lambda i, c: (c, 0)),
          pl.BlockSpec((te, D_MODEL), lambda i, c: (c, 0)),
          pl.BlockSpec((tm // SUBLANES, PEER_I1_PER_STEP, SUBLANES, N_KEYS), lambda i, c: (i, c, 0, 0)),
          pl.BlockSpec((tm, D_MODEL), lambda i, c: (i, 0), pipeline_mode=pl.Buffered(1)),
          pl.BlockSpec((1, D_MODEL), lambda i, c: (0, 0)),
      ],
      out_specs=[_prompt_rows_spec(tm, D_MODEL, tp), _sample_rows_spec(ts, D_MODEL)],
      out_shape=[jax.ShapeDtypeStruct((tp, D_MODEL), F32), jax.ShapeDtypeStruct((ts, D_MODEL), F32)],
      compiler_params=_params(("arbitrary", "arbitrary")),
      name="peer_dense",
  )(xn2, u_bf, v_bf, gmat, h, final_g)


def _rope_tables(pos):
  half = ROT_DIM // 2
  inv_freq = jnp.float32(ROPE_THETA) ** (-jnp.arange(half, dtype=F32) * 2.0 / ROT_DIM)
  ang = pos.astype(F32)[:, None] * inv_freq[None, :]
  cos, sin = jnp.cos(ang), jnp.sin(ang)
  n = pos.shape[0]
  ones = jnp.ones((n, HEAD_DIM - ROT_DIM), F32)
  zeros = jnp.zeros((n, HEAD_DIM - ROT_DIM), F32)
  zh = jnp.zeros((n, half), F32)
  cos_c = jnp.concatenate([cos, cos, ones], axis=1)
  sa_c = jnp.concatenate([-sin, zh, zeros], axis=1)
  sb_c = jnp.concatenate([zh, sin, zeros], axis=1)
  two = lambda a: jnp.concatenate([a, a], axis=1)
  return two(cos_c), two(sa_c), two(sb_c)


def kernel(x_prompt, x_sample, cache_k, cache_v, state_conv, state_h, page_table, norm1_g, w_in, lambda_q1, lambda_k1, lambda_q2, lambda_k2, subln_g, conv_w, conv_b, lru_wa, lru_ba, lru_wx, lru_bx, lru_lambda, w_att_up, w_lru_up, w_out, norm2_g, peer_wq, peer_k1, peer_k2, peer_u, peer_v, final_g):
  batch, seq, _ = x_prompt.shape
  dec_batch, dec_seq, _ = x_sample.shape
  n_pages = page_table.shape[1]
  past_len = n_pages * PAGE_SIZE
  tp = batch * seq
  ts = dec_batch * dec_seq
  assert w_in.shape[0] == 1, "one layer"

  x_p = x_prompt.reshape(tp, D_MODEL)
  x_s = x_sample.reshape(ts, D_MODEL)
  tabs_p = _rope_tables(jnp.arange(seq, dtype=jnp.int32))
  tabs_s = _rope_tables(past_len + jnp.arange(dec_seq, dtype=jnp.int32))
  cos_t, sa_t, sb_t = [jnp.concatenate([jnp.tile(a, (batch, 1)), jnp.tile(b, (dec_batch, 1))], axis=0)
                       for a, b in zip(tabs_p, tabs_s)]
  lam = (jnp.exp(jnp.sum(lambda_q1[0].astype(F32) * lambda_k1[0].astype(F32)))
         - jnp.exp(jnp.sum(lambda_q2[0].astype(F32) * lambda_k2[0].astype(F32))) + LAM_INIT).reshape(1)
  row = lambda a: a.reshape(1, -1)

  proj = _inproj(x_p, x_s, row(norm1_g[0]), w_in[0].astype(BF16), cos_t, sa_t, sb_t)

  att_p = _prompt_attention(proj, lam, row(subln_g[0]), batch, seq)
  proj_s = proj[tp:]
  q_s = proj_s[:, :OFF_Q].reshape(dec_batch, dec_seq, N_KV_HEADS, 2, 2, HEAD_DIM)
  q_s = q_s.transpose(0, 2, 4, 3, 1, 5).reshape(dec_batch, N_KV_HEADS, 2, 2 * dec_seq, HEAD_DIM)
  wq = jnp.einsum("bgcnd,ce->bgcned", q_s, jnp.eye(2, dtype=F32))
  wq = wq.reshape(dec_batch, N_KV_HEADS * 2 * 2 * dec_seq, HEAD_W).astype(BF16)
  new_rows = lambda a: a.reshape(dec_batch, dec_seq * N_KV_HEADS, HEAD_W).astype(BF16)
  k_new, v_new = proj_s[:, OFF_Q:OFF_K], proj_s[:, OFF_K:OFF_V]
  n_pool = cache_k.shape[1]
  att_s = _sample_attention(page_table, lam, wq, new_rows(k_new), new_rows(v_new), row(subln_g[0]),
                            cache_k.reshape(n_pool, PAGE_ROWS, HEAD_W),
                            cache_v.reshape(n_pool, PAGE_ROWS, HEAD_W), dec_seq)
  att_s = att_s.reshape(dec_batch, N_KV_HEADS, 2, dec_seq, HEAD_W).transpose(0, 3, 1, 2, 4)
  att_s = att_s.reshape(ts, ATT_WIDTH).astype(BF16)

  lru_w = (conv_w[0], row(conv_b[0]), lru_wa[0], row(lru_ba[0]), lru_wx[0], row(lru_bx[0]), row(lru_lambda[0]))
  lru_p, h_p = _lru_prompt(proj, lru_w, batch, seq)
  xl_s = proj_s[:, OFF_V:OFF_L].reshape(dec_batch, dec_seq, LRU_WIDTH)
  lru_s, h_s = _lru_sample(xl_s.transpose(1, 0, 2), state_conv[0].transpose(1, 0, 2), state_h[0], lru_w)
  lru_s = lru_s.transpose(1, 0, 2).reshape(ts, LRU_WIDTH)

  h_all, xn2 = _merge(att_p, att_s, lru_p, lru_s, proj, x_p, x_s, w_att_up[0].astype(BF16),
                      w_lru_up[0].astype(BF16), w_out[0].astype(BF16), row(norm2_g[0]))
  qp = _matmul(xn2, peer_wq[0].astype(BF16))
  eid_t, gate_t = _peer_topk(qp, peer_k1[0], peer_k2[0])
  gmat = _gate_matrix(eid_t, gate_t)
  y_p, y_s = _peer_dense(xn2, peer_u[0].astype(BF16), peer_v[0].astype(BF16), gmat, h_all, row(final_g), tp)

  kv_shape_p = (1, batch, seq, N_KV_HEADS, HEAD_W)
  kv_shape_s = (1, dec_batch, dec_seq, N_KV_HEADS, HEAD_W)
  tail = CONV_W - 1
  conv_p = jnp.stack([proj[(b + 1) * seq - tail:(b + 1) * seq, OFF_V:OFF_L] for b in range(batch)])
  conv_s = jnp.concatenate([state_conv[0].astype(F32), xl_s], axis=1)[:, -tail:]
  return (
      y_p.reshape(batch, seq, D_MODEL),
      y_s.reshape(dec_batch, dec_seq, D_MODEL),
      proj[:tp, OFF_Q:OFF_K].reshape(kv_shape_p),
      proj[:tp, OFF_K:OFF_V].reshape(kv_shape_p),
      conv_p[None],
      h_p.reshape(1, batch, LRU_WIDTH),
      k_new.reshape(kv_shape_s),
      v_new.reshape(kv_shape_s),
      conv_s[None],
      h_s[None],
  )
```

```python
import functools
import math

import jax
import jax.numpy as jnp
from jax import lax
from jax.experimental import pallas as pl
from jax.experimental.pallas import tpu as pltpu

F32 = jnp.float32
BF16 = jnp.bfloat16

D_MODEL = 2048
N_HEADS = 8
N_KV_HEADS = 4
HEAD_DIM = 64
HEAD_W = 2 * HEAD_DIM
ROT_DIM = HEAD_DIM // 4
ROPE_THETA = 500000.0
ATT_WIDTH = N_HEADS * HEAD_W
KV_WIDTH = N_KV_HEADS * HEAD_W
LRU_WIDTH = 1024
LRU_BLOCKS = 8
LRU_BLOCK_W = LRU_WIDTH // LRU_BLOCKS
CONV_W = 4
LRU_C = 8.0
N_KEYS = 128
KEY_BITS = 7
N_EXPERTS = N_KEYS * N_KEYS
PEER_HEADS = 8
PEER_TOPK = 16
PEER_SEL = PEER_HEADS * PEER_TOPK
PAGE_SIZE = 128
RMS_EPS = 1e-6
OFF_Q = ATT_WIDTH
OFF_K = OFF_Q + KV_WIDTH
OFF_V = OFF_K + KV_WIDTH
OFF_L = OFF_V + LRU_WIDTH
OFF_GA = OFF_L + D_MODEL
IN_WIDTH = OFF_GA + D_MODEL
LAM_INIT = 0.8 - 0.6 * math.exp(0.0)

LANES = 128
SUBLANES = 8
VMEM_LIMIT_BYTES = 56 * 1024 * 1024

NEG_INF = float("-inf")
LOG2_E = math.log2(math.e)
NT_DIMS = (((1,), (1,)), ((), ()))


def _pick_block(total, cap, quantum):
  best = None
  b = quantum
  while b <= min(cap, total):
    if total % b == 0:
      best = b
    b += quantum
  assert best is not None, (total, cap, quantum)
  return best


def _params(sem, vmem=VMEM_LIMIT_BYTES):
  return pltpu.CompilerParams(dimension_semantics=sem, vmem_limit_bytes=vmem)


def _rms(x, g):
  var = jnp.mean(x * x, axis=-1, keepdims=True)
  return x * lax.rsqrt(var + RMS_EPS) * g


def _tail_rows(tp, ts, tm):
  assert (tp + ts) % tm == 0 and 0 < ts < tm, (tp, ts, tm)
  return tm - ts


def _prompt_rows_spec(tm, width, tp):
  last_prompt_block = pl.cdiv(tp, tm) - 1
  return pl.BlockSpec((tm, width), lambda i, *_: (jnp.minimum(i, last_prompt_block), 0))


def _sample_rows_spec(ts, width):
  return pl.BlockSpec((ts, width), lambda i, *_: (0, 0))


def _stacked_rows(i, n_blocks, rem, p_ref, s_ref, emit):
  tm = rem + s_ref.shape[0]

  @pl.when(i < n_blocks - 1)
  def _():
    emit(slice(0, tm), p_ref[...])

  @pl.when(i == n_blocks - 1)
  def _():
    emit(slice(0, rem), p_ref[0:rem, :])
    emit(slice(rem, tm), s_ref[...])


def _inproj_body(xp_ref, xs_ref, g_ref, w_ref, cos_ref, sa_ref, sb_ref, o_ref, xn_ref, *, tn, n_rope_blocks, rem):
  i = pl.program_id(0)
  j = pl.program_id(1)

  @pl.when(j == 0)
  def _():
    def emit(rows, x):
      xn_ref[rows, :] = _rms(x, g_ref[...]).astype(BF16)
    _stacked_rows(i, pl.num_programs(0), rem, xp_ref, xs_ref, emit)

  acc = jnp.dot(xn_ref[...], w_ref[...], preferred_element_type=F32)

  @pl.when(j < n_rope_blocks)
  def _():
    reps = tn // LANES
    cos = jnp.concatenate([cos_ref[...]] * reps, axis=1)
    sa = jnp.concatenate([sa_ref[...]] * reps, axis=1)
    sb = jnp.concatenate([sb_ref[...]] * reps, axis=1)
    half = ROT_DIM // 2
    rot = acc * cos + pltpu.roll(acc, tn - half, 1) * sa + pltpu.roll(acc, half, 1) * sb
    col = j * tn + lax.broadcasted_iota(jnp.int32, acc.shape, 1)
    o_ref[...] = jnp.where(col < OFF_Q, rot * (HEAD_DIM ** -0.5), jnp.where(col < OFF_K, rot, acc))

  @pl.when(j >= n_rope_blocks)
  def _():
    o_ref[...] = acc


def _inproj(x_p, x_s, norm_g, w_in_bf, cos_t, sa_t, sb_t):
  tp, ts = x_p.shape[0], x_s.shape[0]
  t = tp + ts
  tm = _pick_block(t, 832, 64)
  tn = 1024
  body = functools.partial(_inproj_body, tn=tn, n_rope_blocks=pl.cdiv(OFF_K, tn), rem=_tail_rows(tp, ts, tm))
  return pl.pallas_call(
      body,
      grid=(t // tm, IN_WIDTH // tn),
      in_specs=[
          _prompt_rows_spec(tm, D_MODEL, tp),
          _sample_rows_spec(ts, D_MODEL),
          pl.BlockSpec((1, D_MODEL), lambda i, j: (0, 0)),
          pl.BlockSpec((D_MODEL, tn), lambda i, j: (0, j)),
          pl.BlockSpec((tm, LANES), lambda i, j: (i, 0)),
          pl.BlockSpec((tm, LANES), lambda i, j: (i, 0)),
          pl.BlockSpec((tm, LANES), lambda i, j: (i, 0)),
      ],
      out_specs=pl.BlockSpec((tm, tn), lambda i, j: (i, j)),
      out_shape=jax.ShapeDtypeStruct((t, IN_WIDTH), F32),
      scratch_shapes=[pltpu.VMEM((tm, D_MODEL), BF16)],
      compiler_params=_params(("parallel", "arbitrary")),
      name="inproj",
  )(x_p, x_s, norm_g, w_in_bf, cos_t, sa_t, sb_t)


def _subln(o, g):
  var = jnp.mean(o * o, axis=-1, keepdims=True)
  return o * lax.rsqrt(var + RMS_EPS) * g * (1.0 - LAM_INIT)


def _pattn_body(qi_ref, ki_ref, lam_ref, q_ref, k_ref, v_ref, g_ref, tu_ref, tv_ref, o_ref, tub_ref, tvb_ref,
                q4_ref, m_ref, l_ref, acc_ref, *, tq, tk, cast_steps):
  qi = qi_ref[pl.program_id(2)]
  ki = ki_ref[pl.program_id(2)]
  cols = 2 * 2 * tq

  @pl.when(pl.program_id(2) < cast_steps)
  def _():
    tub_ref[...] = tu_ref[...].astype(BF16)
    tvb_ref[...] = tv_ref[...].astype(BF16)

  @pl.when(ki == 0)
  def _():
    m_ref[...] = jnp.full((1, cols), NEG_INF, F32)
    l_ref[...] = jnp.zeros((1, cols), F32)
    acc_ref[...] = jnp.zeros((HEAD_W, cols), F32)
    lane = lax.broadcasted_iota(jnp.int32, (tq, HEAD_W), 1)
    for r in range(2):
      qh = q_ref[:, r * HEAD_W:(r + 1) * HEAD_W] * LOG2_E
      q4_ref[(2 * r) * tq:(2 * r + 1) * tq, :] = jnp.where(lane < HEAD_DIM, qh, 0.0).astype(BF16)
      q4_ref[(2 * r + 1) * tq:(2 * r + 2) * tq, :] = jnp.where(lane >= HEAD_DIM, qh, 0.0).astype(BF16)

  def update(diagonal):
    k = k_ref[...].astype(BF16)
    s = lax.dot_general(k, q4_ref[...], NT_DIMS, preferred_element_type=F32)
    if diagonal:
      visible = (lax.broadcasted_iota(jnp.int32, (tk, tq), 0) <= lax.broadcasted_iota(jnp.int32, (tk, tq), 1))
      s = jnp.concatenate([jnp.where(visible, s[:, j * tq:(j + 1) * tq], NEG_INF) for j in range(4)], axis=1)
    m_old = m_ref[...]
    m_new = jnp.maximum(m_old, jnp.max(s, axis=0, keepdims=True))
    alpha = jnp.exp2(m_old - m_new)
    p = jnp.exp2(s - m_new)
    l_ref[...] = alpha * l_ref[...] + jnp.sum(p, axis=0, keepdims=True)
    vt = v_ref[...].T.astype(BF16)
    acc_ref[...] = alpha * acc_ref[...] + jnp.dot(vt, p.astype(BF16), preferred_element_type=F32)
    m_ref[...] = m_new

  @pl.when(ki < qi)
  def _():
    update(False)

  @pl.when(ki == qi)
  def _():
    update(True)
    lam = lam_ref[0]
    o = acc_ref[...] * (1.0 / l_ref[...])
    for r in range(2):
      d = o[:, (2 * r) * tq:(2 * r + 1) * tq] - lam * o[:, (2 * r + 1) * tq:(2 * r + 2) * tq]
      var = jnp.mean(d * d, axis=0, keepdims=True)
      dn = d * (lax.rsqrt(var + RMS_EPS) * (1.0 - LAM_INIT))
      o_ref[:, r * HEAD_W:(r + 1) * HEAD_W] = (dn.T * g_ref[...]).astype(o_ref.dtype)


def _prompt_attention(proj, lam, subln_g, batch, seq, table_u, table_v):
  tq = tk = _pick_block(seq, 512, LANES)
  nq = seq // tq
  gw = 2 * HEAD_W
  pairs = [(qi, ki) for qi in range(nq) for ki in range(qi + 1)]
  qi_of = jnp.asarray([p[0] for p in pairs], jnp.int32)
  ki_of = jnp.asarray([p[1] for p in pairs], jnp.int32)
  n_groups = batch * N_KV_HEADS
  cast_steps = 1 << (len(pairs).bit_length() - 1)
  cast_rows = N_EXPERTS // (n_groups * cast_steps)
  assert cast_rows * n_groups * cast_steps == N_EXPERTS and cast_rows % (2 * SUBLANES) == 0
  table_spec = pl.BlockSpec(
      (cast_rows, D_MODEL),
      lambda b, g, p, qo, ko: ((b * N_KV_HEADS + g) * cast_steps + jnp.minimum(p, cast_steps - 1), 0))
  body = functools.partial(_pattn_body, tq=tq, tk=tk, cast_steps=cast_steps)
  grid_spec = pltpu.PrefetchScalarGridSpec(
      num_scalar_prefetch=2,
      grid=(batch, N_KV_HEADS, len(pairs)),
      in_specs=[
          pl.BlockSpec(memory_space=pltpu.SMEM),
          pl.BlockSpec((tq, gw), lambda b, g, p, qo, ko: (b * nq + qo[p], g)),
          pl.BlockSpec((tk, HEAD_W), lambda b, g, p, qo, ko: (b * nq + ko[p], OFF_Q // HEAD_W + g)),
          pl.BlockSpec((tk, HEAD_W), lambda b, g, p, qo, ko: (b * nq + ko[p], OFF_K // HEAD_W + g)),
          pl.BlockSpec((1, HEAD_W), lambda b, g, p, qo, ko: (0, 0)),
          table_spec, table_spec,
      ],
      out_specs=[pl.BlockSpec((tq, gw), lambda b, g, p, qo, ko: (b * nq + qo[p], g)), table_spec, table_spec],
      scratch_shapes=[
          pltpu.VMEM((4 * tq, HEAD_W), BF16),
          pltpu.VMEM((1, 4 * tq), F32),
          pltpu.VMEM((1, 4 * tq), F32),
          pltpu.VMEM((HEAD_W, 4 * tq), F32),
      ],
  )
  return pl.pallas_call(
      body,
      grid_spec=grid_spec,
      out_shape=[jax.ShapeDtypeStruct((batch * seq, ATT_WIDTH), BF16),
                 jax.ShapeDtypeStruct(table_u.shape, BF16), jax.ShapeDtypeStruct(table_v.shape, BF16)],
      compiler_params=_params(("parallel", "parallel", "arbitrary")),
      name="prompt_attention",
  )(qi_of, ki_of, lam, proj, proj, proj, subln_g, table_u, table_v)


PAGES_PER_CHUNK = 8
SATTN_SLOTS = 3
PAGE_ROWS = PAGE_SIZE * N_KV_HEADS


def _sattn_body(pt_ref, lam_ref, wq_ref, kn_ref, vn_ref, g_ref, ck_ref, cv_ref, o_ref,
                kbuf, vbuf, bias_ref, sem, *, n_chunks, dec_seq):
  b = pl.program_id(0)
  nb = pl.num_programs(0)
  rows = wq_ref.shape[1]
  gr = rows // N_KV_HEADS
  cols = PAGES_PER_CHUNK * PAGE_ROWS

  def copies(bb, c, slot):
    out = []
    for p in range(PAGES_PER_CHUNK):
      page = pt_ref[bb, c * PAGES_PER_CHUNK + p]
      out.append(pltpu.make_async_copy(ck_ref.at[page], kbuf.at[slot, p], sem.at[0, slot]))
      out.append(pltpu.make_async_copy(cv_ref.at[page], vbuf.at[slot, p], sem.at[1, slot]))
    return out

  def start(bb, c, slot):
    for n, cp in enumerate(copies(bb, c, slot)):
      cp.start(priority=n % 2)

  def same_head(shape):
    head_of_col = jnp.bitwise_and(lax.broadcasted_iota(jnp.int32, shape, 1), N_KV_HEADS - 1)
    head_of_row = lax.broadcasted_iota(jnp.int32, shape, 0) // gr
    return head_of_col == head_of_row

  ahead = SATTN_SLOTS - 1

  def slot_of(bb, c):
    return lax.rem(bb * n_chunks + c, SATTN_SLOTS)

  @pl.when(b == 0)
  def _():
    for c in range(ahead):
      start(0, c, c)
    bias_ref[...] = jnp.where(same_head((rows, cols)), 0.0, NEG_INF)

  wq = wq_ref[0]

  def softmax_step(carry, s, v):
    m_old, l_old, acc = carry
    m_new = jnp.maximum(m_old, jnp.max(s, axis=1, keepdims=True))
    alpha = jnp.exp(m_old - m_new)
    p = jnp.exp(s - m_new)
    l_new = alpha * l_old + jnp.sum(p, axis=1, keepdims=True)
    acc = alpha * acc + jnp.dot(p.astype(BF16), v, preferred_element_type=F32)
    return m_new, l_new, acc

  def chunk(c, carry):
    slot = slot_of(b, c)

    @pl.when(c + ahead < n_chunks)
    def _():
      start(b, c + ahead, slot_of(b, c + ahead))

    @pl.when(jnp.logical_and(c + ahead >= n_chunks, b + 1 < nb))
    def _():
      start(b + 1, c + ahead - n_chunks, slot_of(b + 1, c + ahead - n_chunks))

    for cp in copies(b, c, slot):
      cp.wait()
    kc = kbuf[slot].reshape(cols, HEAD_W).astype(BF16)
    vc = vbuf[slot].reshape(cols, HEAD_W).astype(BF16)
    s = lax.dot_general(wq, kc, NT_DIMS, preferred_element_type=F32) + bias_ref[...]
    return softmax_step(carry, s, vc)

  init = (jnp.full((rows, 1), NEG_INF, F32), jnp.zeros((rows, 1), F32), jnp.zeros((rows, HEAD_W), F32))
  carry = lax.fori_loop(0, n_chunks, chunk, init)

  new_rows = kn_ref.shape[1]
  s = lax.dot_general(wq, kn_ref[0], NT_DIMS, preferred_element_type=F32)
  t_of_row = lax.broadcasted_iota(jnp.int32, (rows, new_rows), 0) % dec_seq
  t_of_col = lax.broadcasted_iota(jnp.int32, (rows, new_rows), 1) // N_KV_HEADS
  visible = jnp.logical_and(same_head((rows, new_rows)), t_of_col <= t_of_row)
  _, l_fin, acc = softmax_step(carry, jnp.where(visible, s, NEG_INF), vn_ref[0])

  o = acc / l_fin
  lam = lam_ref[0]
  for g in range(N_KV_HEADS):
    blk = o[g * gr:(g + 1) * gr]
    d = blk[:gr // 2] - lam * blk[gr // 2:]
    o_ref[0, g * (gr // 2):(g + 1) * (gr // 2), :] = _subln(d, g_ref[...])


def _sample_attention(page_table, lam, wq, k_new, v_new, subln_g, cache_k, cache_v, dec_seq):
  dec_batch, n_pages = page_table.shape
  assert n_pages % PAGES_PER_CHUNK == 0
  n_chunks = n_pages // PAGES_PER_CHUNK
  assert n_chunks >= SATTN_SLOTS - 1
  rows = wq.shape[1]
  new_rows = k_new.shape[1]
  body = functools.partial(_sattn_body, n_chunks=n_chunks, dec_seq=dec_seq)
  grid_spec = pltpu.PrefetchScalarGridSpec(
      num_scalar_prefetch=1,
      grid=(dec_batch,),
      in_specs=[
          pl.BlockSpec(memory_space=pltpu.SMEM),
          pl.BlockSpec((1, rows, HEAD_W), lambda b, pt: (b, 0, 0)),
          pl.BlockSpec((1, new_rows, HEAD_W), lambda b, pt: (b, 0, 0)),
          pl.BlockSpec((1, new_rows, HEAD_W), lambda b, pt: (b, 0, 0)),
          pl.BlockSpec((1, HEAD_W), lambda b, pt: (0, 0)),
          pl.BlockSpec(memory_space=pl.ANY),
          pl.BlockSpec(memory_space=pl.ANY),
      ],
      out_specs=pl.BlockSpec((1, rows // 2, HEAD_W), lambda b, pt: (b, 0, 0)),
      scratch_shapes=[
          pltpu.VMEM((SATTN_SLOTS, PAGES_PER_CHUNK, PAGE_ROWS, HEAD_W), F32),
          pltpu.VMEM((SATTN_SLOTS, PAGES_PER_CHUNK, PAGE_ROWS, HEAD_W), F32),
          pltpu.VMEM((rows, PAGES_PER_CHUNK * PAGE_ROWS), F32),
          pltpu.SemaphoreType.DMA((2, SATTN_SLOTS)),
      ],
  )
  return pl.pallas_call(
      body,
      grid_spec=grid_spec,
      out_shape=jax.ShapeDtypeStruct((dec_batch, rows // 2, HEAD_W), F32),
      compiler_params=_params(("arbitrary",)),
      name="sample_attention",
  )(page_table, lam, wq, k_new, v_new, subln_g, cache_k, cache_v)


def _lru_gates(xc, wa_ref, ba, wx_ref, bx, lam):
  ra, ix = [], []
  for n in range(LRU_BLOCKS):
    xb = xc[:, n * LRU_BLOCK_W:(n + 1) * LRU_BLOCK_W].astype(BF16)
    ra.append(jnp.dot(xb, wa_ref[n].astype(BF16), preferred_element_type=F32))
    ix.append(jnp.dot(xb, wx_ref[n].astype(BF16), preferred_element_type=F32))
  r = jax.nn.sigmoid(jnp.concatenate(ra, axis=1) + ba)
  i = jax.nn.sigmoid(jnp.concatenate(ix, axis=1) + bx)
  neg = -lam
  softplus = jnp.maximum(neg, 0.0) + jnp.log1p(jnp.exp(-jnp.abs(neg)))
  log_a = -LRU_C * r * softplus
  a = jnp.exp(log_a)
  u = jnp.sqrt(1.0 - a * a) * (i * xc)
  return a, u


def _conv(rows_of, cw_ref, cb):
  out = rows_of(0) * cw_ref[0:1, :]
  for j in range(1, CONV_W):
    out = out + rows_of(j) * cw_ref[j:j + 1, :]
  return out + cb


def _lru_prompt_body(xl_ref, cw_ref, cb_ref, wa_ref, ba_ref, wx_ref, bx_ref, lam_ref, y_ref, hl_ref,
                     ext_ref, a_ref, u_ref, hs_ref, h_ref, *, tt):
  ti = pl.program_id(1)
  head = SUBLANES

  @pl.when(ti == 0)
  def _():
    ext_ref[0:head, :] = jnp.zeros((head, LRU_WIDTH), F32)
    h_ref[...] = jnp.zeros((1, LRU_WIDTH), F32)

  ext_ref[head:head + tt, :] = xl_ref[...]
  xc = _conv(lambda j: ext_ref[head - (CONV_W - 1) + j:head - (CONV_W - 1) + j + tt, :], cw_ref, cb_ref[...])
  a, u = _lru_gates(xc, wa_ref, ba_ref[...], wx_ref, bx_ref[...], lam_ref[...])
  a_ref[...] = a
  u_ref[...] = u

  row = lax.broadcasted_iota(jnp.int32, (SUBLANES, LRU_WIDTH), 0)

  def tile_steps(i, h):
    rows = pl.ds(pl.multiple_of(i * SUBLANES, SUBLANES), SUBLANES)
    a_cum, u_cum = a_ref[rows, :], u_ref[rows, :]
    d = 1
    while d < SUBLANES:
      keep = row >= d
      u_cum = jnp.where(keep, a_cum * pltpu.roll(u_cum, d, 0) + u_cum, u_cum)
      a_cum = jnp.where(keep, a_cum * pltpu.roll(a_cum, d, 0), a_cum)
      d *= 2
    hs = a_cum * h + u_cum
    hs_ref[rows, :] = hs
    return hs[SUBLANES - 1:SUBLANES, :]

  h_fin = lax.fori_loop(0, tt // SUBLANES, tile_steps, h_ref[...], unroll=4)
  h_ref[...] = h_fin
  y_ref[...] = hs_ref[...].astype(y_ref.dtype)
  ext_ref[0:head, :] = ext_ref[tt:tt + head, :]

  @pl.when(ti == pl.num_programs(1) - 1)
  def _():
    hl_ref[0] = h_fin


def _lru_weight_specs(nidx):
  zero2 = (lambda *a: (0, 0))
  zero3 = (lambda *a: (0, 0, 0))
  del nidx
  return [
      pl.BlockSpec((CONV_W, LRU_WIDTH), zero2),
      pl.BlockSpec((1, LRU_WIDTH), zero2),
      pl.BlockSpec((LRU_BLOCKS, LRU_BLOCK_W, LRU_BLOCK_W), zero3),
      pl.BlockSpec((1, LRU_WIDTH), zero2),
      pl.BlockSpec((LRU_BLOCKS, LRU_BLOCK_W, LRU_BLOCK_W), zero3),
      pl.BlockSpec((1, LRU_WIDTH), zero2),
      pl.BlockSpec((1, LRU_WIDTH), zero2),
  ]


def _lru_prompt(proj, lru_w, batch, seq):
  tt = _pick_block(seq, 512, LANES)
  nt = seq // tt
  body = functools.partial(_lru_prompt_body, tt=tt)
  return pl.pallas_call(
      body,
      grid=(batch, nt),
      in_specs=[pl.BlockSpec((tt, LRU_WIDTH), lambda b, ti: (b * nt + ti, OFF_V // LRU_WIDTH))]
      + _lru_weight_specs(2),
      out_specs=[
          pl.BlockSpec((tt, LRU_WIDTH), lambda b, ti: (b * nt + ti, 0)),
          pl.BlockSpec((1, 1, LRU_WIDTH), lambda b, ti: (b, 0, 0)),
      ],
      out_shape=[
          jax.ShapeDtypeStruct((batch * seq, LRU_WIDTH), BF16),
          jax.ShapeDtypeStruct((batch, 1, LRU_WIDTH), F32),
      ],
      scratch_shapes=[
          pltpu.VMEM((tt + 2 * SUBLANES, LRU_WIDTH), F32),
          pltpu.VMEM((tt, LRU_WIDTH), F32),
          pltpu.VMEM((tt, LRU_WIDTH), F32),
          pltpu.VMEM((tt, LRU_WIDTH), F32),
          pltpu.VMEM((1, LRU_WIDTH), F32),
      ],
      compiler_params=_params(("parallel", "arbitrary")),
      name="lru_prompt",
  )(proj, *lru_w)


def _lru_sample_body(xl_ref, cbuf_ref, h0_ref, cw_ref, cb_ref, wa_ref, ba_ref, wx_ref, bx_ref, lam_ref,
                     y_ref, hl_ref, *, dec_seq):
  ext = [cbuf_ref[j] for j in range(CONV_W - 1)] + [xl_ref[t] for t in range(dec_seq)]
  h = h0_ref[...]
  for t in range(dec_seq):
    xc = _conv(lambda j: ext[t + j], cw_ref, cb_ref[...])
    a, u = _lru_gates(xc, wa_ref, ba_ref[...], wx_ref, bx_ref[...], lam_ref[...])
    h = a * h + u
    y_ref[t] = h.astype(y_ref.dtype)
  hl_ref[...] = h


def _lru_sample(xl_t, cbuf_t, h0, lru_w):
  dec_seq, dec_batch, _ = xl_t.shape
  body = functools.partial(_lru_sample_body, dec_seq=dec_seq)
  return pl.pallas_call(
      body,
      out_shape=[
          jax.ShapeDtypeStruct((dec_seq, dec_batch, LRU_WIDTH), BF16),
          jax.ShapeDtypeStruct((dec_batch, LRU_WIDTH), F32),
      ],
      name="lru_sample",
  )(xl_t, cbuf_t, h0, *lru_w)


def _merge_body(attp_ref, atts_ref, lrup_ref, lrus_ref, ga0_ref, ga1_ref, gl0_ref, gl1_ref, xp_ref, xs_ref,
                wa_ref, wl_ref, wo_ref, g2_ref, h_ref, xn_ref, att_buf, lru_buf, x_buf, *, rem):
  i = pl.program_id(0)
  n = pl.num_programs(0)

  def fill(buf):
    def emit(rows, v):
      buf[rows, :] = v
    return emit

  _stacked_rows(i, n, rem, attp_ref, atts_ref, fill(att_buf))
  _stacked_rows(i, n, rem, lrup_ref, lrus_ref, fill(lru_buf))
  _stacked_rows(i, n, rem, xp_ref, xs_ref, fill(x_buf))
  a1 = jnp.dot(att_buf[...], wa_ref[...], preferred_element_type=F32)
  a2 = jnp.dot(lru_buf[...], wl_ref[...], preferred_element_type=F32)
  ga = jnp.concatenate([ga0_ref[...], ga1_ref[...]], axis=1)
  gl = jnp.concatenate([gl0_ref[...], gl1_ref[...]], axis=1)
  m = jax.nn.sigmoid(ga) * a1 + jax.nn.sigmoid(gl) * a2
  h = x_buf[...] + jnp.dot(m.astype(BF16), wo_ref[...], preferred_element_type=F32)
  h_ref[...] = h
  xn_ref[...] = _rms(h, g2_ref[...]).astype(BF16)


def _merge(att_p, att_s, lru_p, lru_s, proj, x_p, x_s, wa, wl, wo, g2):
  tp, ts = x_p.shape[0], x_s.shape[0]
  t = tp + ts
  tm = _pick_block(t, 320, 64)
  half = D_MODEL // 2
  const = lambda shape: pl.BlockSpec(shape, lambda i: (0, 0), pipeline_mode=pl.Buffered(1))
  gate = lambda blk: pl.BlockSpec((tm, half), lambda i: (i, blk))
  return pl.pallas_call(
      functools.partial(_merge_body, rem=_tail_rows(tp, ts, tm)),
      grid=(t // tm,),
      in_specs=[
          _prompt_rows_spec(tm, ATT_WIDTH, tp), _sample_rows_spec(ts, ATT_WIDTH),
          _prompt_rows_spec(tm, LRU_WIDTH, tp), _sample_rows_spec(ts, LRU_WIDTH),
          gate(OFF_L // half), gate(OFF_L // half + 1), gate(OFF_GA // half), gate(OFF_GA // half + 1),
          _prompt_rows_spec(tm, D_MODEL, tp), _sample_rows_spec(ts, D_MODEL),
          const((ATT_WIDTH, D_MODEL)), const((LRU_WIDTH, D_MODEL)), const((D_MODEL, D_MODEL)),
          const((1, D_MODEL)),
      ],
      scratch_shapes=[
          pltpu.VMEM((tm, ATT_WIDTH), BF16),
          pltpu.VMEM((tm, LRU_WIDTH), BF16),
          pltpu.VMEM((tm, D_MODEL), F32),
      ],
      out_specs=[
          pl.BlockSpec((tm, D_MODEL), lambda i: (i, 0)),
          pl.BlockSpec((tm, D_MODEL), lambda i: (i, 0)),
      ],
      out_shape=[
          jax.ShapeDtypeStruct((t, D_MODEL), F32),
          jax.ShapeDtypeStruct((t, D_MODEL), BF16),
      ],
      compiler_params=_params(("parallel",)),
      name="merge",
  )(att_p, att_s, lru_p, lru_s, proj, proj, proj, proj, x_p, x_s, wa, wl, wo, g2)


def _mm_body(x_ref, w_ref, o_ref):
  o_ref[...] = jnp.dot(x_ref[...], w_ref[...], preferred_element_type=F32)


def _matmul(x, w):
  t, kdim = x.shape
  n = w.shape[1]
  tm = _pick_block(t, 640, LANES)
  tn = 512
  return pl.pallas_call(
      _mm_body,
      grid=(t // tm, n // tn),
      in_specs=[pl.BlockSpec((tm, kdim), lambda i, j: (i, 0)), pl.BlockSpec((kdim, tn), lambda i, j: (0, j))],
      out_specs=pl.BlockSpec((tm, tn), lambda i, j: (i, j)),
      out_shape=jax.ShapeDtypeStruct((t, n), F32),
      compiler_params=_params(("parallel", "arbitrary")),
      name="peer_query",
  )(x, w)


def _top16_rows(s, val_ref, idx_ref, lane0):
  n, w = s.shape
  sub = lax.broadcasted_iota(jnp.int32, s.shape, 0).astype(F32)
  for k in range(PEER_TOPK):
    m = jnp.max(s, axis=0, keepdims=True)
    idx = jnp.min(jnp.where(s == m, sub, float(n)), axis=0, keepdims=True)
    val_ref[k:k + 1, lane0:lane0 + w] = m
    idx_ref[k:k + 1, lane0:lane0 + w] = idx
    s = jnp.where(sub == idx, NEG_INF, s)


def _odd_even_merge_sort_pairs(n):
  pairs = []
  p = 1
  while p < n:
    k = p
    while k >= 1:
      for j in range(k % p, n - k, 2 * k):
        for i in range(min(k, n - j - k)):
          if (i + j) // (2 * p) == (i + j + k) // (2 * p):
            pairs.append((i + j, i + j + k))
      k //= 2
    p *= 2
  return pairs


def _top16_distinct(s, val_ref, idx_ref, lane0):
  n, w = s.shape
  depth = n // SUBLANES
  assert depth == PEER_TOPK
  sub = lax.broadcasted_iota(jnp.int32, (SUBLANES, w), 0).astype(F32)
  col = [s[g * SUBLANES:(g + 1) * SUBLANES, :] for g in range(depth)]
  cid = [sub + float(g * SUBLANES) for g in range(depth)]
  for a, b in _odd_even_merge_sort_pairs(depth):
    up = col[b] > col[a]
    col[a], col[b] = jnp.where(up, col[b], col[a]), jnp.where(up, col[a], col[b])
    cid[a], cid[b] = jnp.where(up, cid[b], cid[a]), jnp.where(up, cid[a], cid[b])
  vals = []
  for t in range(PEER_TOPK):
    m = jnp.max(col[0], axis=0, keepdims=True)
    hit = col[0] == m
    vals.append(m)
    val_ref[t:t + 1, lane0:lane0 + w] = m
    idx_ref[t:t + 1, lane0:lane0 + w] = jnp.max(jnp.where(hit, cid[0], -1.0), axis=0, keepdims=True)
    for k in range(depth - 1 - t):
      col[k] = jnp.where(hit, col[k + 1], col[k])
      cid[k] = jnp.where(hit, cid[k + 1], cid[k])
  tied = jnp.zeros((1, w), F32)
  for t in range(PEER_TOPK - 1):
    tied = jnp.where(vals[t] == vals[t + 1], 1.0, tied)
  at_least_last = jnp.sum(jnp.where(s >= vals[-1], 1.0, 0.0), axis=0, keepdims=True)
  return jnp.where(at_least_last > float(PEER_TOPK), 1.0, tied)


PEER_HEADS_PER_ITER = 2


def _topk_body(q_ref, k1_ref, k2_ref, eid_ref, gate_ref, val_ref, idx_ref, best_ref, sel_ref, eid_s, gate_s):
  tb = q_ref.shape[0]
  kk = PEER_TOPK
  half_w = N_KEYS
  hp = PEER_HEADS_PER_ITER
  wide = hp * tb

  def head_group(hg, carry):
    def scores(hh, c):
      col = pl.multiple_of((2 * (hg * hp + hh) + c) * half_w, half_w)
      qh = q_ref[:, pl.ds(col, half_w)]
      return lax.dot_general((k1_ref, k2_ref)[c][...], qh, NT_DIMS, preferred_element_type=F32,
                             precision=lax.Precision.HIGHEST)

    halves = [(hh, c) for hh in range(hp) for c in range(2)]
    tied = [_top16_distinct(scores(hh, c), val_ref, idx_ref, (2 * hh + c) * tb) for hh, c in halves]

    @pl.when(jnp.max(jnp.concatenate(tied, axis=1)) > 0.0)
    def _():
      for hh, c in halves:
        _top16_rows(scores(hh, c), val_ref, idx_ref, (2 * hh + c) * tb)

    def pick(ref, c):
      return jnp.concatenate([ref[:, (2 * hh + c) * tb:(2 * hh + c + 1) * tb] for hh in range(hp)], axis=1)

    v1, v2 = pick(val_ref, 0), pick(val_ref, 1)
    i1, i2 = pick(idx_ref, 0), pick(idx_ref, 1)
    b16 = lax.broadcasted_iota(jnp.int32, (kk, wide), 0).astype(F32)
    b8 = lax.broadcasted_iota(jnp.int32, (SUBLANES, wide), 0).astype(F32)
    vals = [v1[0:1] + v2]
    flat = [b16]
    code = [i1[0:1] * N_KEYS + i2]
    for a in range(1, SUBLANES):
      vals.append(v1[a:a + 1] + v2[0:SUBLANES])
      flat.append(a * kk + b8)
      code.append(i1[a:a + 1] * N_KEYS + i2[0:SUBLANES])
    vals.append(v1[SUBLANES:kk] + v2[0:1])
    flat.append((b8 + SUBLANES) * kk)
    code.append(i1[SUBLANES:kk] * N_KEYS + i2[0:1])
    cand = jnp.concatenate(vals, axis=0)
    flat = jnp.concatenate(flat, axis=0)
    code = jnp.concatenate(code, axis=0)

    first = b8 == 0.0
    lists, list_code = [], []
    for b in range(kk):
      t = v1[0:SUBLANES] + v2[b:b + 1]
      lists.append(t if b < SUBLANES else jnp.where(first, t, NEG_INF))
      list_code.append(i1[0:SUBLANES] * N_KEYS + i2[b:b + 1])
    single = v1[SUBLANES:kk] + v2[0:1]
    single_code = i1[SUBLANES:kk] * N_KEYS + i2[0:1]
    popped = []
    for k in range(kk):
      m = jnp.max(jnp.maximum(lists[0], single), axis=0, keepdims=True)
      hit_l, hit_s = lists[0] == m, single == m
      popped.append(m)
      best_ref[k:k + 1, :] = m
      sel_ref[k:k + 1, :] = jnp.max(jnp.maximum(jnp.where(hit_l, list_code[0], -1.0),
                                                jnp.where(hit_s, single_code, -1.0)), axis=0, keepdims=True)
      for b in range(kk - 1 - k):
        lists[b] = jnp.where(hit_l, lists[b + 1], lists[b])
        list_code[b] = jnp.where(hit_l, list_code[b + 1], list_code[b])
      single = jnp.where(hit_s, NEG_INF, single)
    tied = jnp.sum(jnp.where(cand >= popped[-1], 1.0, 0.0), axis=0, keepdims=True) > float(kk)
    for k in range(kk - 1):
      tied = jnp.logical_or(tied, popped[k] == popped[k + 1])

    @pl.when(jnp.max(jnp.where(tied, 1.0, 0.0)) > 0.0)
    def _():
      left = cand
      for k in range(kk):
        m = jnp.max(left, axis=0, keepdims=True)
        fsel = jnp.min(jnp.where(left == m, flat, float(kk * kk)), axis=0, keepdims=True)
        hit = flat == fsel
        best_ref[k:k + 1, :] = m
        sel_ref[k:k + 1, :] = jnp.max(jnp.where(hit, code, -1.0), axis=0, keepdims=True)
        left = jnp.where(hit, NEG_INF, left)

    best = best_ref[...]
    e = jnp.exp(best - best[0:1])
    gate = e / jnp.sum(e, axis=0, keepdims=True)
    eid = sel_ref[...].astype(jnp.int32)
    for hh in range(hp):
      row = pl.multiple_of((hg * hp + hh) * kk, kk)
      gate_s[pl.ds(row, kk), :] = gate[:, hh * tb:(hh + 1) * tb]
      eid_s[pl.ds(row, kk), :] = eid[:, hh * tb:(hh + 1) * tb]
    return carry

  lax.fori_loop(0, PEER_HEADS // hp, head_group, 0)
  gate_ref[...] = gate_s[...].T
  eid_ref[...] = eid_s[...].T


def _peer_topk(qp, k1, k2):
  t = qp.shape[0]
  tb = LANES
  return pl.pallas_call(
      _topk_body,
      grid=(t // tb,),
      in_specs=[
          pl.BlockSpec((tb, qp.shape[1]), lambda i: (i, 0)),
          pl.BlockSpec((N_KEYS, N_KEYS), lambda i: (0, 0)),
          pl.BlockSpec((N_KEYS, N_KEYS), lambda i: (0, 0)),
      ],
      out_specs=[
          pl.BlockSpec((tb, PEER_SEL), lambda i: (i, 0)),
          pl.BlockSpec((tb, PEER_SEL), lambda i: (i, 0)),
      ],
      out_shape=[
          jax.ShapeDtypeStruct((t, PEER_SEL), jnp.int32),
          jax.ShapeDtypeStruct((t, PEER_SEL), F32),
      ],
      scratch_shapes=[
          pltpu.VMEM((PEER_TOPK, 2 * PEER_HEADS_PER_ITER * tb), F32),
          pltpu.VMEM((PEER_TOPK, 2 * PEER_HEADS_PER_ITER * tb), F32),
          pltpu.VMEM((PEER_TOPK, PEER_HEADS_PER_ITER * tb), F32),
          pltpu.VMEM((PEER_TOPK, PEER_HEADS_PER_ITER * tb), F32),
          pltpu.VMEM((PEER_SEL, tb), jnp.int32),
          pltpu.VMEM((PEER_SEL, tb), F32),
      ],
      compiler_params=_params(("parallel",)),
      name="peer_topk",
  )(qp, k1, k2)


def _gate_matrix_body(eid_ref, gate_ref, g_ref):
  tb = eid_ref.shape[0]
  sub = lax.broadcasted_iota(jnp.int32, (N_KEYS, PEER_SEL), 0)

  def token_group(gi, carry):
    tiles = []
    for tau in range(SUBLANES):
      t = gi * SUBLANES + tau
      e = eid_ref[pl.ds(t, 1), :]
      gt = gate_ref[pl.ds(t, 1), :]
      at = jnp.where(sub == lax.shift_right_logical(e, KEY_BITS), gt, 0.0).astype(BF16)
      bt = jnp.where(sub == jnp.bitwise_and(e, N_KEYS - 1), 1.0, 0.0).astype(BF16)
      tiles.append(lax.dot_general(at, bt, NT_DIMS, preferred_element_type=F32))
    blocks = jnp.stack([jnp.stack([tile[h * SUBLANES:(h + 1) * SUBLANES, :] for tile in tiles])
                        for h in range(N_KEYS // SUBLANES)])
    g_ref[gi] = jnp.swapaxes(blocks, 1, 2).reshape(N_KEYS, SUBLANES, N_KEYS)
    return carry

  lax.fori_loop(0, tb // SUBLANES, token_group, 0)


def _gate_matrix(eid_t, gate_t):
  t = eid_t.shape[0]
  tb = _pick_block(t, 64, SUBLANES)
  return pl.pallas_call(
      _gate_matrix_body,
      grid=(t // tb,),
      in_specs=[pl.BlockSpec((tb, PEER_SEL), lambda i: (i, 0)), pl.BlockSpec((tb, PEER_SEL), lambda i: (i, 0))],
      out_specs=pl.BlockSpec((tb // SUBLANES, N_KEYS, SUBLANES, N_KEYS), lambda i: (i, 0, 0, 0)),
      out_shape=jax.ShapeDtypeStruct((t // SUBLANES, N_KEYS, SUBLANES, N_KEYS), F32),
      compiler_params=_params(("parallel",)),
      name="peer_gate_matrix",
  )(eid_t, gate_t)


PEER_I1_PER_STEP = 8


def _peer_dense_body(x_ref, u_ref, v_ref, g_ref, h_ref, fg_ref, y_ref, ys_ref, *, rem):
  c = pl.program_id(1)
  s = lax.dot_general(x_ref[...], u_ref[...], NT_DIMS, preferred_element_type=F32)
  act = 0.5 * s * (1.0 + lax.erf(s * (2.0 ** -0.5)))
  tm = x_ref.shape[0]
  coef = jnp.concatenate(
      [g_ref[:, j].reshape(tm, N_KEYS) * act[:, j * N_KEYS:(j + 1) * N_KEYS] for j in range(PEER_I1_PER_STEP)],
      axis=1)
  contrib = jnp.dot(coef.astype(BF16), v_ref[...], preferred_element_type=F32)

  @pl.when(c == 0)
  def _():
    y_ref[...] = contrib

  @pl.when(c > 0)
  def _():
    y_ref[...] += contrib

  @pl.when(c == pl.num_programs(1) - 1)
  def _():
    y = _rms(h_ref[...] + y_ref[...], fg_ref[...])
    y_ref[...] = y

    @pl.when(pl.program_id(0) == pl.num_programs(0) - 1)
    def _():
      ys_ref[...] = y[rem:, :]


def _peer_dense(xn2, u_bf, v_bf, gmat, h, final_g, tp):
  t = xn2.shape[0]
  ts = t - tp
  tm = _pick_block(t, 640, LANES)
  te = PEER_I1_PER_STEP * N_KEYS
  return pl.pallas_call(
      functools.partial(_peer_dense_body, rem=_tail_rows(tp, ts, tm)),
      grid=(t // tm, N_EXPERTS // te),
      in_specs=[
          pl.BlockSpec((tm, D_MODEL), lambda i, c: (i, 0)),
          pl.BlockSpec((te, D_MODEL), lambda i, c: (c, 0)),
          pl.BlockSpec((te, D_MODEL), lambda i, c: (c, 0)),
          pl.BlockSpec((tm // SUBLANES, PEER_I1_PER_STEP, SUBLANES, N_KEYS), lambda i, c: (i, c, 0, 0)),
          pl.BlockSpec((tm, D_MODEL), lambda i, c: (i, 0), pipeline_mode=pl.Buffered(1)),
          pl.BlockSpec((1, D_MODEL), lambda i, c: (0, 0)),
      ],
      out_specs=[_prompt_rows_spec(tm, D_MODEL, tp), _sample_rows_spec(ts, D_MODEL)],
      out_shape=[jax.ShapeDtypeStruct((tp, D_MODEL), F32), jax.ShapeDtypeStruct((ts, D_MODEL), F32)],
      compiler_params=_params(("arbitrary", "arbitrary")),
      name="peer_dense",
  )(xn2, u_bf, v_bf, gmat, h, final_g)


def _rope_tables(pos):
  half = ROT_DIM // 2
  inv_freq = jnp.float32(ROPE_THETA) ** (-jnp.arange(half, dtype=F32) * 2.0 / ROT_DIM)
  ang = pos.astype(F32)[:, None] * inv_freq[None, :]
  cos, sin = jnp.cos(ang), jnp.sin(ang)
  n = pos.shape[0]
  ones = jnp.ones((n, HEAD_DIM - ROT_DIM), F32)
  zeros = jnp.zeros((n, HEAD_DIM - ROT_DIM), F32)
  zh = jnp.zeros((n, half), F32)
  cos_c = jnp.concatenate([cos, cos, ones], axis=1)
  sa_c = jnp.concatenate([-sin, zh, zeros], axis=1)
  sb_c = jnp.concatenate([zh, sin, zeros], axis=1)
  two = lambda a: jnp.concatenate([a, a], axis=1)
  return two(cos_c), two(sa_c), two(sb_c)


def kernel(x_prompt, x_sample, cache_k, cache_v, state_conv, state_h, page_table, norm1_g, w_in, lambda_q1, lambda_k1, lambda_q2, lambda_k2, subln_g, conv_w, conv_b, lru_wa, lru_ba, lru_wx, lru_bx, lru_lambda, w_att_up, w_lru_up, w_out, norm2_g, peer_wq, peer_k1, peer_k2, peer_u, peer_v, final_g):
  batch, seq, _ = x_prompt.shape
  dec_batch, dec_seq, _ = x_sample.shape
  n_pages = page_table.shape[1]
  past_len = n_pages * PAGE_SIZE
  tp = batch * seq
  ts = dec_batch * dec_seq
  assert w_in.shape[0] == 1, "one layer"

  x_p = x_prompt.reshape(tp, D_MODEL)
  x_s = x_sample.reshape(ts, D_MODEL)
  tabs_p = _rope_tables(jnp.arange(seq, dtype=jnp.int32))
  tabs_s = _rope_tables(past_len + jnp.arange(dec_seq, dtype=jnp.int32))
  cos_t, sa_t, sb_t = [jnp.concatenate([jnp.tile(a, (batch, 1)), jnp.tile(b, (dec_batch, 1))], axis=0)
                       for a, b in zip(tabs_p, tabs_s)]
  lam = (jnp.exp(jnp.sum(lambda_q1[0].astype(F32) * lambda_k1[0].astype(F32)))
         - jnp.exp(jnp.sum(lambda_q2[0].astype(F32) * lambda_k2[0].astype(F32))) + LAM_INIT).reshape(1)
  row = lambda a: a.reshape(1, -1)

  proj = _inproj(x_p, x_s, row(norm1_g[0]), w_in[0].astype(BF16), cos_t, sa_t, sb_t)

  att_p, u_bf, v_bf = _prompt_attention(proj, lam, row(subln_g[0]), batch, seq, peer_u[0], peer_v[0])
  proj_s = proj[tp:]
  q_s = proj_s[:, :OFF_Q].reshape(dec_batch, dec_seq, N_KV_HEADS, 2, 2, HEAD_DIM)
  q_s = q_s.transpose(0, 2, 4, 3, 1, 5).reshape(dec_batch, N_KV_HEADS, 2, 2 * dec_seq, HEAD_DIM)
  wq = jnp.einsum("bgcnd,ce->bgcned", q_s, jnp.eye(2, dtype=F32))
  wq = wq.reshape(dec_batch, N_KV_HEADS * 2 * 2 * dec_seq, HEAD_W).astype(BF16)
  new_rows = lambda a: a.reshape(dec_batch, dec_seq * N_KV_HEADS, HEAD_W).astype(BF16)
  k_new, v_new = proj_s[:, OFF_Q:OFF_K], proj_s[:, OFF_K:OFF_V]
  n_pool = cache_k.shape[1]
  att_s = _sample_attention(page_table, lam, wq, new_rows(k_new), new_rows(v_new), row(subln_g[0]),
                            cache_k.reshape(n_pool, PAGE_ROWS, HEAD_W),
                            cache_v.reshape(n_pool, PAGE_ROWS, HEAD_W), dec_seq)
  att_s = att_s.reshape(dec_batch, N_KV_HEADS, 2, dec_seq, HEAD_W).transpose(0, 3, 1, 2, 4)
  att_s = att_s.reshape(ts, ATT_WIDTH).astype(BF16)

  lru_w = (conv_w[0], row(conv_b[0]), lru_wa[0], row(lru_ba[0]), lru_wx[0], row(lru_bx[0]), row(lru_lambda[0]))
  lru_p, h_p = _lru_prompt(proj, lru_w, batch, seq)
  xl_s = proj_s[:, OFF_V:OFF_L].reshape(dec_batch, dec_seq, LRU_WIDTH)
  lru_s, h_s = _lru_sample(xl_s.transpose(1, 0, 2), state_conv[0].transpose(1, 0, 2), state_h[0], lru_w)
  lru_s = lru_s.transpose(1, 0, 2).reshape(ts, LRU_WIDTH)

  h_all, xn2 = _merge(att_p, att_s, lru_p, lru_s, proj, x_p, x_s, w_att_up[0].astype(BF16),
                      w_lru_up[0].astype(BF16), w_out[0].astype(BF16), row(norm2_g[0]))
  qp = _matmul(xn2, peer_wq[0].astype(BF16))
  eid_t, gate_t = _peer_topk(qp, peer_k1[0], peer_k2[0])
  gmat = _gate_matrix(eid_t, gate_t)
  y_p, y_s = _peer_dense(xn2, u_bf, v_bf, gmat, h_all, row(final_g), tp)

  kv_shape_p = (1, batch, seq, N_KV_HEADS, HEAD_W)
  kv_shape_s = (1, dec_batch, dec_seq, N_KV_HEADS, HEAD_W)
  tail = CONV_W - 1
  conv_p = jnp.stack([proj[(b + 1) * seq - tail:(b + 1) * seq, OFF_V:OFF_L] for b in range(batch)])
  conv_s = jnp.concatenate([state_conv[0].astype(F32), xl_s], axis=1)[:, -tail:]
  return (
      y_p.reshape(batch, seq, D_MODEL),
      y_s.reshape(dec_batch, dec_seq, D_MODEL),
      proj[:tp, OFF_Q:OFF_K].reshape(kv_shape_p),
      proj[:tp, OFF_K:OFF_V].reshape(kv_shape_p),
      conv_p[None],
      h_p.reshape(1, batch, LRU_WIDTH),
      k_new.reshape(kv_shape_s),
      v_new.reshape(kv_shape_s),
      conv_s[None],
      h_s[None],
  )
```

```python
import functools
import math

import jax
import jax.numpy as jnp
from jax import lax
from jax.experimental import pallas as pl
from jax.experimental.pallas import tpu as pltpu

F32 = jnp.float32
BF16 = jnp.bfloat16

D_MODEL = 2048
N_HEADS = 8
N_KV_HEADS = 4
HEAD_DIM = 64
HEAD_W = 2 * HEAD_DIM
ROT_DIM = HEAD_DIM // 4
ROPE_THETA = 500000.0
ATT_WIDTH = N_HEADS * HEAD_W
KV_WIDTH = N_KV_HEADS * HEAD_W
LRU_WIDTH = 1024
LRU_BLOCKS = 8
LRU_BLOCK_W = LRU_WIDTH // LRU_BLOCKS
CONV_W = 4
LRU_C = 8.0
N_KEYS = 128
KEY_BITS = 7
N_EXPERTS = N_KEYS * N_KEYS
PEER_HEADS = 8
PEER_TOPK = 16
PEER_SEL = PEER_HEADS * PEER_TOPK
PAGE_SIZE = 128
RMS_EPS = 1e-6
OFF_Q = ATT_WIDTH
OFF_K = OFF_Q + KV_WIDTH
OFF_V = OFF_K + KV_WIDTH
OFF_L = OFF_V + LRU_WIDTH
OFF_GA = OFF_L + D_MODEL
IN_WIDTH = OFF_GA + D_MODEL
LAM_INIT = 0.8 - 0.6 * math.exp(0.0)

LANES = 128
SUBLANES = 8
VMEM_LIMIT_BYTES = 56 * 1024 * 1024

NEG_INF = float("-inf")
LOG2_E = math.log2(math.e)
NT_DIMS = (((1,), (1,)), ((), ()))


def _pick_block(total, cap, quantum):
  best = None
  b = quantum
  while b <= min(cap, total):
    if total % b == 0:
      best = b
    b += quantum
  assert best is not None, (total, cap, quantum)
  return best


def _params(sem, vmem=VMEM_LIMIT_BYTES):
  return pltpu.CompilerParams(dimension_semantics=sem, vmem_limit_bytes=vmem)


def _rms(x, g):
  var = jnp.mean(x * x, axis=-1, keepdims=True)
  return x * lax.rsqrt(var + RMS_EPS) * g


def _tail_rows(tp, ts, tm):
  assert (tp + ts) % tm == 0 and 0 < ts < tm, (tp, ts, tm)
  return tm - ts


def _prompt_rows_spec(tm, width, tp):
  last_prompt_block = pl.cdiv(tp, tm) - 1
  return pl.BlockSpec((tm, width), lambda i, *_: (jnp.minimum(i, last_prompt_block), 0))


def _sample_rows_spec(ts, width):
  return pl.BlockSpec((ts, width), lambda i, *_: (0, 0))


def _stacked_rows(i, n_blocks, rem, p_ref, s_ref, emit):
  tm = rem + s_ref.shape[0]

  @pl.when(i < n_blocks - 1)
  def _():
    emit(slice(0, tm), p_ref[...])

  @pl.when(i == n_blocks - 1)
  def _():
    emit(slice(0, rem), p_ref[0:rem, :])
    emit(slice(rem, tm), s_ref[...])


def _inproj_body(xp_ref, xs_ref, g_ref, w_ref, cos_ref, sa_ref, sb_ref, o_ref, xn_ref, *, tn, n_rope_blocks, rem):
  i = pl.program_id(0)
  j = pl.program_id(1)

  @pl.when(j == 0)
  def _():
    def emit(rows, x):
      xn_ref[rows, :] = _rms(x, g_ref[...]).astype(BF16)
    _stacked_rows(i, pl.num_programs(0), rem, xp_ref, xs_ref, emit)

  acc = jnp.dot(xn_ref[...], w_ref[...], preferred_element_type=F32)

  @pl.when(j < n_rope_blocks)
  def _():
    reps = tn // LANES
    cos = jnp.concatenate([cos_ref[...]] * reps, axis=1)
    sa = jnp.concatenate([sa_ref[...]] * reps, axis=1)
    sb = jnp.concatenate([sb_ref[...]] * reps, axis=1)
    half = ROT_DIM // 2
    rot = acc * cos + pltpu.roll(acc, tn - half, 1) * sa + pltpu.roll(acc, half, 1) * sb
    col = j * tn + lax.broadcasted_iota(jnp.int32, acc.shape, 1)
    o_ref[...] = jnp.where(col < OFF_Q, rot * (HEAD_DIM ** -0.5), jnp.where(col < OFF_K, rot, acc))

  @pl.when(j >= n_rope_blocks)
  def _():
    o_ref[...] = acc


def _inproj(x_p, x_s, norm_g, w_in_bf, cos_t, sa_t, sb_t):
  tp, ts = x_p.shape[0], x_s.shape[0]
  t = tp + ts
  tm = _pick_block(t, 832, 64)
  tn = 1024
  body = functools.partial(_inproj_body, tn=tn, n_rope_blocks=pl.cdiv(OFF_K, tn), rem=_tail_rows(tp, ts, tm))
  return pl.pallas_call(
      body,
      grid=(t // tm, IN_WIDTH // tn),
      in_specs=[
          _prompt_rows_spec(tm, D_MODEL, tp),
          _sample_rows_spec(ts, D_MODEL),
          pl.BlockSpec((1, D_MODEL), lambda i, j: (0, 0)),
          pl.BlockSpec((D_MODEL, tn), lambda i, j: (0, j)),
          pl.BlockSpec((tm, LANES), lambda i, j: (i, 0)),
          pl.BlockSpec((tm, LANES), lambda i, j: (i, 0)),
          pl.BlockSpec((tm, LANES), lambda i, j: (i, 0)),
      ],
      out_specs=pl.BlockSpec((tm, tn), lambda i, j: (i, j)),
      out_shape=jax.ShapeDtypeStruct((t, IN_WIDTH), F32),
      scratch_shapes=[pltpu.VMEM((tm, D_MODEL), BF16)],
      compiler_params=_params(("parallel", "arbitrary")),
      name="inproj",
  )(x_p, x_s, norm_g, w_in_bf, cos_t, sa_t, sb_t)


def _subln(o, g):
  var = jnp.mean(o * o, axis=-1, keepdims=True)
  return o * lax.rsqrt(var + RMS_EPS) * g * (1.0 - LAM_INIT)


def _pattn_body(qi_ref, ki_ref, lam_ref, q_ref, k_ref, v_ref, g_ref, tu_ref, tv_ref, o_ref, tub_ref, tvb_ref,
                q4_ref, m_ref, l_ref, acc_ref, *, tq, tk, cast_steps):
  qi = qi_ref[pl.program_id(2)]
  ki = ki_ref[pl.program_id(2)]
  cols = 2 * 2 * tq

  @pl.when(pl.program_id(2) < cast_steps)
  def _():
    tub_ref[...] = tu_ref[...].astype(BF16)
    tvb_ref[...] = tv_ref[...].astype(BF16)

  @pl.when(ki == 0)
  def _():
    m_ref[...] = jnp.full((1, cols), NEG_INF, F32)
    l_ref[...] = jnp.zeros((1, cols), F32)
    acc_ref[...] = jnp.zeros((HEAD_W, cols), F32)
    lane = lax.broadcasted_iota(jnp.int32, (tq, HEAD_W), 1)
    for r in range(2):
      qh = q_ref[:, r * HEAD_W:(r + 1) * HEAD_W] * LOG2_E
      q4_ref[(2 * r) * tq:(2 * r + 1) * tq, :] = jnp.where(lane < HEAD_DIM, qh, 0.0).astype(BF16)
      q4_ref[(2 * r + 1) * tq:(2 * r + 2) * tq, :] = jnp.where(lane >= HEAD_DIM, qh, 0.0).astype(BF16)

  def update(diagonal):
    k = k_ref[...].astype(BF16)
    s = lax.dot_general(k, q4_ref[...], NT_DIMS, preferred_element_type=F32)
    if diagonal:
      visible = (lax.broadcasted_iota(jnp.int32, (tk, tq), 0) <= lax.broadcasted_iota(jnp.int32, (tk, tq), 1))
      s = jnp.concatenate([jnp.where(visible, s[:, j * tq:(j + 1) * tq], NEG_INF) for j in range(4)], axis=1)
    m_old = m_ref[...]
    m_new = jnp.maximum(m_old, jnp.max(s, axis=0, keepdims=True))
    alpha = jnp.exp2(m_old - m_new)
    p = jnp.exp2(s - m_new)
    l_ref[...] = alpha * l_ref[...] + jnp.sum(p, axis=0, keepdims=True)
    vt = v_ref[...].T.astype(BF16)
    acc_ref[...] = alpha * acc_ref[...] + jnp.dot(vt, p.astype(BF16), preferred_element_type=F32)
    m_ref[...] = m_new

  @pl.when(ki < qi)
  def _():
    update(False)

  @pl.when(ki == qi)
  def _():
    update(True)
    lam = lam_ref[0]
    o = acc_ref[...] * (1.0 / l_ref[...])
    for r in range(2):
      d = o[:, (2 * r) * tq:(2 * r + 1) * tq] - lam * o[:, (2 * r + 1) * tq:(2 * r + 2) * tq]
      var = jnp.mean(d * d, axis=0, keepdims=True)
      dn = d * (lax.rsqrt(var + RMS_EPS) * (1.0 - LAM_INIT))
      o_ref[:, r * HEAD_W:(r + 1) * HEAD_W] = (dn.T * g_ref[...]).astype(o_ref.dtype)


def _prompt_attention(proj, lam, subln_g, batch, seq, table_u, table_v):
  tq = tk = _pick_block(seq, 512, LANES)
  nq = seq // tq
  gw = 2 * HEAD_W
  pairs = [(qi, ki) for qi in range(nq) for ki in range(qi + 1)]
  qi_of = jnp.asarray([p[0] for p in pairs], jnp.int32)
  ki_of = jnp.asarray([p[1] for p in pairs], jnp.int32)
  n_groups = batch * N_KV_HEADS
  cast_steps = 1 << (len(pairs).bit_length() - 1)
  cast_rows = N_EXPERTS // (n_groups * cast_steps)
  assert cast_rows * n_groups * cast_steps == N_EXPERTS and cast_rows % (2 * SUBLANES) == 0
  table_spec = pl.BlockSpec(
      (cast_rows, D_MODEL),
      lambda b, g, p, qo, ko: ((b * N_KV_HEADS + g) * cast_steps + jnp.minimum(p, cast_steps - 1), 0))
  body = functools.partial(_pattn_body, tq=tq, tk=tk, cast_steps=cast_steps)
  grid_spec = pltpu.PrefetchScalarGridSpec(
      num_scalar_prefetch=2,
      grid=(batch, N_KV_HEADS, len(pairs)),
      in_specs=[
          pl.BlockSpec(memory_space=pltpu.SMEM),
          pl.BlockSpec((tq, gw), lambda b, g, p, qo, ko: (b * nq + qo[p], g)),
          pl.BlockSpec((tk, HEAD_W), lambda b, g, p, qo, ko: (b * nq + ko[p], OFF_Q // HEAD_W + g)),
          pl.BlockSpec((tk, HEAD_W), lambda b, g, p, qo, ko: (b * nq + ko[p], OFF_K // HEAD_W + g)),
          pl.BlockSpec((1, HEAD_W), lambda b, g, p, qo, ko: (0, 0)),
          table_spec, table_spec,
      ],
      out_specs=[pl.BlockSpec((tq, gw), lambda b, g, p, qo, ko: (b * nq + qo[p], g)), table_spec, table_spec],
      scratch_shapes=[
          pltpu.VMEM((4 * tq, HEAD_W), BF16),
          pltpu.VMEM((1, 4 * tq), F32),
          pltpu.VMEM((1, 4 * tq), F32),
          pltpu.VMEM((HEAD_W, 4 * tq), F32),
      ],
  )
  return pl.pallas_call(
      body,
      grid_spec=grid_spec,
      out_shape=[jax.ShapeDtypeStruct((batch * seq, ATT_WIDTH), BF16),
                 jax.ShapeDtypeStruct(table_u.shape, BF16), jax.ShapeDtypeStruct(table_v.shape, BF16)],
      compiler_params=_params(("parallel", "parallel", "arbitrary")),
      name="prompt_attention",
  )(qi_of, ki_of, lam, proj, proj, proj, subln_g, table_u, table_v)


PAGES_PER_CHUNK = 8
SATTN_SLOTS = 3
PAGE_ROWS = PAGE_SIZE * N_KV_HEADS


def _sattn_body(pt_ref, lam_ref, wq_ref, kn_ref, vn_ref, g_ref, ck_ref, cv_ref, o_ref,
                kbuf, vbuf, bias_ref, sem, *, n_chunks, dec_seq):
  b = pl.program_id(0)
  nb = pl.num_programs(0)
  rows = wq_ref.shape[1]
  gr = rows // N_KV_HEADS
  cols = PAGES_PER_CHUNK * PAGE_ROWS

  def copies(bb, c, slot):
    out = []
    for p in range(PAGES_PER_CHUNK):
      page = pt_ref[bb, c * PAGES_PER_CHUNK + p]
      out.append(pltpu.make_async_copy(ck_ref.at[page], kbuf.at[slot, p], sem.at[0, slot]))
      out.append(pltpu.make_async_copy(cv_ref.at[page], vbuf.at[slot, p], sem.at[1, slot]))
    return out

  def start(bb, c, slot):
    for n, cp in enumerate(copies(bb, c, slot)):
      cp.start(priority=n % 2)

  def same_head(shape):
    head_of_col = jnp.bitwise_and(lax.broadcasted_iota(jnp.int32, shape, 1), N_KV_HEADS - 1)
    head_of_row = lax.broadcasted_iota(jnp.int32, shape, 0) // gr
    return head_of_col == head_of_row

  ahead = SATTN_SLOTS - 1

  def slot_of(bb, c):
    return lax.rem(bb * n_chunks + c, SATTN_SLOTS)

  @pl.when(b == 0)
  def _():
    for c in range(ahead):
      start(0, c, c)
    bias_ref[...] = jnp.where(same_head((rows, cols)), 0.0, NEG_INF)

  wq = wq_ref[0]

  def softmax_step(carry, s, v):
    m_old, l_old, acc = carry
    m_new = jnp.maximum(m_old, jnp.max(s, axis=1, keepdims=True))
    alpha = jnp.exp(m_old - m_new)
    p = jnp.exp(s - m_new)
    l_new = alpha * l_old + jnp.sum(p, axis=1, keepdims=True)
    acc = alpha * acc + jnp.dot(p.astype(BF16), v, preferred_element_type=F32)
    return m_new, l_new, acc

  def chunk(c, carry):
    slot = slot_of(b, c)

    @pl.when(c + ahead < n_chunks)
    def _():
      start(b, c + ahead, slot_of(b, c + ahead))

    @pl.when(jnp.logical_and(c + ahead >= n_chunks, b + 1 < nb))
    def _():
      start(b + 1, c + ahead - n_chunks, slot_of(b + 1, c + ahead - n_chunks))

    for cp in copies(b, c, slot):
      cp.wait()
    kc = kbuf[slot].reshape(cols, HEAD_W).astype(BF16)
    vc = vbuf[slot].reshape(cols, HEAD_W).astype(BF16)
    s = lax.dot_general(wq, kc, NT_DIMS, preferred_element_type=F32) + bias_ref[...]
    return softmax_step(carry, s, vc)

  init = (jnp.full((rows, 1), NEG_INF, F32), jnp.zeros((rows, 1), F32), jnp.zeros((rows, HEAD_W), F32))
  carry = lax.fori_loop(0, n_chunks, chunk, init)

  new_rows = kn_ref.shape[1]
  s = lax.dot_general(wq, kn_ref[0], NT_DIMS, preferred_element_type=F32)
  t_of_row = lax.broadcasted_iota(jnp.int32, (rows, new_rows), 0) % dec_seq
  t_of_col = lax.broadcasted_iota(jnp.int32, (rows, new_rows), 1) // N_KV_HEADS
  visible = jnp.logical_and(same_head((rows, new_rows)), t_of_col <= t_of_row)
  _, l_fin, acc = softmax_step(carry, jnp.where(visible, s, NEG_INF), vn_ref[0])

  o = acc / l_fin
  lam = lam_ref[0]
  for g in range(N_KV_HEADS):
    blk = o[g * gr:(g + 1) * gr]
    d = blk[:gr // 2] - lam * blk[gr // 2:]
    o_ref[0, g * (gr // 2):(g + 1) * (gr // 2), :] = _subln(d, g_ref[...])


def _sample_attention(page_table, lam, wq, k_new, v_new, subln_g, cache_k, cache_v, dec_seq):
  dec_batch, n_pages = page_table.shape
  assert n_pages % PAGES_PER_CHUNK == 0
  n_chunks = n_pages // PAGES_PER_CHUNK
  assert n_chunks >= SATTN_SLOTS - 1
  rows = wq.shape[1]
  new_rows = k_new.shape[1]
  body = functools.partial(_sattn_body, n_chunks=n_chunks, dec_seq=dec_seq)
  grid_spec = pltpu.PrefetchScalarGridSpec(
      num_scalar_prefetch=1,
      grid=(dec_batch,),
      in_specs=[
          pl.BlockSpec(memory_space=pltpu.SMEM),
          pl.BlockSpec((1, rows, HEAD_W), lambda b, pt: (b, 0, 0)),
          pl.BlockSpec((1, new_rows, HEAD_W), lambda b, pt: (b, 0, 0)),
          pl.BlockSpec((1, new_rows, HEAD_W), lambda b, pt: (b, 0, 0)),
          pl.BlockSpec((1, HEAD_W), lambda b, pt: (0, 0)),
          pl.BlockSpec(memory_space=pl.ANY),
          pl.BlockSpec(memory_space=pl.ANY),
      ],
      out_specs=pl.BlockSpec((1, rows // 2, HEAD_W), lambda b, pt: (b, 0, 0)),
      scratch_shapes=[
          pltpu.VMEM((SATTN_SLOTS, PAGES_PER_CHUNK, PAGE_ROWS, HEAD_W), F32),
          pltpu.VMEM((SATTN_SLOTS, PAGES_PER_CHUNK, PAGE_ROWS, HEAD_W), F32),
          pltpu.VMEM((rows, PAGES_PER_CHUNK * PAGE_ROWS), F32),
          pltpu.SemaphoreType.DMA((2, SATTN_SLOTS)),
      ],
  )
  return pl.pallas_call(
      body,
      grid_spec=grid_spec,
      out_shape=jax.ShapeDtypeStruct((dec_batch, rows // 2, HEAD_W), F32),
      compiler_params=_params(("arbitrary",)),
      name="sample_attention",
  )(page_table, lam, wq, k_new, v_new, subln_g, cache_k, cache_v)


def _lru_gates(xc, wa_ref, ba, wx_ref, bx, lam):
  ra, ix = [], []
  for n in range(LRU_BLOCKS):
    xb = xc[:, n * LRU_BLOCK_W:(n + 1) * LRU_BLOCK_W].astype(BF16)
    ra.append(jnp.dot(xb, wa_ref[n].astype(BF16), preferred_element_type=F32))
    ix.append(jnp.dot(xb, wx_ref[n].astype(BF16), preferred_element_type=F32))
  r = jax.nn.sigmoid(jnp.concatenate(ra, axis=1) + ba)
  i = jax.nn.sigmoid(jnp.concatenate(ix, axis=1) + bx)
  neg = -lam
  softplus = jnp.maximum(neg, 0.0) + jnp.log1p(jnp.exp(-jnp.abs(neg)))
  log_a = -LRU_C * r * softplus
  a = jnp.exp(log_a)
  u = jnp.sqrt(1.0 - a * a) * (i * xc)
  return a, u


def _conv(rows_of, cw_ref, cb):
  out = rows_of(0) * cw_ref[0:1, :]
  for j in range(1, CONV_W):
    out = out + rows_of(j) * cw_ref[j:j + 1, :]
  return out + cb


def _lru_prompt_body(xl_ref, cw_ref, cb_ref, wa_ref, ba_ref, wx_ref, bx_ref, lam_ref, y_ref, hl_ref,
                     ext_ref, a_ref, u_ref, hs_ref, h_ref, *, tt):
  ti = pl.program_id(1)
  head = SUBLANES

  @pl.when(ti == 0)
  def _():
    ext_ref[0:head, :] = jnp.zeros((head, LRU_WIDTH), F32)
    h_ref[...] = jnp.zeros((1, LRU_WIDTH), F32)

  ext_ref[head:head + tt, :] = xl_ref[...]
  xc = _conv(lambda j: ext_ref[head - (CONV_W - 1) + j:head - (CONV_W - 1) + j + tt, :], cw_ref, cb_ref[...])
  a, u = _lru_gates(xc, wa_ref, ba_ref[...], wx_ref, bx_ref[...], lam_ref[...])
  a_ref[...] = a
  u_ref[...] = u

  row = lax.broadcasted_iota(jnp.int32, (SUBLANES, LRU_WIDTH), 0)

  def tile_steps(i, h):
    rows = pl.ds(pl.multiple_of(i * SUBLANES, SUBLANES), SUBLANES)
    a_cum, u_cum = a_ref[rows, :], u_ref[rows, :]
    d = 1
    while d < SUBLANES:
      keep = row >= d
      u_cum = jnp.where(keep, a_cum * pltpu.roll(u_cum, d, 0) + u_cum, u_cum)
      a_cum = jnp.where(keep, a_cum * pltpu.roll(a_cum, d, 0), a_cum)
      d *= 2
    hs = a_cum * h + u_cum
    hs_ref[rows, :] = hs
    return hs[SUBLANES - 1:SUBLANES, :]

  h_fin = lax.fori_loop(0, tt // SUBLANES, tile_steps, h_ref[...], unroll=4)
  h_ref[...] = h_fin
  y_ref[...] = hs_ref[...].astype(y_ref.dtype)
  ext_ref[0:head, :] = ext_ref[tt:tt + head, :]

  @pl.when(ti == pl.num_programs(1) - 1)
  def _():
    hl_ref[0] = h_fin


def _lru_weight_specs(nidx):
  zero2 = (lambda *a: (0, 0))
  zero3 = (lambda *a: (0, 0, 0))
  del nidx
  return [
      pl.BlockSpec((CONV_W, LRU_WIDTH), zero2),
      pl.BlockSpec((1, LRU_WIDTH), zero2),
      pl.BlockSpec((LRU_BLOCKS, LRU_BLOCK_W, LRU_BLOCK_W), zero3),
      pl.BlockSpec((1, LRU_WIDTH), zero2),
      pl.BlockSpec((LRU_BLOCKS, LRU_BLOCK_W, LRU_BLOCK_W), zero3),
      pl.BlockSpec((1, LRU_WIDTH), zero2),
      pl.BlockSpec((1, LRU_WIDTH), zero2),
  ]


def _lru_prompt(proj, lru_w, batch, seq):
  tt = _pick_block(seq, 512, LANES)
  nt = seq // tt
  body = functools.partial(_lru_prompt_body, tt=tt)
  return pl.pallas_call(
      body,
      grid=(batch, nt),
      in_specs=[pl.BlockSpec((tt, LRU_WIDTH), lambda b, ti: (b * nt + ti, OFF_V // LRU_WIDTH))]
      + _lru_weight_specs(2),
      out_specs=[
          pl.BlockSpec((tt, LRU_WIDTH), lambda b, ti: (b * nt + ti, 0)),
          pl.BlockSpec((1, 1, LRU_WIDTH), lambda b, ti: (b, 0, 0)),
      ],
      out_shape=[
          jax.ShapeDtypeStruct((batch * seq, LRU_WIDTH), BF16),
          jax.ShapeDtypeStruct((batch, 1, LRU_WIDTH), F32),
      ],
      scratch_shapes=[
          pltpu.VMEM((tt + 2 * SUBLANES, LRU_WIDTH), F32),
          pltpu.VMEM((tt, LRU_WIDTH), F32),
          pltpu.VMEM((tt, LRU_WIDTH), F32),
          pltpu.VMEM((tt, LRU_WIDTH), F32),
          pltpu.VMEM((1, LRU_WIDTH), F32),
      ],
      compiler_params=_params(("parallel", "arbitrary")),
      name="lru_prompt",
  )(proj, *lru_w)


def _lru_sample_body(xl_ref, cbuf_ref, h0_ref, cw_ref, cb_ref, wa_ref, ba_ref, wx_ref, bx_ref, lam_ref,
                     y_ref, hl_ref, *, dec_seq):
  ext = [cbuf_ref[j] for j in range(CONV_W - 1)] + [xl_ref[t] for t in range(dec_seq)]
  h = h0_ref[...]
  for t in range(dec_seq):
    xc = _conv(lambda j: ext[t + j], cw_ref, cb_ref[...])
    a, u = _lru_gates(xc, wa_ref, ba_ref[...], wx_ref, bx_ref[...], lam_ref[...])
    h = a * h + u
    y_ref[t] = h.astype(y_ref.dtype)
  hl_ref[...] = h


def _lru_sample(xl_t, cbuf_t, h0, lru_w):
  dec_seq, dec_batch, _ = xl_t.shape
  body = functools.partial(_lru_sample_body, dec_seq=dec_seq)
  return pl.pallas_call(
      body,
      out_shape=[
          jax.ShapeDtypeStruct((dec_seq, dec_batch, LRU_WIDTH), BF16),
          jax.ShapeDtypeStruct((dec_batch, LRU_WIDTH), F32),
      ],
      name="lru_sample",
  )(xl_t, cbuf_t, h0, *lru_w)


def _merge_body(attp_ref, atts_ref, lrup_ref, lrus_ref, ga0_ref, ga1_ref, gl0_ref, gl1_ref, xp_ref, xs_ref,
                wa_ref, wl_ref, wo_ref, g2_ref, h_ref, xn_ref, att_buf, lru_buf, x_buf, *, rem):
  i = pl.program_id(0)
  n = pl.num_programs(0)

  def fill(buf):
    def emit(rows, v):
      buf[rows, :] = v
    return emit

  _stacked_rows(i, n, rem, attp_ref, atts_ref, fill(att_buf))
  _stacked_rows(i, n, rem, lrup_ref, lrus_ref, fill(lru_buf))
  _stacked_rows(i, n, rem, xp_ref, xs_ref, fill(x_buf))
  a1 = jnp.dot(att_buf[...], wa_ref[...], preferred_element_type=F32)
  a2 = jnp.dot(lru_buf[...], wl_ref[...], preferred_element_type=F32)
  ga = jnp.concatenate([ga0_ref[...], ga1_ref[...]], axis=1)
  gl = jnp.concatenate([gl0_ref[...], gl1_ref[...]], axis=1)
  m = jax.nn.sigmoid(ga) * a1 + jax.nn.sigmoid(gl) * a2
  h = x_buf[...] + jnp.dot(m.astype(BF16), wo_ref[...], preferred_element_type=F32)
  h_ref[...] = h
  xn_ref[...] = _rms(h, g2_ref[...]).astype(BF16)


def _merge(att_p, att_s, lru_p, lru_s, proj, x_p, x_s, wa, wl, wo, g2):
  tp, ts = x_p.shape[0], x_s.shape[0]
  t = tp + ts
  tm = _pick_block(t, 320, 64)
  half = D_MODEL // 2
  const = lambda shape: pl.BlockSpec(shape, lambda i: (0, 0), pipeline_mode=pl.Buffered(1))
  gate = lambda blk: pl.BlockSpec((tm, half), lambda i: (i, blk))
  return pl.pallas_call(
      functools.partial(_merge_body, rem=_tail_rows(tp, ts, tm)),
      grid=(t // tm,),
      in_specs=[
          _prompt_rows_spec(tm, ATT_WIDTH, tp), _sample_rows_spec(ts, ATT_WIDTH),
          _prompt_rows_spec(tm, LRU_WIDTH, tp), _sample_rows_spec(ts, LRU_WIDTH),
          gate(OFF_L // half), gate(OFF_L // half + 1), gate(OFF_GA // half), gate(OFF_GA // half + 1),
          _prompt_rows_spec(tm, D_MODEL, tp), _sample_rows_spec(ts, D_MODEL),
          const((ATT_WIDTH, D_MODEL)), const((LRU_WIDTH, D_MODEL)), const((D_MODEL, D_MODEL)),
          const((1, D_MODEL)),
      ],
      scratch_shapes=[
          pltpu.VMEM((tm, ATT_WIDTH), BF16),
          pltpu.VMEM((tm, LRU_WIDTH), BF16),
          pltpu.VMEM((tm, D_MODEL), F32),
      ],
      out_specs=[
          pl.BlockSpec((tm, D_MODEL), lambda i: (i, 0)),
          pl.BlockSpec((tm, D_MODEL), lambda i: (i, 0)),
      ],
      out_shape=[
          jax.ShapeDtypeStruct((t, D_MODEL), F32),
          jax.ShapeDtypeStruct((t, D_MODEL), BF16),
      ],
      compiler_params=_params(("parallel",)),
      name="merge",
  )(att_p, att_s, lru_p, lru_s, proj, proj, proj, proj, x_p, x_s, wa, wl, wo, g2)


def _mm_body(x_ref, w_ref, o_ref):
  o_ref[...] = jnp.dot(x_ref[...], w_ref[...], preferred_element_type=F32)


def _matmul(x, w):
  t, kdim = x.shape
  n = w.shape[1]
  tm = _pick_block(t, 640, LANES)
  tn = 512
  return pl.pallas_call(
      _mm_body,
      grid=(t // tm, n // tn),
      in_specs=[pl.BlockSpec((tm, kdim), lambda i, j: (i, 0)), pl.BlockSpec((kdim, tn), lambda i, j: (0, j))],
      out_specs=pl.BlockSpec((tm, tn), lambda i, j: (i, j)),
      out_shape=jax.ShapeDtypeStruct((t, n), F32),
      compiler_params=_params(("parallel", "arbitrary")),
      name="peer_query",
  )(x, w)


def _top16_rows(s, val_ref, idx_ref, lane0):
  n, w = s.shape
  sub = lax.broadcasted_iota(jnp.int32, s.shape, 0).astype(F32)
  for k in range(PEER_TOPK):
    m = jnp.max(s, axis=0, keepdims=True)
    idx = jnp.min(jnp.where(s == m, sub, float(n)), axis=0, keepdims=True)
    val_ref[k:k + 1, lane0:lane0 + w] = m
    idx_ref[k:k + 1, lane0:lane0 + w] = idx
    s = jnp.where(sub == idx, NEG_INF, s)


def _odd_even_merge_sort_pairs(n):
  pairs = []
  p = 1
  while p < n:
    k = p
    while k >= 1:
      for j in range(k % p, n - k, 2 * k):
        for i in range(min(k, n - j - k)):
          if (i + j) // (2 * p) == (i + j + k) // (2 * p):
            pairs.append((i + j, i + j + k))
      k //= 2
    p *= 2
  return pairs


def _top16_distinct(s, val_ref, idx_ref, lane0):
  n, w = s.shape
  depth = n // SUBLANES
  assert depth == PEER_TOPK
  sub = lax.broadcasted_iota(jnp.int32, (SUBLANES, w), 0).astype(F32)
  col = [s[g * SUBLANES:(g + 1) * SUBLANES, :] for g in range(depth)]
  cid = [sub + float(g * SUBLANES) for g in range(depth)]
  for a, b in _odd_even_merge_sort_pairs(depth):
    up = col[b] > col[a]
    col[a], col[b] = jnp.where(up, col[b], col[a]), jnp.where(up, col[a], col[b])
    cid[a], cid[b] = jnp.where(up, cid[b], cid[a]), jnp.where(up, cid[a], cid[b])
  vals = []
  for t in range(PEER_TOPK):
    m = jnp.max(col[0], axis=0, keepdims=True)
    hit = col[0] == m
    vals.append(m)
    val_ref[t:t + 1, lane0:lane0 + w] = m
    idx_ref[t:t + 1, lane0:lane0 + w] = jnp.max(jnp.where(hit, cid[0], -1.0), axis=0, keepdims=True)
    for k in range(depth - 1 - t):
      col[k] = jnp.where(hit, col[k + 1], col[k])
      cid[k] = jnp.where(hit, cid[k + 1], cid[k])
  tied = jnp.zeros((1, w), F32)
  for t in range(PEER_TOPK - 1):
    tied = jnp.where(vals[t] == vals[t + 1], 1.0, tied)
  at_least_last = jnp.sum(jnp.where(s >= vals[-1], 1.0, 0.0), axis=0, keepdims=True)
  return jnp.where(at_least_last > float(PEER_TOPK), 1.0, tied)


PEER_HEADS_PER_ITER = 2


def _topk_body(q_ref, k1_ref, k2_ref, eid_ref, gate_ref, val_ref, idx_ref, best_ref, sel_ref, eid_s, gate_s):
  tb = q_ref.shape[0]
  kk = PEER_TOPK
  half_w = N_KEYS
  hp = PEER_HEADS_PER_ITER
  wide = hp * tb

  def head_group(hg, carry):
    def scores(hh, c):
      col = pl.multiple_of((2 * (hg * hp + hh) + c) * half_w, half_w)
      qh = q_ref[:, pl.ds(col, half_w)]
      return lax.dot_general((k1_ref, k2_ref)[c][...], qh, NT_DIMS, preferred_element_type=F32,
                             precision=lax.Precision.HIGHEST)

    halves = [(hh, c) for hh in range(hp) for c in range(2)]
    tied = [_top16_distinct(scores(hh, c), val_ref, idx_ref, (2 * hh + c) * tb) for hh, c in halves]

    @pl.when(jnp.max(jnp.concatenate(tied, axis=1)) > 0.0)
    def _():
      for hh, c in halves:
        _top16_rows(scores(hh, c), val_ref, idx_ref, (2 * hh + c) * tb)

    def pick(ref, c):
      return jnp.concatenate([ref[:, (2 * hh + c) * tb:(2 * hh + c + 1) * tb] for hh in range(hp)], axis=1)

    v1, v2 = pick(val_ref, 0), pick(val_ref, 1)
    i1, i2 = pick(idx_ref, 0), pick(idx_ref, 1)
    b16 = lax.broadcasted_iota(jnp.int32, (kk, wide), 0).astype(F32)
    b8 = lax.broadcasted_iota(jnp.int32, (SUBLANES, wide), 0).astype(F32)
    vals = [v1[0:1] + v2]
    flat = [b16]
    code = [i1[0:1] * N_KEYS + i2]
    for a in range(1, SUBLANES):
      vals.append(v1[a:a + 1] + v2[0:SUBLANES])
      flat.append(a * kk + b8)
      code.append(i1[a:a + 1] * N_KEYS + i2[0:SUBLANES])
    vals.append(v1[SUBLANES:kk] + v2[0:1])
    flat.append((b8 + SUBLANES) * kk)
    code.append(i1[SUBLANES:kk] * N_KEYS + i2[0:1])
    cand = jnp.concatenate(vals, axis=0)
    flat = jnp.concatenate(flat, axis=0)
    code = jnp.concatenate(code, axis=0)

    first = b8 == 0.0
    lists, list_code = [], []
    for b in range(kk):
      t = v1[0:SUBLANES] + v2[b:b + 1]
      lists.append(t if b < SUBLANES else jnp.where(first, t, NEG_INF))
      list_code.append(i1[0:SUBLANES] * N_KEYS + i2[b:b + 1])
    single = v1[SUBLANES:kk] + v2[0:1]
    single_code = i1[SUBLANES:kk] * N_KEYS + i2[0:1]
    popped = []
    for k in range(kk):
      m = jnp.max(jnp.maximum(lists[0], single), axis=0, keepdims=True)
      hit_l, hit_s = lists[0] == m, single == m
      popped.append(m)
      best_ref[k:k + 1, :] = m
      sel_ref[k:k + 1, :] = jnp.max(jnp.maximum(jnp.where(hit_l, list_code[0], -1.0),
                                                jnp.where(hit_s, single_code, -1.0)), axis=0, keepdims=True)
      for b in range(kk - 1 - k):
        lists[b] = jnp.where(hit_l, lists[b + 1], lists[b])
        list_code[b] = jnp.where(hit_l, list_code[b + 1], list_code[b])
      single = jnp.where(hit_s, NEG_INF, single)
    tied = jnp.sum(jnp.where(cand >= popped[-1], 1.0, 0.0), axis=0, keepdims=True) > float(kk)
    for k in range(kk - 1):
      tied = jnp.logical_or(tied, popped[k] == popped[k + 1])

    @pl.when(jnp.max(jnp.where(tied, 1.0, 0.0)) > 0.0)
    def _():
      left = cand
      for k in range(kk):
        m = jnp.max(left, axis=0, keepdims=True)
        fsel = jnp.min(jnp.where(left == m, flat, float(kk * kk)), axis=0, keepdims=True)
        hit = flat == fsel
        best_ref[k:k + 1, :] = m
        sel_ref[k:k + 1, :] = jnp.max(jnp.where(hit, code, -1.0), axis=0, keepdims=True)
        left = jnp.where(hit, NEG_INF, left)

    best = best_ref[...]
    e = jnp.exp(best - best[0:1])
    gate = e / jnp.sum(e, axis=0, keepdims=True)
    eid = sel_ref[...].astype(jnp.int32)
    for hh in range(hp):
      row = pl.multiple_of((hg * hp + hh) * kk, kk)
      gate_s[pl.ds(row, kk), :] = gate[:, hh * tb:(hh + 1) * tb]
      eid_s[pl.ds(row, kk), :] = eid[:, hh * tb:(hh + 1) * tb]
    return carry

  lax.fori_loop(0, PEER_HEADS // hp, head_group, 0)
  gate_ref[...] = gate_s[...].T
  eid_ref[...] = eid_s[...].T


def _peer_topk(qp, k1, k2):
  t = qp.shape[0]
  tb = LANES
  return pl.pallas_call(
      _topk_body,
      grid=(t // tb,),
      in_specs=[
          pl.BlockSpec((tb, qp.shape[1]), lambda i: (i, 0)),
          pl.BlockSpec((N_KEYS, N_KEYS), lambda i: (0, 0)),
          pl.BlockSpec((N_KEYS, N_KEYS), lambda i: (0, 0)),
      ],
      out_specs=[
          pl.BlockSpec((tb, PEER_SEL), lambda i: (i, 0)),
          pl.BlockSpec((tb, PEER_SEL), lambda i: (i, 0)),
      ],
      out_shape=[
          jax.ShapeDtypeStruct((t, PEER_SEL), jnp.int32),
          jax.ShapeDtypeStruct((t, PEER_SEL), F32),
      ],
      scratch_shapes=[
          pltpu.VMEM((PEER_TOPK, 2 * PEER_HEADS_PER_ITER * tb), F32),
          pltpu.VMEM((PEER_TOPK, 2 * PEER_HEADS_PER_ITER * tb), F32),
          pltpu.VMEM((PEER_TOPK, PEER_HEADS_PER_ITER * tb), F32),
          pltpu.VMEM((PEER_TOPK, PEER_HEADS_PER_ITER * tb), F32),
          pltpu.VMEM((PEER_SEL, tb), jnp.int32),
          pltpu.VMEM((PEER_SEL, tb), F32),
      ],
      compiler_params=_params(("parallel",)),
      name="peer_topk",
  )(qp, k1, k2)


def _gate_matrix_body(eid_ref, gate_ref, g_ref):
  tb = eid_ref.shape[0]
  sub = lax.broadcasted_iota(jnp.int32, (N_KEYS, PEER_SEL), 0)

  def token_group(gi, carry):
    tiles = []
    for tau in range(SUBLANES):
      t = gi * SUBLANES + tau
      e = eid_ref[pl.ds(t, 1), :]
      gt = gate_ref[pl.ds(t, 1), :]
      at = jnp.where(sub == lax.shift_right_logical(e, KEY_BITS), gt, 0.0).astype(BF16)
      bt = jnp.where(sub == jnp.bitwise_and(e, N_KEYS - 1), 1.0, 0.0).astype(BF16)
      tiles.append(lax.dot_general(at, bt, NT_DIMS, preferred_element_type=F32))
    blocks = jnp.stack([jnp.stack([tile[h * SUBLANES:(h + 1) * SUBLANES, :] for tile in tiles])
                        for h in range(N_KEYS // SUBLANES)])
    g_ref[gi] = jnp.swapaxes(blocks, 1, 2).reshape(N_KEYS, SUBLANES, N_KEYS)
    return carry

  lax.fori_loop(0, tb // SUBLANES, token_group, 0, unroll=8)


def _gate_matrix(eid_t, gate_t):
  t = eid_t.shape[0]
  tb = _pick_block(t, 64, SUBLANES)
  return pl.pallas_call(
      _gate_matrix_body,
      grid=(t // tb,),
      in_specs=[pl.BlockSpec((tb, PEER_SEL), lambda i: (i, 0)), pl.BlockSpec((tb, PEER_SEL), lambda i: (i, 0))],
      out_specs=pl.BlockSpec((tb // SUBLANES, N_KEYS, SUBLANES, N_KEYS), lambda i: (i, 0, 0, 0)),
      out_shape=jax.ShapeDtypeStruct((t // SUBLANES, N_KEYS, SUBLANES, N_KEYS), F32),
      compiler_params=_params(("parallel",)),
      name="peer_gate_matrix",
  )(eid_t, gate_t)


PEER_I1_PER_STEP = 8


def _peer_dense_body(x_ref, u_ref, v_ref, g_ref, h_ref, fg_ref, y_ref, ys_ref, *, rem):
  c = pl.program_id(1)
  s = lax.dot_general(x_ref[...], u_ref[...], NT_DIMS, preferred_element_type=F32)
  act = 0.5 * s * (1.0 + lax.erf(s * (2.0 ** -0.5)))
  tm = x_ref.shape[0]
  coef = jnp.concatenate(
      [g_ref[:, j].reshape(tm, N_KEYS) * act[:, j * N_KEYS:(j + 1) * N_KEYS] for j in range(PEER_I1_PER_STEP)],
      axis=1)
  contrib = jnp.dot(coef.astype(BF16), v_ref[...], preferred_element_type=F32)

  @pl.when(c == 0)
  def _():
    y_ref[...] = contrib

  @pl.when(c > 0)
  def _():
    y_ref[...] += contrib

  @pl.when(c == pl.num_programs(1) - 1)
  def _():
    y = _rms(h_ref[...] + y_ref[...], fg_ref[...])
    y_ref[...] = y

    @pl.when(pl.program_id(0) == pl.num_programs(0) - 1)
    def _():
      ys_ref[...] = y[rem:, :]


def _peer_dense(xn2, u_bf, v_bf, gmat, h, final_g, tp):
  t = xn2.shape[0]
  ts = t - tp
  tm = _pick_block(t, 640, LANES)
  te = PEER_I1_PER_STEP * N_KEYS
  return pl.pallas_call(
      functools.partial(_peer_dense_body, rem=_tail_rows(tp, ts, tm)),
      grid=(t // tm, N_EXPERTS // te),
      in_specs=[
          pl.BlockSpec((tm, D_MODEL), lambda i, c: (i, 0)),
          pl.BlockSpec((te, D_MODEL), lambda i, c: (c, 0)),
          pl.BlockSpec((te, D_MODEL), lambda i, c: (c, 0)),
          pl.BlockSpec((tm // SUBLANES, PEER_I1_PER_STEP, SUBLANES, N_KEYS), lambda i, c: (i, c, 0, 0)),
          pl.BlockSpec((tm, D_MODEL), lambda i, c: (i, 0), pipeline_mode=pl.Buffered(1)),
          pl.BlockSpec((1, D_MODEL), lambda i, c: (0, 0)),
      ],
      out_specs=[_prompt_rows_spec(tm, D_MODEL, tp), _sample_rows_spec(ts, D_MODEL)],
      out_shape=[jax.ShapeDtypeStruct((tp, D_MODEL), F32), jax.ShapeDtypeStruct((ts, D_MODEL), F32)],
      compiler_params=_params(("arbitrary", "arbitrary")),
      name="peer_dense",
  )(xn2, u_bf, v_bf, gmat, h, final_g)


def _rope_tables(pos):
  half = ROT_DIM // 2
  inv_freq = jnp.float32(ROPE_THETA) ** (-jnp.arange(half, dtype=F32) * 2.0 / ROT_DIM)
  ang = pos.astype(F32)[:, None] * inv_freq[None, :]
  cos, sin = jnp.cos(ang), jnp.sin(ang)
  n = pos.shape[0]
  ones = jnp.ones((n, HEAD_DIM - ROT_DIM), F32)
  zeros = jnp.zeros((n, HEAD_DIM - ROT_DIM), F32)
  zh = jnp.zeros((n, half), F32)
  cos_c = jnp.concatenate([cos, cos, ones], axis=1)
  sa_c = jnp.concatenate([-sin, zh, zeros], axis=1)
  sb_c = jnp.concatenate([zh, sin, zeros], axis=1)
  two = lambda a: jnp.concatenate([a, a], axis=1)
  return two(cos_c), two(sa_c), two(sb_c)


def kernel(x_prompt, x_sample, cache_k, cache_v, state_conv, state_h, page_table, norm1_g, w_in, lambda_q1, lambda_k1, lambda_q2, lambda_k2, subln_g, conv_w, conv_b, lru_wa, lru_ba, lru_wx, lru_bx, lru_lambda, w_att_up, w_lru_up, w_out, norm2_g, peer_wq, peer_k1, peer_k2, peer_u, peer_v, final_g):
  batch, seq, _ = x_prompt.shape
  dec_batch, dec_seq, _ = x_sample.shape
  n_pages = page_table.shape[1]
  past_len = n_pages * PAGE_SIZE
  tp = batch * seq
  ts = dec_batch * dec_seq
  assert w_in.shape[0] == 1, "one layer"

  x_p = x_prompt.reshape(tp, D_MODEL)
  x_s = x_sample.reshape(ts, D_MODEL)
  tabs_p = _rope_tables(jnp.arange(seq, dtype=jnp.int32))
  tabs_s = _rope_tables(past_len + jnp.arange(dec_seq, dtype=jnp.int32))
  cos_t, sa_t, sb_t = [jnp.concatenate([jnp.tile(a, (batch, 1)), jnp.tile(b, (dec_batch, 1))], axis=0)
                       for a, b in zip(tabs_p, tabs_s)]
  lam = (jnp.exp(jnp.sum(lambda_q1[0].astype(F32) * lambda_k1[0].astype(F32)))
         - jnp.exp(jnp.sum(lambda_q2[0].astype(F32) * lambda_k2[0].astype(F32))) + LAM_INIT).reshape(1)
  row = lambda a: a.reshape(1, -1)

  proj = _inproj(x_p, x_s, row(norm1_g[0]), w_in[0].astype(BF16), cos_t, sa_t, sb_t)

  att_p, u_bf, v_bf = _prompt_attention(proj, lam, row(subln_g[0]), batch, seq, peer_u[0], peer_v[0])
  proj_s = proj[tp:]
  q_s = proj_s[:, :OFF_Q].reshape(dec_batch, dec_seq, N_KV_HEADS, 2, 2, HEAD_DIM)
  q_s = q_s.transpose(0, 2, 4, 3, 1, 5).reshape(dec_batch, N_KV_HEADS, 2, 2 * dec_seq, HEAD_DIM)
  wq = jnp.einsum("bgcnd,ce->bgcned", q_s, jnp.eye(2, dtype=F32))
  wq = wq.reshape(dec_batch, N_KV_HEADS * 2 * 2 * dec_seq, HEAD_W).astype(BF16)
  new_rows = lambda a: a.reshape(dec_batch, dec_seq * N_KV_HEADS, HEAD_W).astype(BF16)
  k_new, v_new = proj_s[:, OFF_Q:OFF_K], proj_s[:, OFF_K:OFF_V]
  n_pool = cache_k.shape[1]
  att_s = _sample_attention(page_table, lam, wq, new_rows(k_new), new_rows(v_new), row(subln_g[0]),
                            cache_k.reshape(n_pool, PAGE_ROWS, HEAD_W),
                            cache_v.reshape(n_pool, PAGE_ROWS, HEAD_W), dec_seq)
  att_s = att_s.reshape(dec_batch, N_KV_HEADS, 2, dec_seq, HEAD_W).transpose(0, 3, 1, 2, 4)
  att_s = att_s.reshape(ts, ATT_WIDTH).astype(BF16)

  lru_w = (conv_w[0], row(conv_b[0]), lru_wa[0], row(lru_ba[0]), lru_wx[0], row(lru_bx[0]), row(lru_lambda[0]))
  lru_p, h_p = _lru_prompt(proj, lru_w, batch, seq)
  xl_s = proj_s[:, OFF_V:OFF_L].reshape(dec_batch, dec_seq, LRU_WIDTH)
  lru_s, h_s = _lru_sample(xl_s.transpose(1, 0, 2), state_conv[0].transpose(1, 0, 2), state_h[0], lru_w)
  lru_s = lru_s.transpose(1, 0, 2).reshape(ts, LRU_WIDTH)

  h_all, xn2 = _merge(att_p, att_s, lru_p, lru_s, proj, x_p, x_s, w_att_up[0].astype(BF16),
                      w_lru_up[0].astype(BF16), w_out[0].astype(BF16), row(norm2_g[0]))
  qp = _matmul(xn2, peer_wq[0].astype(BF16))
  eid_t, gate_t = _peer_topk(qp, peer_k1[0], peer_k2[0])
  gmat = _gate_matrix(eid_t, gate_t)
  y_p, y_s = _peer_dense(xn2, u_bf, v_bf, gmat, h_all, row(final_g), tp)

  kv_shape_p = (1, batch, seq, N_KV_HEADS, HEAD_W)
  kv_shape_s = (1, dec_batch, dec_seq, N_KV_HEADS, HEAD_W)
  tail = CONV_W - 1
  conv_p = jnp.stack([proj[(b + 1) * seq - tail:(b + 1) * seq, OFF_V:OFF_L] for b in range(batch)])
  conv_s = jnp.concatenate([state_conv[0].astype(F32), xl_s], axis=1)[:, -tail:]
  return (
      y_p.reshape(batch, seq, D_MODEL),
      y_s.reshape(dec_batch, dec_seq, D_MODEL),
      proj[:tp, OFF_Q:OFF_K].reshape(kv_shape_p),
      proj[:tp, OFF_K:OFF_V].reshape(kv_shape_p),
      conv_p[None],
      h_p.reshape(1, batch, LRU_WIDTH),
      k_new.reshape(kv_shape_s),
      v_new.reshape(kv_shape_s),
      conv_s[None],
      h_s[None],
  )
```

```python
import functools
import math

import jax
import jax.numpy as jnp
from jax import lax
from jax.experimental import pallas as pl
from jax.experimental.pallas import tpu as pltpu

F32 = jnp.float32
BF16 = jnp.bfloat16

D_MODEL = 2048
N_HEADS = 8
N_KV_HEADS = 4
HEAD_DIM = 64
HEAD_W = 2 * HEAD_DIM
ROT_DIM = HEAD_DIM // 4
ROPE_THETA = 500000.0
ATT_WIDTH = N_HEADS * HEAD_W
KV_WIDTH = N_KV_HEADS * HEAD_W
LRU_WIDTH = 1024
LRU_BLOCKS = 8
LRU_BLOCK_W = LRU_WIDTH // LRU_BLOCKS
CONV_W = 4
LRU_C = 8.0
N_KEYS = 128
KEY_BITS = 7
N_EXPERTS = N_KEYS * N_KEYS
PEER_HEADS = 8
PEER_TOPK = 16
PEER_SEL = PEER_HEADS * PEER_TOPK
PAGE_SIZE = 128
RMS_EPS = 1e-6
OFF_Q = ATT_WIDTH
OFF_K = OFF_Q + KV_WIDTH
OFF_V = OFF_K + KV_WIDTH
OFF_L = OFF_V + LRU_WIDTH
OFF_GA = OFF_L + D_MODEL
IN_WIDTH = OFF_GA + D_MODEL
LAM_INIT = 0.8 - 0.6 * math.exp(0.0)

LANES = 128
SUBLANES = 8
VMEM_LIMIT_BYTES = 56 * 1024 * 1024

NEG_INF = float("-inf")
LOG2_E = math.log2(math.e)
NT_DIMS = (((1,), (1,)), ((), ()))


def _pick_block(total, cap, quantum):
  best = None
  b = quantum
  while b <= min(cap, total):
    if total % b == 0:
      best = b
    b += quantum
  assert best is not None, (total, cap, quantum)
  return best


def _params(sem, vmem=VMEM_LIMIT_BYTES):
  return pltpu.CompilerParams(dimension_semantics=sem, vmem_limit_bytes=vmem)


def _rms(x, g):
  var = jnp.mean(x * x, axis=-1, keepdims=True)
  return x * lax.rsqrt(var + RMS_EPS) * g


def _tail_rows(tp, ts, tm):
  assert (tp + ts) % tm == 0 and 0 < ts < tm, (tp, ts, tm)
  return tm - ts


def _prompt_rows_spec(tm, width, tp):
  last_prompt_block = pl.cdiv(tp, tm) - 1
  return pl.BlockSpec((tm, width), lambda i, *_: (jnp.minimum(i, last_prompt_block), 0))


def _sample_rows_spec(ts, width):
  return pl.BlockSpec((ts, width), lambda i, *_: (0, 0))


def _stacked_rows(i, n_blocks, rem, p_ref, s_ref, emit):
  tm = rem + s_ref.shape[0]

  @pl.when(i < n_blocks - 1)
  def _():
    emit(slice(0, tm), p_ref[...])

  @pl.when(i == n_blocks - 1)
  def _():
    emit(slice(0, rem), p_ref[0:rem, :])
    emit(slice(rem, tm), s_ref[...])


def _inproj_body(xp_ref, xs_ref, g_ref, w_ref, cos_ref, sa_ref, sb_ref, o_ref, xn_ref, *, tn, n_rope_blocks, rem):
  i = pl.program_id(0)
  j = pl.program_id(1)

  @pl.when(j == 0)
  def _():
    def emit(rows, x):
      xn_ref[rows, :] = _rms(x, g_ref[...]).astype(BF16)
    _stacked_rows(i, pl.num_programs(0), rem, xp_ref, xs_ref, emit)

  acc = jnp.dot(xn_ref[...], w_ref[...], preferred_element_type=F32)

  @pl.when(j < n_rope_blocks)
  def _():
    reps = tn // LANES
    cos = jnp.concatenate([cos_ref[...]] * reps, axis=1)
    sa = jnp.concatenate([sa_ref[...]] * reps, axis=1)
    sb = jnp.concatenate([sb_ref[...]] * reps, axis=1)
    half = ROT_DIM // 2
    rot = acc * cos + pltpu.roll(acc, tn - half, 1) * sa + pltpu.roll(acc, half, 1) * sb
    col = j * tn + lax.broadcasted_iota(jnp.int32, acc.shape, 1)
    o_ref[...] = jnp.where(col < OFF_Q, rot * (HEAD_DIM ** -0.5), jnp.where(col < OFF_K, rot, acc))

  @pl.when(j >= n_rope_blocks)
  def _():
    o_ref[...] = acc


def _inproj(x_p, x_s, norm_g, w_in_bf, cos_t, sa_t, sb_t):
  tp, ts = x_p.shape[0], x_s.shape[0]
  t = tp + ts
  tm = _pick_block(t, 832, 64)
  tn = 1024
  body = functools.partial(_inproj_body, tn=tn, n_rope_blocks=pl.cdiv(OFF_K, tn), rem=_tail_rows(tp, ts, tm))
  return pl.pallas_call(
      body,
      grid=(t // tm, IN_WIDTH // tn),
      in_specs=[
          _prompt_rows_spec(tm, D_MODEL, tp),
          _sample_rows_spec(ts, D_MODEL),
          pl.BlockSpec((1, D_MODEL), lambda i, j: (0, 0)),
          pl.BlockSpec((D_MODEL, tn), lambda i, j: (0, j)),
          pl.BlockSpec((tm, LANES), lambda i, j: (i, 0)),
          pl.BlockSpec((tm, LANES), lambda i, j: (i, 0)),
          pl.BlockSpec((tm, LANES), lambda i, j: (i, 0)),
      ],
      out_specs=pl.BlockSpec((tm, tn), lambda i, j: (i, j)),
      out_shape=jax.ShapeDtypeStruct((t, IN_WIDTH), F32),
      scratch_shapes=[pltpu.VMEM((tm, D_MODEL), BF16)],
      compiler_params=_params(("parallel", "arbitrary")),
      name="inproj",
  )(x_p, x_s, norm_g, w_in_bf, cos_t, sa_t, sb_t)


def _subln(o, g):
  var = jnp.mean(o * o, axis=-1, keepdims=True)
  return o * lax.rsqrt(var + RMS_EPS) * g * (1.0 - LAM_INIT)


def _pattn_body(qi_ref, ki_ref, lam_ref, q_ref, k_ref, v_ref, g_ref, tu_ref, tv_ref, o_ref, tub_ref, tvb_ref,
                q4_ref, m_ref, l_ref, acc_ref, *, tq, tk, cast_steps):
  qi = qi_ref[pl.program_id(2)]
  ki = ki_ref[pl.program_id(2)]
  cols = 2 * 2 * tq

  @pl.when(pl.program_id(2) < cast_steps)
  def _():
    tub_ref[...] = tu_ref[...].astype(BF16)
    tvb_ref[...] = tv_ref[...].astype(BF16)

  @pl.when(ki == 0)
  def _():
    m_ref[...] = jnp.full((1, cols), NEG_INF, F32)
    l_ref[...] = jnp.zeros((1, cols), F32)
    acc_ref[...] = jnp.zeros((HEAD_W, cols), F32)
    lane = lax.broadcasted_iota(jnp.int32, (tq, HEAD_W), 1)
    for r in range(2):
      qh = q_ref[:, r * HEAD_W:(r + 1) * HEAD_W] * LOG2_E
      q4_ref[(2 * r) * tq:(2 * r + 1) * tq, :] = jnp.where(lane < HEAD_DIM, qh, 0.0).astype(BF16)
      q4_ref[(2 * r + 1) * tq:(2 * r + 2) * tq, :] = jnp.where(lane >= HEAD_DIM, qh, 0.0).astype(BF16)

  def update(diagonal):
    k = k_ref[...].astype(BF16)
    s = lax.dot_general(k, q4_ref[...], NT_DIMS, preferred_element_type=F32)
    if diagonal:
      visible = (lax.broadcasted_iota(jnp.int32, (tk, tq), 0) <= lax.broadcasted_iota(jnp.int32, (tk, tq), 1))
      s = jnp.concatenate([jnp.where(visible, s[:, j * tq:(j + 1) * tq], NEG_INF) for j in range(4)], axis=1)
    m_old = m_ref[...]
    m_new = jnp.maximum(m_old, jnp.max(s, axis=0, keepdims=True))
    alpha = jnp.exp2(m_old - m_new)
    p = jnp.exp2(s - m_new)
    l_ref[...] = alpha * l_ref[...] + jnp.sum(p, axis=0, keepdims=True)
    vt = v_ref[...].T.astype(BF16)
    acc_ref[...] = alpha * acc_ref[...] + jnp.dot(vt, p.astype(BF16), preferred_element_type=F32)
    m_ref[...] = m_new

  @pl.when(ki < qi)
  def _():
    update(False)

  @pl.when(ki == qi)
  def _():
    update(True)
    lam = lam_ref[0]
    o = acc_ref[...] * (1.0 / l_ref[...])
    for r in range(2):
      d = o[:, (2 * r) * tq:(2 * r + 1) * tq] - lam * o[:, (2 * r + 1) * tq:(2 * r + 2) * tq]
      var = jnp.mean(d * d, axis=0, keepdims=True)
      dn = d * (lax.rsqrt(var + RMS_EPS) * (1.0 - LAM_INIT))
      o_ref[:, r * HEAD_W:(r + 1) * HEAD_W] = (dn.T * g_ref[...]).astype(o_ref.dtype)


def _prompt_attention(proj, lam, subln_g, batch, seq, table_u, table_v):
  tq = tk = _pick_block(seq, 512, LANES)
  nq = seq // tq
  gw = 2 * HEAD_W
  pairs = [(qi, ki) for qi in range(nq) for ki in range(qi + 1)]
  qi_of = jnp.asarray([p[0] for p in pairs], jnp.int32)
  ki_of = jnp.asarray([p[1] for p in pairs], jnp.int32)
  n_groups = batch * N_KV_HEADS
  cast_steps = 1 << (len(pairs).bit_length() - 1)
  cast_rows = N_EXPERTS // (n_groups * cast_steps)
  assert cast_rows * n_groups * cast_steps == N_EXPERTS and cast_rows % (2 * SUBLANES) == 0
  table_spec = pl.BlockSpec(
      (cast_rows, D_MODEL),
      lambda b, g, p, qo, ko: ((b * N_KV_HEADS + g) * cast_steps + jnp.minimum(p, cast_steps - 1), 0))
  body = functools.partial(_pattn_body, tq=tq, tk=tk, cast_steps=cast_steps)
  grid_spec = pltpu.PrefetchScalarGridSpec(
      num_scalar_prefetch=2,
      grid=(batch, N_KV_HEADS, len(pairs)),
      in_specs=[
          pl.BlockSpec(memory_space=pltpu.SMEM),
          pl.BlockSpec((tq, gw), lambda b, g, p, qo, ko: (b * nq + qo[p], g)),
          pl.BlockSpec((tk, HEAD_W), lambda b, g, p, qo, ko: (b * nq + ko[p], OFF_Q // HEAD_W + g)),
          pl.BlockSpec((tk, HEAD_W), lambda b, g, p, qo, ko: (b * nq + ko[p], OFF_K // HEAD_W + g)),
          pl.BlockSpec((1, HEAD_W), lambda b, g, p, qo, ko: (0, 0)),
          table_spec, table_spec,
      ],
      out_specs=[pl.BlockSpec((tq, gw), lambda b, g, p, qo, ko: (b * nq + qo[p], g)), table_spec, table_spec],
      scratch_shapes=[
          pltpu.VMEM((4 * tq, HEAD_W), BF16),
          pltpu.VMEM((1, 4 * tq), F32),
          pltpu.VMEM((1, 4 * tq), F32),
          pltpu.VMEM((HEAD_W, 4 * tq), F32),
      ],
  )
  return pl.pallas_call(
      body,
      grid_spec=grid_spec,
      out_shape=[jax.ShapeDtypeStruct((batch * seq, ATT_WIDTH), BF16),
                 jax.ShapeDtypeStruct(table_u.shape, BF16), jax.ShapeDtypeStruct(table_v.shape, BF16)],
      compiler_params=_params(("parallel", "parallel", "arbitrary")),
      name="prompt_attention",
  )(qi_of, ki_of, lam, proj, proj, proj, subln_g, table_u, table_v)


PAGES_PER_CHUNK = 8
SATTN_SLOTS = 3
PAGE_ROWS = PAGE_SIZE * N_KV_HEADS


def _sattn_body(pt_ref, lam_ref, wq_ref, kn_ref, vn_ref, g_ref, ck_ref, cv_ref, o_ref,
                kbuf, vbuf, bias_ref, sem, *, n_chunks, dec_seq):
  b = pl.program_id(0)
  nb = pl.num_programs(0)
  rows = wq_ref.shape[1]
  gr = rows // N_KV_HEADS
  cols = PAGES_PER_CHUNK * PAGE_ROWS

  def copies(bb, c, slot):
    out = []
    for p in range(PAGES_PER_CHUNK):
      page = pt_ref[bb, c * PAGES_PER_CHUNK + p]
      out.append(pltpu.make_async_copy(ck_ref.at[page], kbuf.at[slot, p], sem.at[0, slot]))
      out.append(pltpu.make_async_copy(cv_ref.at[page], vbuf.at[slot, p], sem.at[1, slot]))
    return out

  def start(bb, c, slot):
    for n, cp in enumerate(copies(bb, c, slot)):
      cp.start(priority=n % 2)

  def same_head(shape):
    head_of_col = jnp.bitwise_and(lax.broadcasted_iota(jnp.int32, shape, 1), N_KV_HEADS - 1)
    head_of_row = lax.broadcasted_iota(jnp.int32, shape, 0) // gr
    return head_of_col == head_of_row

  ahead = SATTN_SLOTS - 1

  def slot_of(bb, c):
    return lax.rem(bb * n_chunks + c, SATTN_SLOTS)

  @pl.when(b == 0)
  def _():
    for c in range(ahead):
      start(0, c, c)
    bias_ref[...] = jnp.where(same_head((rows, cols)), 0.0, NEG_INF)

  wq = wq_ref[0]

  def softmax_step(carry, s, v):
    m_old, l_old, acc = carry
    m_new = jnp.maximum(m_old, jnp.max(s, axis=1, keepdims=True))
    alpha = jnp.exp(m_old - m_new)
    p = jnp.exp(s - m_new)
    l_new = alpha * l_old + jnp.sum(p, axis=1, keepdims=True)
    acc = alpha * acc + jnp.dot(p.astype(BF16), v, preferred_element_type=F32)
    return m_new, l_new, acc

  def chunk(c, carry):
    slot = slot_of(b, c)

    @pl.when(c + ahead < n_chunks)
    def _():
      start(b, c + ahead, slot_of(b, c + ahead))

    @pl.when(jnp.logical_and(c + ahead >= n_chunks, b + 1 < nb))
    def _():
      start(b + 1, c + ahead - n_chunks, slot_of(b + 1, c + ahead - n_chunks))

    for cp in copies(b, c, slot):
      cp.wait()
    kc = kbuf[slot].reshape(cols, HEAD_W).astype(BF16)
    vc = vbuf[slot].reshape(cols, HEAD_W).astype(BF16)
    s = lax.dot_general(wq, kc, NT_DIMS, preferred_element_type=F32) + bias_ref[...]
    return softmax_step(carry, s, vc)

  init = (jnp.full((rows, 1), NEG_INF, F32), jnp.zeros((rows, 1), F32), jnp.zeros((rows, HEAD_W), F32))
  carry = lax.fori_loop(0, n_chunks, chunk, init)

  new_rows = kn_ref.shape[1]
  s = lax.dot_general(wq, kn_ref[0], NT_DIMS, preferred_element_type=F32)
  t_of_row = lax.broadcasted_iota(jnp.int32, (rows, new_rows), 0) % dec_seq
  t_of_col = lax.broadcasted_iota(jnp.int32, (rows, new_rows), 1) // N_KV_HEADS
  visible = jnp.logical_and(same_head((rows, new_rows)), t_of_col <= t_of_row)
  _, l_fin, acc = softmax_step(carry, jnp.where(visible, s, NEG_INF), vn_ref[0])

  o = acc / l_fin
  lam = lam_ref[0]
  for g in range(N_KV_HEADS):
    blk = o[g * gr:(g + 1) * gr]
    d = blk[:gr // 2] - lam * blk[gr // 2:]
    o_ref[0, g * (gr // 2):(g + 1) * (gr // 2), :] = _subln(d, g_ref[...])


def _sample_attention(page_table, lam, wq, k_new, v_new, subln_g, cache_k, cache_v, dec_seq):
  dec_batch, n_pages = page_table.shape
  assert n_pages % PAGES_PER_CHUNK == 0
  n_chunks = n_pages // PAGES_PER_CHUNK
  assert n_chunks >= SATTN_SLOTS - 1
  rows = wq.shape[1]
  new_rows = k_new.shape[1]
  body = functools.partial(_sattn_body, n_chunks=n_chunks, dec_seq=dec_seq)
  grid_spec = pltpu.PrefetchScalarGridSpec(
      num_scalar_prefetch=1,
      grid=(dec_batch,),
      in_specs=[
          pl.BlockSpec(memory_space=pltpu.SMEM),
          pl.BlockSpec((1, rows, HEAD_W), lambda b, pt: (b, 0, 0)),
          pl.BlockSpec((1, new_rows, HEAD_W), lambda b, pt: (b, 0, 0)),
          pl.BlockSpec((1, new_rows, HEAD_W), lambda b, pt: (b, 0, 0)),
          pl.BlockSpec((1, HEAD_W), lambda b, pt: (0, 0)),
          pl.BlockSpec(memory_space=pl.ANY),
          pl.BlockSpec(memory_space=pl.ANY),
      ],
      out_specs=pl.BlockSpec((1, rows // 2, HEAD_W), lambda b, pt: (b, 0, 0)),
      scratch_shapes=[
          pltpu.VMEM((SATTN_SLOTS, PAGES_PER_CHUNK, PAGE_ROWS, HEAD_W), F32),
          pltpu.VMEM((SATTN_SLOTS, PAGES_PER_CHUNK, PAGE_ROWS, HEAD_W), F32),
          pltpu.VMEM((rows, PAGES_PER_CHUNK * PAGE_ROWS), F32),
          pltpu.SemaphoreType.DMA((2, SATTN_SLOTS)),
      ],
  )
  return pl.pallas_call(
      body,
      grid_spec=grid_spec,
      out_shape=jax.ShapeDtypeStruct((dec_batch, rows // 2, HEAD_W), F32),
      compiler_params=_params(("arbitrary",)),
      name="sample_attention",
  )(page_table, lam, wq, k_new, v_new, subln_g, cache_k, cache_v)


def _lru_gates(xc, wa_ref, ba, wx_ref, bx, lam):
  ra, ix = [], []
  for n in range(LRU_BLOCKS):
    xb = xc[:, n * LRU_BLOCK_W:(n + 1) * LRU_BLOCK_W].astype(BF16)
    ra.append(jnp.dot(xb, wa_ref[n].astype(BF16), preferred_element_type=F32))
    ix.append(jnp.dot(xb, wx_ref[n].astype(BF16), preferred_element_type=F32))
  r = jax.nn.sigmoid(jnp.concatenate(ra, axis=1) + ba)
  i = jax.nn.sigmoid(jnp.concatenate(ix, axis=1) + bx)
  neg = -lam
  softplus = jnp.maximum(neg, 0.0) + jnp.log1p(jnp.exp(-jnp.abs(neg)))
  log_a = -LRU_C * r * softplus
  a = jnp.exp(log_a)
  u = jnp.sqrt(1.0 - a * a) * (i * xc)
  return a, u


def _conv(rows_of, cw_ref, cb):
  out = rows_of(0) * cw_ref[0:1, :]
  for j in range(1, CONV_W):
    out = out + rows_of(j) * cw_ref[j:j + 1, :]
  return out + cb


def _lru_prompt_body(xl_ref, cw_ref, cb_ref, wa_ref, ba_ref, wx_ref, bx_ref, lam_ref, y_ref, hl_ref,
                     ext_ref, a_ref, u_ref, hs_ref, h_ref, *, tt):
  ti = pl.program_id(1)
  head = SUBLANES

  @pl.when(ti == 0)
  def _():
    ext_ref[0:head, :] = jnp.zeros((head, LRU_WIDTH), F32)
    h_ref[...] = jnp.zeros((1, LRU_WIDTH), F32)

  ext_ref[head:head + tt, :] = xl_ref[...]
  xc = _conv(lambda j: ext_ref[head - (CONV_W - 1) + j:head - (CONV_W - 1) + j + tt, :], cw_ref, cb_ref[...])
  a, u = _lru_gates(xc, wa_ref, ba_ref[...], wx_ref, bx_ref[...], lam_ref[...])
  a_ref[...] = a
  u_ref[...] = u

  row = lax.broadcasted_iota(jnp.int32, (SUBLANES, LRU_WIDTH), 0)

  def tile_steps(i, h):
    rows = pl.ds(pl.multiple_of(i * SUBLANES, SUBLANES), SUBLANES)
    a_cum, u_cum = a_ref[rows, :], u_ref[rows, :]
    d = 1
    while d < SUBLANES:
      keep = row >= d
      u_cum = jnp.where(keep, a_cum * pltpu.roll(u_cum, d, 0) + u_cum, u_cum)
      a_cum = jnp.where(keep, a_cum * pltpu.roll(a_cum, d, 0), a_cum)
      d *= 2
    hs = a_cum * h + u_cum
    hs_ref[rows, :] = hs
    return hs[SUBLANES - 1:SUBLANES, :]

  h_fin = lax.fori_loop(0, tt // SUBLANES, tile_steps, h_ref[...], unroll=4)
  h_ref[...] = h_fin
  y_ref[...] = hs_ref[...].astype(y_ref.dtype)
  ext_ref[0:head, :] = ext_ref[tt:tt + head, :]

  @pl.when(ti == pl.num_programs(1) - 1)
  def _():
    hl_ref[0] = h_fin


def _lru_weight_specs():
  zero2 = (lambda *a: (0, 0))
  zero3 = (lambda *a: (0, 0, 0))
  return [
      pl.BlockSpec((CONV_W, LRU_WIDTH), zero2),
      pl.BlockSpec((1, LRU_WIDTH), zero2),
      pl.BlockSpec((LRU_BLOCKS, LRU_BLOCK_W, LRU_BLOCK_W), zero3),
      pl.BlockSpec((1, LRU_WIDTH), zero2),
      pl.BlockSpec((LRU_BLOCKS, LRU_BLOCK_W, LRU_BLOCK_W), zero3),
      pl.BlockSpec((1, LRU_WIDTH), zero2),
      pl.BlockSpec((1, LRU_WIDTH), zero2),
  ]


def _lru_prompt(proj, lru_w, batch, seq):
  tt = _pick_block(seq, 512, LANES)
  nt = seq // tt
  body = functools.partial(_lru_prompt_body, tt=tt)
  return pl.pallas_call(
      body,
      grid=(batch, nt),
      in_specs=[pl.BlockSpec((tt, LRU_WIDTH), lambda b, ti: (b * nt + ti, OFF_V // LRU_WIDTH))]
      + _lru_weight_specs(),
      out_specs=[
          pl.BlockSpec((tt, LRU_WIDTH), lambda b, ti: (b * nt + ti, 0)),
          pl.BlockSpec((1, 1, LRU_WIDTH), lambda b, ti: (b, 0, 0)),
      ],
      out_shape=[
          jax.ShapeDtypeStruct((batch * seq, LRU_WIDTH), BF16),
          jax.ShapeDtypeStruct((batch, 1, LRU_WIDTH), F32),
      ],
      scratch_shapes=[
          pltpu.VMEM((tt + 2 * SUBLANES, LRU_WIDTH), F32),
          pltpu.VMEM((tt, LRU_WIDTH), F32),
          pltpu.VMEM((tt, LRU_WIDTH), F32),
          pltpu.VMEM((tt, LRU_WIDTH), F32),
          pltpu.VMEM((1, LRU_WIDTH), F32),
      ],
      compiler_params=_params(("parallel", "arbitrary")),
      name="lru_prompt",
  )(proj, *lru_w)


def _lru_sample_body(xl_ref, cbuf_ref, h0_ref, cw_ref, cb_ref, wa_ref, ba_ref, wx_ref, bx_ref, lam_ref,
                     y_ref, hl_ref, *, dec_seq):
  ext = [cbuf_ref[j] for j in range(CONV_W - 1)] + [xl_ref[t] for t in range(dec_seq)]
  h = h0_ref[...]
  for t in range(dec_seq):
    xc = _conv(lambda j: ext[t + j], cw_ref, cb_ref[...])
    a, u = _lru_gates(xc, wa_ref, ba_ref[...], wx_ref, bx_ref[...], lam_ref[...])
    h = a * h + u
    y_ref[t] = h.astype(y_ref.dtype)
  hl_ref[...] = h


def _lru_sample(xl_t, cbuf_t, h0, lru_w):
  dec_seq, dec_batch, _ = xl_t.shape
  body = functools.partial(_lru_sample_body, dec_seq=dec_seq)
  return pl.pallas_call(
      body,
      out_shape=[
          jax.ShapeDtypeStruct((dec_seq, dec_batch, LRU_WIDTH), BF16),
          jax.ShapeDtypeStruct((dec_batch, LRU_WIDTH), F32),
      ],
      name="lru_sample",
  )(xl_t, cbuf_t, h0, *lru_w)


def _merge_body(attp_ref, atts_ref, lrup_ref, lrus_ref, ga0_ref, ga1_ref, gl0_ref, gl1_ref, xp_ref, xs_ref,
                wa_ref, wl_ref, wo_ref, g2_ref, h_ref, xn_ref, att_buf, lru_buf, x_buf, *, rem):
  i = pl.program_id(0)
  n = pl.num_programs(0)

  def fill(buf):
    def emit(rows, v):
      buf[rows, :] = v
    return emit

  _stacked_rows(i, n, rem, attp_ref, atts_ref, fill(att_buf))
  _stacked_rows(i, n, rem, lrup_ref, lrus_ref, fill(lru_buf))
  _stacked_rows(i, n, rem, xp_ref, xs_ref, fill(x_buf))
  a1 = jnp.dot(att_buf[...], wa_ref[...], preferred_element_type=F32)
  a2 = jnp.dot(lru_buf[...], wl_ref[...], preferred_element_type=F32)
  ga = jnp.concatenate([ga0_ref[...], ga1_ref[...]], axis=1)
  gl = jnp.concatenate([gl0_ref[...], gl1_ref[...]], axis=1)
  m = jax.nn.sigmoid(ga) * a1 + jax.nn.sigmoid(gl) * a2
  h = x_buf[...] + jnp.dot(m.astype(BF16), wo_ref[...], preferred_element_type=F32)
  h_ref[...] = h
  xn_ref[...] = _rms(h, g2_ref[...]).astype(BF16)


def _merge(att_p, att_s, lru_p, lru_s, proj, x_p, x_s, wa, wl, wo, g2):
  tp, ts = x_p.shape[0], x_s.shape[0]
  t = tp + ts
  tm = _pick_block(t, 320, 64)
  half = D_MODEL // 2
  const = lambda shape: pl.BlockSpec(shape, lambda i: (0, 0), pipeline_mode=pl.Buffered(1))
  gate = lambda blk: pl.BlockSpec((tm, half), lambda i: (i, blk))
  return pl.pallas_call(
      functools.partial(_merge_body, rem=_tail_rows(tp, ts, tm)),
      grid=(t // tm,),
      in_specs=[
          _prompt_rows_spec(tm, ATT_WIDTH, tp), _sample_rows_spec(ts, ATT_WIDTH),
          _prompt_rows_spec(tm, LRU_WIDTH, tp), _sample_rows_spec(ts, LRU_WIDTH),
          gate(OFF_L // half), gate(OFF_L // half + 1), gate(OFF_GA // half), gate(OFF_GA // half + 1),
          _prompt_rows_spec(tm, D_MODEL, tp), _sample_rows_spec(ts, D_MODEL),
          const((ATT_WIDTH, D_MODEL)), const((LRU_WIDTH, D_MODEL)), const((D_MODEL, D_MODEL)),
          const((1, D_MODEL)),
      ],
      scratch_shapes=[
          pltpu.VMEM((tm, ATT_WIDTH), BF16),
          pltpu.VMEM((tm, LRU_WIDTH), BF16),
          pltpu.VMEM((tm, D_MODEL), F32),
      ],
      out_specs=[
          pl.BlockSpec((tm, D_MODEL), lambda i: (i, 0)),
          pl.BlockSpec((tm, D_MODEL), lambda i: (i, 0)),
      ],
      out_shape=[
          jax.ShapeDtypeStruct((t, D_MODEL), F32),
          jax.ShapeDtypeStruct((t, D_MODEL), BF16),
      ],
      compiler_params=_params(("parallel",)),
      name="merge",
  )(att_p, att_s, lru_p, lru_s, proj, proj, proj, proj, x_p, x_s, wa, wl, wo, g2)


def _mm_body(x_ref, w_ref, o_ref):
  o_ref[...] = jnp.dot(x_ref[...], w_ref[...], preferred_element_type=F32)


def _matmul(x, w):
  t, kdim = x.shape
  n = w.shape[1]
  tm = _pick_block(t, 640, LANES)
  tn = 512
  return pl.pallas_call(
      _mm_body,
      grid=(t // tm, n // tn),
      in_specs=[pl.BlockSpec((tm, kdim), lambda i, j: (i, 0)), pl.BlockSpec((kdim, tn), lambda i, j: (0, j))],
      out_specs=pl.BlockSpec((tm, tn), lambda i, j: (i, j)),
      out_shape=jax.ShapeDtypeStruct((t, n), F32),
      compiler_params=_params(("parallel", "arbitrary")),
      name="peer_query",
  )(x, w)


def _top16_rows(s, val_ref, idx_ref, lane0):
  n, w = s.shape
  sub = lax.broadcasted_iota(jnp.int32, s.shape, 0).astype(F32)
  for k in range(PEER_TOPK):
    m = jnp.max(s, axis=0, keepdims=True)
    idx = jnp.min(jnp.where(s == m, sub, float(n)), axis=0, keepdims=True)
    val_ref[k:k + 1, lane0:lane0 + w] = m
    idx_ref[k:k + 1, lane0:lane0 + w] = idx
    s = jnp.where(sub == idx, NEG_INF, s)


def _odd_even_merge_sort_pairs(n):
  pairs = []
  p = 1
  while p < n:
    k = p
    while k >= 1:
      for j in range(k % p, n - k, 2 * k):
        for i in range(min(k, n - j - k)):
          if (i + j) // (2 * p) == (i + j + k) // (2 * p):
            pairs.append((i + j, i + j + k))
      k //= 2
    p *= 2
  return pairs


def _top16_distinct(s, val_ref, idx_ref, lane0):
  n, w = s.shape
  depth = n // SUBLANES
  assert depth == PEER_TOPK
  sub = lax.broadcasted_iota(jnp.int32, (SUBLANES, w), 0).astype(F32)
  col = [s[g * SUBLANES:(g + 1) * SUBLANES, :] for g in range(depth)]
  cid = [sub + float(g * SUBLANES) for g in range(depth)]
  for a, b in _odd_even_merge_sort_pairs(depth):
    up = col[b] > col[a]
    col[a], col[b] = jnp.where(up, col[b], col[a]), jnp.where(up, col[a], col[b])
    cid[a], cid[b] = jnp.where(up, cid[b], cid[a]), jnp.where(up, cid[a], cid[b])
  vals = []
  for t in range(PEER_TOPK):
    m = jnp.max(col[0], axis=0, keepdims=True)
    hit = col[0] == m
    vals.append(m)
    val_ref[t:t + 1, lane0:lane0 + w] = m
    idx_ref[t:t + 1, lane0:lane0 + w] = jnp.max(jnp.where(hit, cid[0], -1.0), axis=0, keepdims=True)
    for k in range(depth - 1 - t):
      col[k] = jnp.where(hit, col[k + 1], col[k])
      cid[k] = jnp.where(hit, cid[k + 1], cid[k])
  tied = jnp.zeros((1, w), F32)
  for t in range(PEER_TOPK - 1):
    tied = jnp.where(vals[t] == vals[t + 1], 1.0, tied)
  at_least_last = jnp.sum(jnp.where(s >= vals[-1], 1.0, 0.0), axis=0, keepdims=True)
  return jnp.where(at_least_last > float(PEER_TOPK), 1.0, tied)


PEER_HEADS_PER_ITER = 4


def _topk_body(q_ref, k1_ref, k2_ref, eid_ref, gate_ref, val_ref, idx_ref, best_ref, sel_ref, eid_s, gate_s):
  tb = q_ref.shape[0]
  kk = PEER_TOPK
  half_w = N_KEYS
  hp = PEER_HEADS_PER_ITER
  wide = hp * tb

  def head_group(hg, carry):
    def scores(hh, c):
      col = pl.multiple_of((2 * (hg * hp + hh) + c) * half_w, half_w)
      qh = q_ref[:, pl.ds(col, half_w)]
      return lax.dot_general((k1_ref, k2_ref)[c][...], qh, NT_DIMS, preferred_element_type=F32,
                             precision=lax.Precision.HIGHEST)

    halves = [(hh, c) for hh in range(hp) for c in range(2)]
    tied = [_top16_distinct(scores(hh, c), val_ref, idx_ref, (2 * hh + c) * tb) for hh, c in halves]

    @pl.when(jnp.max(jnp.concatenate(tied, axis=1)) > 0.0)
    def _():
      for hh, c in halves:
        _top16_rows(scores(hh, c), val_ref, idx_ref, (2 * hh + c) * tb)

    def pick(ref, c):
      return jnp.concatenate([ref[:, (2 * hh + c) * tb:(2 * hh + c + 1) * tb] for hh in range(hp)], axis=1)

    v1, v2 = pick(val_ref, 0), pick(val_ref, 1)
    i1, i2 = pick(idx_ref, 0), pick(idx_ref, 1)
    b16 = lax.broadcasted_iota(jnp.int32, (kk, wide), 0).astype(F32)
    b8 = lax.broadcasted_iota(jnp.int32, (SUBLANES, wide), 0).astype(F32)
    vals = [v1[0:1] + v2]
    flat = [b16]
    code = [i1[0:1] * N_KEYS + i2]
    for a in range(1, SUBLANES):
      vals.append(v1[a:a + 1] + v2[0:SUBLANES])
      flat.append(a * kk + b8)
      code.append(i1[a:a + 1] * N_KEYS + i2[0:SUBLANES])
    vals.append(v1[SUBLANES:kk] + v2[0:1])
    flat.append((b8 + SUBLANES) * kk)
    code.append(i1[SUBLANES:kk] * N_KEYS + i2[0:1])
    cand = jnp.concatenate(vals, axis=0)
    flat = jnp.concatenate(flat, axis=0)
    code = jnp.concatenate(code, axis=0)

    first = b8 == 0.0
    lists, list_code = [], []
    for b in range(kk):
      t = v1[0:SUBLANES] + v2[b:b + 1]
      lists.append(t if b < SUBLANES else jnp.where(first, t, NEG_INF))
      list_code.append(i1[0:SUBLANES] * N_KEYS + i2[b:b + 1])
    single = v1[SUBLANES:kk] + v2[0:1]
    single_code = i1[SUBLANES:kk] * N_KEYS + i2[0:1]
    popped = []
    for k in range(kk):
      m = jnp.max(jnp.maximum(lists[0], single), axis=0, keepdims=True)
      hit_l, hit_s = lists[0] == m, single == m
      popped.append(m)
      best_ref[k:k + 1, :] = m
      sel_ref[k:k + 1, :] = jnp.max(jnp.maximum(jnp.where(hit_l, list_code[0], -1.0),
                                                jnp.where(hit_s, single_code, -1.0)), axis=0, keepdims=True)
      for b in range(kk - 1 - k):
        lists[b] = jnp.where(hit_l, lists[b + 1], lists[b])
        list_code[b] = jnp.where(hit_l, list_code[b + 1], list_code[b])
      single = jnp.where(hit_s, NEG_INF, single)
    tied = jnp.sum(jnp.where(cand >= popped[-1], 1.0, 0.0), axis=0, keepdims=True) > float(kk)
    for k in range(kk - 1):
      tied = jnp.logical_or(tied, popped[k] == popped[k + 1])

    @pl.when(jnp.max(jnp.where(tied, 1.0, 0.0)) > 0.0)
    def _():
      left = cand
      for k in range(kk):
        m = jnp.max(left, axis=0, keepdims=True)
        fsel = jnp.min(jnp.where(left == m, flat, float(kk * kk)), axis=0, keepdims=True)
        hit = flat == fsel
        best_ref[k:k + 1, :] = m
        sel_ref[k:k + 1, :] = jnp.max(jnp.where(hit, code, -1.0), axis=0, keepdims=True)
        left = jnp.where(hit, NEG_INF, left)

    best = best_ref[...]
    e = jnp.exp(best - best[0:1])
    gate = e / jnp.sum(e, axis=0, keepdims=True)
    eid = sel_ref[...].astype(jnp.int32)
    for hh in range(hp):
      row = pl.multiple_of((hg * hp + hh) * kk, kk)
      gate_s[pl.ds(row, kk), :] = gate[:, hh * tb:(hh + 1) * tb]
      eid_s[pl.ds(row, kk), :] = eid[:, hh * tb:(hh + 1) * tb]
    return carry

  lax.fori_loop(0, PEER_HEADS // hp, head_group, 0)
  gate_ref[...] = gate_s[...].T
  eid_ref[...] = eid_s[...].T


def _peer_topk(qp, k1, k2):
  t = qp.shape[0]
  tb = LANES
  return pl.pallas_call(
      _topk_body,
      grid=(t // tb,),
      in_specs=[
          pl.BlockSpec((tb, qp.shape[1]), lambda i: (i, 0)),
          pl.BlockSpec((N_KEYS, N_KEYS), lambda i: (0, 0)),
          pl.BlockSpec((N_KEYS, N_KEYS), lambda i: (0, 0)),
      ],
      out_specs=[
          pl.BlockSpec((tb, PEER_SEL), lambda i: (i, 0)),
          pl.BlockSpec((tb, PEER_SEL), lambda i: (i, 0)),
      ],
      out_shape=[
          jax.ShapeDtypeStruct((t, PEER_SEL), jnp.int32),
          jax.ShapeDtypeStruct((t, PEER_SEL), F32),
      ],
      scratch_shapes=[
          pltpu.VMEM((PEER_TOPK, 2 * PEER_HEADS_PER_ITER * tb), F32),
          pltpu.VMEM((PEER_TOPK, 2 * PEER_HEADS_PER_ITER * tb), F32),
          pltpu.VMEM((PEER_TOPK, PEER_HEADS_PER_ITER * tb), F32),
          pltpu.VMEM((PEER_TOPK, PEER_HEADS_PER_ITER * tb), F32),
          pltpu.VMEM((PEER_SEL, tb), jnp.int32),
          pltpu.VMEM((PEER_SEL, tb), F32),
      ],
      compiler_params=_params(("parallel",)),
      name="peer_topk",
  )(qp, k1, k2)


def _gate_matrix_body(eid_ref, gate_ref, g_ref):
  tb = eid_ref.shape[0]
  sub = lax.broadcasted_iota(jnp.int32, (N_KEYS, PEER_SEL), 0)

  def token_group(gi, carry):
    tiles = []
    for tau in range(SUBLANES):
      t = gi * SUBLANES + tau
      e = eid_ref[pl.ds(t, 1), :]
      gt = gate_ref[pl.ds(t, 1), :]
      at = jnp.where(sub == lax.shift_right_logical(e, KEY_BITS), gt, 0.0).astype(BF16)
      bt = jnp.where(sub == jnp.bitwise_and(e, N_KEYS - 1), 1.0, 0.0).astype(BF16)
      tiles.append(lax.dot_general(at, bt, NT_DIMS, preferred_element_type=F32))
    blocks = jnp.stack([jnp.stack([tile[h * SUBLANES:(h + 1) * SUBLANES, :] for tile in tiles])
                        for h in range(N_KEYS // SUBLANES)])
    g_ref[gi] = jnp.swapaxes(blocks, 1, 2).reshape(N_KEYS, SUBLANES, N_KEYS)
    return carry

  lax.fori_loop(0, tb // SUBLANES, token_group, 0, unroll=8)


def _gate_matrix(eid_t, gate_t):
  t = eid_t.shape[0]
  tb = _pick_block(t, 64, SUBLANES)
  return pl.pallas_call(
      _gate_matrix_body,
      grid=(t // tb,),
      in_specs=[pl.BlockSpec((tb, PEER_SEL), lambda i: (i, 0)), pl.BlockSpec((tb, PEER_SEL), lambda i: (i, 0))],
      out_specs=pl.BlockSpec((tb // SUBLANES, N_KEYS, SUBLANES, N_KEYS), lambda i: (i, 0, 0, 0)),
      out_shape=jax.ShapeDtypeStruct((t // SUBLANES, N_KEYS, SUBLANES, N_KEYS), F32),
      compiler_params=_params(("parallel",)),
      name="peer_gate_matrix",
  )(eid_t, gate_t)


PEER_I1_PER_STEP = 8


def _peer_dense_body(x_ref, u_ref, v_ref, g_ref, h_ref, fg_ref, y_ref, ys_ref, *, rem):
  c = pl.program_id(1)
  s = lax.dot_general(x_ref[...], u_ref[...], NT_DIMS, preferred_element_type=F32)
  act = 0.5 * s * (1.0 + lax.erf(s * (2.0 ** -0.5)))
  tm = x_ref.shape[0]
  coef = jnp.concatenate(
      [g_ref[:, j].reshape(tm, N_KEYS) * act[:, j * N_KEYS:(j + 1) * N_KEYS] for j in range(PEER_I1_PER_STEP)],
      axis=1)
  contrib = jnp.dot(coef.astype(BF16), v_ref[...], preferred_element_type=F32)

  @pl.when(c == 0)
  def _():
    y_ref[...] = contrib

  @pl.when(c > 0)
  def _():
    y_ref[...] += contrib

  @pl.when(c == pl.num_programs(1) - 1)
  def _():
    y = _rms(h_ref[...] + y_ref[...], fg_ref[...])
    y_ref[...] = y

    @pl.when(pl.program_id(0) == pl.num_programs(0) - 1)
    def _():
      ys_ref[...] = y[rem:, :]


def _peer_dense(xn2, u_bf, v_bf, gmat, h, final_g, tp):
  t = xn2.shape[0]
  ts = t - tp
  tm = _pick_block(t, 640, LANES)
  te = PEER_I1_PER_STEP * N_KEYS
  return pl.pallas_call(
      functools.partial(_peer_dense_body, rem=_tail_rows(tp, ts, tm)),
      grid=(t // tm, N_EXPERTS // te),
      in_specs=[
          pl.BlockSpec((tm, D_MODEL), lambda i, c: (i, 0)),
          pl.BlockSpec((te, D_MODEL), lambda i, c: (c, 0)),
          pl.BlockSpec((te, D_MODEL), lambda i, c: (c, 0)),
          pl.BlockSpec((tm // SUBLANES, PEER_I1_PER_STEP, SUBLANES, N_KEYS), lambda i, c: (i, c, 0, 0)),
          pl.BlockSpec((tm, D_MODEL), lambda i, c: (i, 0), pipeline_mode=pl.Buffered(1)),
          pl.BlockSpec((1, D_MODEL), lambda i, c: (0, 0)),
      ],
      out_specs=[_prompt_rows_spec(tm, D_MODEL, tp), _sample_rows_spec(ts, D_MODEL)],
      out_shape=[jax.ShapeDtypeStruct((tp, D_MODEL), F32), jax.ShapeDtypeStruct((ts, D_MODEL), F32)],
      compiler_params=_params(("arbitrary", "arbitrary")),
      name="peer_dense",
  )(xn2, u_bf, v_bf, gmat, h, final_g)


def _rope_tables(pos):
  half = ROT_DIM // 2
  inv_freq = jnp.float32(ROPE_THETA) ** (-jnp.arange(half, dtype=F32) * 2.0 / ROT_DIM)
  ang = pos.astype(F32)[:, None] * inv_freq[None, :]
  cos, sin = jnp.cos(ang), jnp.sin(ang)
  n = pos.shape[0]
  ones = jnp.ones((n, HEAD_DIM - ROT_DIM), F32)
  zeros = jnp.zeros((n, HEAD_DIM - ROT_DIM), F32)
  zh = jnp.zeros((n, half), F32)
  cos_c = jnp.concatenate([cos, cos, ones], axis=1)
  sa_c = jnp.concatenate([-sin, zh, zeros], axis=1)
  sb_c = jnp.concatenate([zh, sin, zeros], axis=1)
  two = lambda a: jnp.concatenate([a, a], axis=1)
  return two(cos_c), two(sa_c), two(sb_c)


def kernel(x_prompt, x_sample, cache_k, cache_v, state_conv, state_h, page_table, norm1_g, w_in, lambda_q1, lambda_k1, lambda_q2, lambda_k2, subln_g, conv_w, conv_b, lru_wa, lru_ba, lru_wx, lru_bx, lru_lambda, w_att_up, w_lru_up, w_out, norm2_g, peer_wq, peer_k1, peer_k2, peer_u, peer_v, final_g):
  batch, seq, _ = x_prompt.shape
  dec_batch, dec_seq, _ = x_sample.shape
  n_pages = page_table.shape[1]
  past_len = n_pages * PAGE_SIZE
  tp = batch * seq
  ts = dec_batch * dec_seq
  assert w_in.shape[0] == 1, "one layer"

  x_p = x_prompt.reshape(tp, D_MODEL)
  x_s = x_sample.reshape(ts, D_MODEL)
  tabs_p = _rope_tables(jnp.arange(seq, dtype=jnp.int32))
  tabs_s = _rope_tables(past_len + jnp.arange(dec_seq, dtype=jnp.int32))
  cos_t, sa_t, sb_t = [jnp.concatenate([jnp.tile(a, (batch, 1)), jnp.tile(b, (dec_batch, 1))], axis=0)
                       for a, b in zip(tabs_p, tabs_s)]
  lam = (jnp.exp(jnp.sum(lambda_q1[0].astype(F32) * lambda_k1[0].astype(F32)))
         - jnp.exp(jnp.sum(lambda_q2[0].astype(F32) * lambda_k2[0].astype(F32))) + LAM_INIT).reshape(1)
  row = lambda a: a.reshape(1, -1)

  proj = _inproj(x_p, x_s, row(norm1_g[0]), w_in[0].astype(BF16), cos_t, sa_t, sb_t)

  att_p, u_bf, v_bf = _prompt_attention(proj, lam, row(subln_g[0]), batch, seq, peer_u[0], peer_v[0])
  proj_s = proj[tp:]
  q_s = proj_s[:, :OFF_Q].reshape(dec_batch, dec_seq, N_KV_HEADS, 2, 2, HEAD_DIM)
  q_s = q_s.transpose(0, 2, 4, 3, 1, 5).reshape(dec_batch, N_KV_HEADS, 2, 2 * dec_seq, HEAD_DIM)
  wq = jnp.einsum("bgcnd,ce->bgcned", q_s, jnp.eye(2, dtype=F32))
  wq = wq.reshape(dec_batch, N_KV_HEADS * 2 * 2 * dec_seq, HEAD_W).astype(BF16)
  new_rows = lambda a: a.reshape(dec_batch, dec_seq * N_KV_HEADS, HEAD_W).astype(BF16)
  k_new, v_new = proj_s[:, OFF_Q:OFF_K], proj_s[:, OFF_K:OFF_V]
  n_pool = cache_k.shape[1]
  att_s = _sample_attention(page_table, lam, wq, new_rows(k_new), new_rows(v_new), row(subln_g[0]),
                            cache_k.reshape(n_pool, PAGE_ROWS, HEAD_W),
                            cache_v.reshape(n_pool, PAGE_ROWS, HEAD_W), dec_seq)
  att_s = att_s.reshape(dec_batch, N_KV_HEADS, 2, dec_seq, HEAD_W).transpose(0, 3, 1, 2, 4)
  att_s = att_s.reshape(ts, ATT_WIDTH).astype(BF16)

  lru_w = (conv_w[0], row(conv_b[0]), lru_wa[0], row(lru_ba[0]), lru_wx[0], row(lru_bx[0]), row(lru_lambda[0]))
  lru_p, h_p = _lru_prompt(proj, lru_w, batch, seq)
  xl_s = proj_s[:, OFF_V:OFF_L].reshape(dec_batch, dec_seq, LRU_WIDTH)
  lru_s, h_s = _lru_sample(xl_s.transpose(1, 0, 2), state_conv[0].transpose(1, 0, 2), state_h[0], lru_w)
  lru_s = lru_s.transpose(1, 0, 2).reshape(ts, LRU_WIDTH)

  h_all, xn2 = _merge(att_p, att_s, lru_p, lru_s, proj, x_p, x_s, w_att_up[0].astype(BF16),
                      w_lru_up[0].astype(BF16), w_out[0].astype(BF16), row(norm2_g[0]))
  qp = _matmul(xn2, peer_wq[0].astype(BF16))
  eid_t, gate_t = _peer_topk(qp, peer_k1[0], peer_k2[0])
  gmat = _gate_matrix(eid_t, gate_t)
  y_p, y_s = _peer_dense(xn2, u_bf, v_bf, gmat, h_all, row(final_g), tp)

  kv_shape_p = (1, batch, seq, N_KV_HEADS, HEAD_W)
  kv_shape_s = (1, dec_batch, dec_seq, N_KV_HEADS, HEAD_W)
  tail = CONV_W - 1
  conv_p = jnp.stack([proj[(b + 1) * seq - tail:(b + 1) * seq, OFF_V:OFF_L] for b in range(batch)])
  conv_s = jnp.concatenate([state_conv[0].astype(F32), xl_s], axis=1)[:, -tail:]
  return (
      y_p.reshape(batch, seq, D_MODEL),
      y_s.reshape(dec_batch, dec_seq, D_MODEL),
      proj[:tp, OFF_Q:OFF_K].reshape(kv_shape_p),
      proj[:tp, OFF_K:OFF_V].reshape(kv_shape_p),
      conv_p[None],
      h_p.reshape(1, batch, LRU_WIDTH),
      k_new.reshape(kv_shape_s),
      v_new.reshape(kv_shape_s),
      conv_s[None],
      h_s[None],
  )
```

```python
import functools
import math

import jax
import jax.numpy as jnp
from jax import lax
from jax.experimental import pallas as pl
from jax.experimental.pallas import tpu as pltpu

F32 = jnp.float32
BF16 = jnp.bfloat16

D_MODEL = 2048
N_HEADS = 8
N_KV_HEADS = 4
HEAD_DIM = 64
HEAD_W = 2 * HEAD_DIM
ROT_DIM = HEAD_DIM // 4
ROPE_THETA = 500000.0
ATT_WIDTH = N_HEADS * HEAD_W
KV_WIDTH = N_KV_HEADS * HEAD_W
LRU_WIDTH = 1024
LRU_BLOCKS = 8
LRU_BLOCK_W = LRU_WIDTH // LRU_BLOCKS
CONV_W = 4
LRU_C = 8.0
N_KEYS = 128
KEY_BITS = 7
N_EXPERTS = N_KEYS * N_KEYS
PEER_HEADS = 8
PEER_TOPK = 16
PEER_SEL = PEER_HEADS * PEER_TOPK
PAGE_SIZE = 128
RMS_EPS = 1e-6
OFF_Q = ATT_WIDTH
OFF_K = OFF_Q + KV_WIDTH
OFF_V = OFF_K + KV_WIDTH
OFF_L = OFF_V + LRU_WIDTH
OFF_GA = OFF_L + D_MODEL
IN_WIDTH = OFF_GA + D_MODEL
LAM_INIT = 0.8 - 0.6 * math.exp(0.0)

LANES = 128
SUBLANES = 8
VMEM_LIMIT_BYTES = 56 * 1024 * 1024

NEG_INF = float("-inf")
LOG2_E = math.log2(math.e)
NT_DIMS = (((1,), (1,)), ((), ()))


def _pick_block(total, cap, quantum):
  best = None
  b = quantum
  while b <= min(cap, total):
    if total % b == 0:
      best = b
    b += quantum
  assert best is not None, (total, cap, quantum)
  return best


def _params(sem, vmem=VMEM_LIMIT_BYTES):
  return pltpu.CompilerParams(dimension_semantics=sem, vmem_limit_bytes=vmem)


def _rms(x, g):
  var = jnp.mean(x * x, axis=-1, keepdims=True)
  return x * lax.rsqrt(var + RMS_EPS) * g


def _tail_rows(tp, ts, tm):
  assert (tp + ts) % tm == 0 and 0 < ts < tm, (tp, ts, tm)
  return tm - ts


def _prompt_rows_spec(tm, width, tp):
  last_prompt_block = pl.cdiv(tp, tm) - 1
  return pl.BlockSpec((tm, width), lambda i, *_: (jnp.minimum(i, last_prompt_block), 0))


def _sample_rows_spec(ts, width):
  return pl.BlockSpec((ts, width), lambda i, *_: (0, 0))


def _stacked_rows(i, n_blocks, rem, p_ref, s_ref, emit):
  tm = rem + s_ref.shape[0]

  @pl.when(i < n_blocks - 1)
  def _():
    emit(slice(0, tm), p_ref[...])

  @pl.when(i == n_blocks - 1)
  def _():
    emit(slice(0, rem), p_ref[0:rem, :])
    emit(slice(rem, tm), s_ref[...])


def _inproj_body(xp_ref, xs_ref, g_ref, w_ref, cos_ref, sa_ref, sb_ref, o_ref, xn_ref, *, tn, n_rope_blocks, rem):
  i = pl.program_id(0)
  j = pl.program_id(1)

  @pl.when(j == 0)
  def _():
    def emit(rows, x):
      xn_ref[rows, :] = _rms(x, g_ref[...]).astype(BF16)
    _stacked_rows(i, pl.num_programs(0), rem, xp_ref, xs_ref, emit)

  acc = jnp.dot(xn_ref[...], w_ref[...], preferred_element_type=F32)

  @pl.when(j < n_rope_blocks)
  def _():
    reps = tn // LANES
    cos = jnp.concatenate([cos_ref[...]] * reps, axis=1)
    sa = jnp.concatenate([sa_ref[...]] * reps, axis=1)
    sb = jnp.concatenate([sb_ref[...]] * reps, axis=1)
    half = ROT_DIM // 2
    rot = acc * cos + pltpu.roll(acc, tn - half, 1) * sa + pltpu.roll(acc, half, 1) * sb
    col = j * tn + lax.broadcasted_iota(jnp.int32, acc.shape, 1)
    o_ref[...] = jnp.where(col < OFF_Q, rot * (HEAD_DIM ** -0.5), jnp.where(col < OFF_K, rot, acc))

  @pl.when(j >= n_rope_blocks)
  def _():
    o_ref[...] = acc


def _inproj(x_p, x_s, norm_g, w_in_bf, cos_t, sa_t, sb_t):
  tp, ts = x_p.shape[0], x_s.shape[0]
  t = tp + ts
  tm = _pick_block(t, 832, 64)
  tn = 1024
  body = functools.partial(_inproj_body, tn=tn, n_rope_blocks=pl.cdiv(OFF_K, tn), rem=_tail_rows(tp, ts, tm))
  return pl.pallas_call(
      body,
      grid=(t // tm, IN_WIDTH // tn),
      in_specs=[
          _prompt_rows_spec(tm, D_MODEL, tp),
          _sample_rows_spec(ts, D_MODEL),
          pl.BlockSpec((1, D_MODEL), lambda i, j: (0, 0)),
          pl.BlockSpec((D_MODEL, tn), lambda i, j: (0, j)),
          pl.BlockSpec((tm, LANES), lambda i, j: (i, 0)),
          pl.BlockSpec((tm, LANES), lambda i, j: (i, 0)),
          pl.BlockSpec((tm, LANES), lambda i, j: (i, 0)),
      ],
      out_specs=pl.BlockSpec((tm, tn), lambda i, j: (i, j)),
      out_shape=jax.ShapeDtypeStruct((t, IN_WIDTH), F32),
      scratch_shapes=[pltpu.VMEM((tm, D_MODEL), BF16)],
      compiler_params=_params(("parallel", "arbitrary")),
      name="inproj",
  )(x_p, x_s, norm_g, w_in_bf, cos_t, sa_t, sb_t)


def _subln(o, g):
  var = jnp.mean(o * o, axis=-1, keepdims=True)
  return o * lax.rsqrt(var + RMS_EPS) * g * (1.0 - LAM_INIT)


def _pattn_body(qi_ref, ki_ref, lam_ref, q_ref, k_ref, v_ref, g_ref, tu_ref, tv_ref, o_ref, tub_ref, tvb_ref,
                q4_ref, m_ref, l_ref, acc_ref, *, tq, tk, cast_steps):
  qi = qi_ref[pl.program_id(2)]
  ki = ki_ref[pl.program_id(2)]
  cols = 2 * 2 * tq

  @pl.when(pl.program_id(2) < cast_steps)
  def _():
    tub_ref[...] = tu_ref[...].astype(BF16)
    tvb_ref[...] = tv_ref[...].astype(BF16)

  @pl.when(ki == 0)
  def _():
    m_ref[...] = jnp.full((1, cols), NEG_INF, F32)
    l_ref[...] = jnp.zeros((1, cols), F32)
    acc_ref[...] = jnp.zeros((HEAD_W, cols), F32)
    lane = lax.broadcasted_iota(jnp.int32, (tq, HEAD_W), 1)
    for r in range(2):
      qh = q_ref[:, r * HEAD_W:(r + 1) * HEAD_W] * LOG2_E
      q4_ref[(2 * r) * tq:(2 * r + 1) * tq, :] = jnp.where(lane < HEAD_DIM, qh, 0.0).astype(BF16)
      q4_ref[(2 * r + 1) * tq:(2 * r + 2) * tq, :] = jnp.where(lane >= HEAD_DIM, qh, 0.0).astype(BF16)

  def update(diagonal):
    k = k_ref[...].astype(BF16)
    s = lax.dot_general(k, q4_ref[...], NT_DIMS, preferred_element_type=F32)
    if diagonal:
      visible = (lax.broadcasted_iota(jnp.int32, (tk, tq), 0) <= lax.broadcasted_iota(jnp.int32, (tk, tq), 1))
      s = jnp.concatenate([jnp.where(visible, s[:, j * tq:(j + 1) * tq], NEG_INF) for j in range(4)], axis=1)
    m_old = m_ref[...]
    m_new = jnp.maximum(m_old, jnp.max(s, axis=0, keepdims=True))
    alpha = jnp.exp2(m_old - m_new)
    p = jnp.exp2(s - m_new)
    l_ref[...] = alpha * l_ref[...] + jnp.sum(p, axis=0, keepdims=True)
    vt = v_ref[...].T.astype(BF16)
    acc_ref[...] = alpha * acc_ref[...] + jnp.dot(vt, p.astype(BF16), preferred_element_type=F32)
    m_ref[...] = m_new

  @pl.when(ki < qi)
  def _():
    update(False)

  @pl.when(ki == qi)
  def _():
    update(True)
    lam = lam_ref[0]
    o = acc_ref[...] * (1.0 / l_ref[...])
    for r in range(2):
      d = o[:, (2 * r) * tq:(2 * r + 1) * tq] - lam * o[:, (2 * r + 1) * tq:(2 * r + 2) * tq]
      var = jnp.mean(d * d, axis=0, keepdims=True)
      dn = d * (lax.rsqrt(var + RMS_EPS) * (1.0 - LAM_INIT))
      o_ref[:, r * HEAD_W:(r + 1) * HEAD_W] = (dn.T * g_ref[...]).astype(o_ref.dtype)


def _prompt_attention(proj, lam, subln_g, batch, seq, table_u, table_v):
  tq = tk = _pick_block(seq, 512, LANES)
  nq = seq // tq
  gw = 2 * HEAD_W
  pairs = [(qi, ki) for qi in range(nq) for ki in range(qi + 1)]
  qi_of = jnp.asarray([p[0] for p in pairs], jnp.int32)
  ki_of = jnp.asarray([p[1] for p in pairs], jnp.int32)
  n_groups = batch * N_KV_HEADS
  cast_steps = 1 << (len(pairs).bit_length() - 1)
  cast_rows = N_EXPERTS // (n_groups * cast_steps)
  assert cast_rows * n_groups * cast_steps == N_EXPERTS and cast_rows % (2 * SUBLANES) == 0
  table_spec = pl.BlockSpec(
      (cast_rows, D_MODEL),
      lambda b, g, p, qo, ko: ((b * N_KV_HEADS + g) * cast_steps + jnp.minimum(p, cast_steps - 1), 0))
  body = functools.partial(_pattn_body, tq=tq, tk=tk, cast_steps=cast_steps)
  grid_spec = pltpu.PrefetchScalarGridSpec(
      num_scalar_prefetch=2,
      grid=(batch, N_KV_HEADS, len(pairs)),
      in_specs=[
          pl.BlockSpec(memory_space=pltpu.SMEM),
          pl.BlockSpec((tq, gw), lambda b, g, p, qo, ko: (b * nq + qo[p], g)),
          pl.BlockSpec((tk, HEAD_W), lambda b, g, p, qo, ko: (b * nq + ko[p], OFF_Q // HEAD_W + g)),
          pl.BlockSpec((tk, HEAD_W), lambda b, g, p, qo, ko: (b * nq + ko[p], OFF_K // HEAD_W + g)),
          pl.BlockSpec((1, HEAD_W), lambda b, g, p, qo, ko: (0, 0)),
          table_spec, table_spec,
      ],
      out_specs=[pl.BlockSpec((tq, gw), lambda b, g, p, qo, ko: (b * nq + qo[p], g)), table_spec, table_spec],
      scratch_shapes=[
          pltpu.VMEM((4 * tq, HEAD_W), BF16),
          pltpu.VMEM((1, 4 * tq), F32),
          pltpu.VMEM((1, 4 * tq), F32),
          pltpu.VMEM((HEAD_W, 4 * tq), F32),
      ],
  )
  return pl.pallas_call(
      body,
      grid_spec=grid_spec,
      out_shape=[jax.ShapeDtypeStruct((batch * seq, ATT_WIDTH), BF16),
                 jax.ShapeDtypeStruct(table_u.shape, BF16), jax.ShapeDtypeStruct(table_v.shape, BF16)],
      compiler_params=_params(("parallel", "parallel", "arbitrary")),
      name="prompt_attention",
  )(qi_of, ki_of, lam, proj, proj, proj, subln_g, table_u, table_v)


PAGES_PER_CHUNK = 8
SATTN_SLOTS = 3
PAGE_ROWS = PAGE_SIZE * N_KV_HEADS


def _sattn_body(pt_ref, lam_ref, wq_ref, kn_ref, vn_ref, g_ref, ck_ref, cv_ref, o_ref,
                kbuf, vbuf, bias_ref, sem, *, n_chunks, dec_seq):
  b = pl.program_id(0)
  nb = pl.num_programs(0)
  rows = wq_ref.shape[1]
  gr = rows // N_KV_HEADS
  cols = PAGES_PER_CHUNK * PAGE_ROWS

  def copies(bb, c, slot):
    out = []
    for p in range(PAGES_PER_CHUNK):
      page = pt_ref[bb, c * PAGES_PER_CHUNK + p]
      out.append(pltpu.make_async_copy(ck_ref.at[page], kbuf.at[slot, p], sem.at[0, slot]))
      out.append(pltpu.make_async_copy(cv_ref.at[page], vbuf.at[slot, p], sem.at[1, slot]))
    return out

  def start(bb, c, slot):
    for n, cp in enumerate(copies(bb, c, slot)):
      cp.start(priority=n % 2)

  def same_head(shape):
    head_of_col = jnp.bitwise_and(lax.broadcasted_iota(jnp.int32, shape, 1), N_KV_HEADS - 1)
    head_of_row = lax.broadcasted_iota(jnp.int32, shape, 0) // gr
    return head_of_col == head_of_row

  ahead = SATTN_SLOTS - 1

  def slot_of(bb, c):
    return lax.rem(bb * n_chunks + c, SATTN_SLOTS)

  @pl.when(b == 0)
  def _():
    for c in range(ahead):
      start(0, c, c)
    bias_ref[...] = jnp.where(same_head((rows, cols)), 0.0, NEG_INF)

  wq = wq_ref[0]

  def softmax_step(carry, s, v):
    m_old, l_old, acc = carry
    m_new = jnp.maximum(m_old, jnp.max(s, axis=1, keepdims=True))
    alpha = jnp.exp(m_old - m_new)
    p = jnp.exp(s - m_new)
    l_new = alpha * l_old + jnp.sum(p, axis=1, keepdims=True)
    acc = alpha * acc + jnp.dot(p.astype(BF16), v, preferred_element_type=F32)
    return m_new, l_new, acc

  def chunk(c, carry):
    slot = slot_of(b, c)

    @pl.when(c + ahead < n_chunks)
    def _():
      start(b, c + ahead, slot_of(b, c + ahead))

    @pl.when(jnp.logical_and(c + ahead >= n_chunks, b + 1 < nb))
    def _():
      start(b + 1, c + ahead - n_chunks, slot_of(b + 1, c + ahead - n_chunks))

    for cp in copies(b, c, slot):
      cp.wait()
    kc = kbuf[slot].reshape(cols, HEAD_W).astype(BF16)
    vc = vbuf[slot].reshape(cols, HEAD_W).astype(BF16)
    s = lax.dot_general(wq, kc, NT_DIMS, preferred_element_type=F32) + bias_ref[...]
    return softmax_step(carry, s, vc)

  init = (jnp.full((rows, 1), NEG_INF, F32), jnp.zeros((rows, 1), F32), jnp.zeros((rows, HEAD_W), F32))
  carry = lax.fori_loop(0, n_chunks, chunk, init)

  new_rows = kn_ref.shape[1]
  s = lax.dot_general(wq, kn_ref[0], NT_DIMS, preferred_element_type=F32)
  t_of_row = lax.broadcasted_iota(jnp.int32, (rows, new_rows), 0) % dec_seq
  t_of_col = lax.broadcasted_iota(jnp.int32, (rows, new_rows), 1) // N_KV_HEADS
  visible = jnp.logical_and(same_head((rows, new_rows)), t_of_col <= t_of_row)
  _, l_fin, acc = softmax_step(carry, jnp.where(visible, s, NEG_INF), vn_ref[0])

  o = acc / l_fin
  lam = lam_ref[0]
  for g in range(N_KV_HEADS):
    blk = o[g * gr:(g + 1) * gr]
    d = blk[:gr // 2] - lam * blk[gr // 2:]
    o_ref[0, g * (gr // 2):(g + 1) * (gr // 2), :] = _subln(d, g_ref[...])


def _sample_attention(page_table, lam, wq, k_new, v_new, subln_g, cache_k, cache_v, dec_seq):
  dec_batch, n_pages = page_table.shape
  assert n_pages % PAGES_PER_CHUNK == 0
  n_chunks = n_pages // PAGES_PER_CHUNK
  assert n_chunks >= SATTN_SLOTS - 1
  rows = wq.shape[1]
  new_rows = k_new.shape[1]
  body = functools.partial(_sattn_body, n_chunks=n_chunks, dec_seq=dec_seq)
  grid_spec = pltpu.PrefetchScalarGridSpec(
      num_scalar_prefetch=1,
      grid=(dec_batch,),
      in_specs=[
          pl.BlockSpec(memory_space=pltpu.SMEM),
          pl.BlockSpec((1, rows, HEAD_W), lambda b, pt: (b, 0, 0)),
          pl.BlockSpec((1, new_rows, HEAD_W), lambda b, pt: (b, 0, 0)),
          pl.BlockSpec((1, new_rows, HEAD_W), lambda b, pt: (b, 0, 0)),
          pl.BlockSpec((1, HEAD_W), lambda b, pt: (0, 0)),
          pl.BlockSpec(memory_space=pl.ANY),
          pl.BlockSpec(memory_space=pl.ANY),
      ],
      out_specs=pl.BlockSpec((1, rows // 2, HEAD_W), lambda b, pt: (b, 0, 0)),
      scratch_shapes=[
          pltpu.VMEM((SATTN_SLOTS, PAGES_PER_CHUNK, PAGE_ROWS, HEAD_W), F32),
          pltpu.VMEM((SATTN_SLOTS, PAGES_PER_CHUNK, PAGE_ROWS, HEAD_W), F32),
          pltpu.VMEM((rows, PAGES_PER_CHUNK * PAGE_ROWS), F32),
          pltpu.SemaphoreType.DMA((2, SATTN_SLOTS)),
      ],
  )
  return pl.pallas_call(
      body,
      grid_spec=grid_spec,
      out_shape=jax.ShapeDtypeStruct((dec_batch, rows // 2, HEAD_W), F32),
      compiler_params=_params(("arbitrary",)),
      name="sample_attention",
  )(page_table, lam, wq, k_new, v_new, subln_g, cache_k, cache_v)


def _lru_gates(xc, wa_ref, ba, wx_ref, bx, lam):
  ra, ix = [], []
  for n in range(LRU_BLOCKS):
    xb = xc[:, n * LRU_BLOCK_W:(n + 1) * LRU_BLOCK_W].astype(BF16)
    ra.append(jnp.dot(xb, wa_ref[n].astype(BF16), preferred_element_type=F32))
    ix.append(jnp.dot(xb, wx_ref[n].astype(BF16), preferred_element_type=F32))
  r = jax.nn.sigmoid(jnp.concatenate(ra, axis=1) + ba)
  i = jax.nn.sigmoid(jnp.concatenate(ix, axis=1) + bx)
  neg = -lam
  softplus = jnp.maximum(neg, 0.0) + jnp.log1p(jnp.exp(-jnp.abs(neg)))
  log_a = -LRU_C * r * softplus
  a = jnp.exp(log_a)
  u = jnp.sqrt(1.0 - a * a) * (i * xc)
  return a, u


def _conv(rows_of, cw_ref, cb):
  out = rows_of(0) * cw_ref[0:1, :]
  for j in range(1, CONV_W):
    out = out + rows_of(j) * cw_ref[j:j + 1, :]
  return out + cb


def _lru_prompt_body(xl_ref, cw_ref, cb_ref, wa_ref, ba_ref, wx_ref, bx_ref, lam_ref, y_ref, hl_ref,
                     ext_ref, a_ref, u_ref, hs_ref, h_ref, *, tt):
  ti = pl.program_id(1)
  head = SUBLANES

  @pl.when(ti == 0)
  def _():
    ext_ref[0:head, :] = jnp.zeros((head, LRU_WIDTH), F32)
    h_ref[...] = jnp.zeros((1, LRU_WIDTH), F32)

  ext_ref[head:head + tt, :] = xl_ref[...]
  xc = _conv(lambda j: ext_ref[head - (CONV_W - 1) + j:head - (CONV_W - 1) + j + tt, :], cw_ref, cb_ref[...])
  a, u = _lru_gates(xc, wa_ref, ba_ref[...], wx_ref, bx_ref[...], lam_ref[...])
  a_ref[...] = a
  u_ref[...] = u

  row = lax.broadcasted_iota(jnp.int32, (SUBLANES, LRU_WIDTH), 0)

  def tile_steps(i, h):
    rows = pl.ds(pl.multiple_of(i * SUBLANES, SUBLANES), SUBLANES)
    a_cum, u_cum = a_ref[rows, :], u_ref[rows, :]
    d = 1
    while d < SUBLANES:
      keep = row >= d
      u_cum = jnp.where(keep, a_cum * pltpu.roll(u_cum, d, 0) + u_cum, u_cum)
      a_cum = jnp.where(keep, a_cum * pltpu.roll(a_cum, d, 0), a_cum)
      d *= 2
    hs = a_cum * h + u_cum
    hs_ref[rows, :] = hs
    return hs[SUBLANES - 1:SUBLANES, :]

  h_fin = lax.fori_loop(0, tt // SUBLANES, tile_steps, h_ref[...], unroll=4)
  h_ref[...] = h_fin
  y_ref[...] = hs_ref[...].astype(y_ref.dtype)
  ext_ref[0:head, :] = ext_ref[tt:tt + head, :]

  @pl.when(ti == pl.num_programs(1) - 1)
  def _():
    hl_ref[0] = h_fin


def _lru_weight_specs():
  zero2 = (lambda *a: (0, 0))
  zero3 = (lambda *a: (0, 0, 0))
  return [
      pl.BlockSpec((CONV_W, LRU_WIDTH), zero2),
      pl.BlockSpec((1, LRU_WIDTH), zero2),
      pl.BlockSpec((LRU_BLOCKS, LRU_BLOCK_W, LRU_BLOCK_W), zero3),
      pl.BlockSpec((1, LRU_WIDTH), zero2),
      pl.BlockSpec((LRU_BLOCKS, LRU_BLOCK_W, LRU_BLOCK_W), zero3),
      pl.BlockSpec((1, LRU_WIDTH), zero2),
      pl.BlockSpec((1, LRU_WIDTH), zero2),
  ]


def _lru_prompt(proj, lru_w, batch, seq):
  tt = _pick_block(seq, 512, LANES)
  nt = seq // tt
  body = functools.partial(_lru_prompt_body, tt=tt)
  return pl.pallas_call(
      body,
      grid=(batch, nt),
      in_specs=[pl.BlockSpec((tt, LRU_WIDTH), lambda b, ti: (b * nt + ti, OFF_V // LRU_WIDTH))]
      + _lru_weight_specs(),
      out_specs=[
          pl.BlockSpec((tt, LRU_WIDTH), lambda b, ti: (b * nt + ti, 0)),
          pl.BlockSpec((1, 1, LRU_WIDTH), lambda b, ti: (b, 0, 0)),
      ],
      out_shape=[
          jax.ShapeDtypeStruct((batch * seq, LRU_WIDTH), BF16),
          jax.ShapeDtypeStruct((batch, 1, LRU_WIDTH), F32),
      ],
      scratch_shapes=[
          pltpu.VMEM((tt + 2 * SUBLANES, LRU_WIDTH), F32),
          pltpu.VMEM((tt, LRU_WIDTH), F32),
          pltpu.VMEM((tt, LRU_WIDTH), F32),
          pltpu.VMEM((tt, LRU_WIDTH), F32),
          pltpu.VMEM((1, LRU_WIDTH), F32),
      ],
      compiler_params=_params(("parallel", "arbitrary")),
      name="lru_prompt",
  )(proj, *lru_w)


def _lru_sample_body(xl_ref, cbuf_ref, h0_ref, cw_ref, cb_ref, wa_ref, ba_ref, wx_ref, bx_ref, lam_ref,
                     y_ref, hl_ref, *, dec_seq):
  ext = [cbuf_ref[j] for j in range(CONV_W - 1)] + [xl_ref[t] for t in range(dec_seq)]
  h = h0_ref[...]
  for t in range(dec_seq):
    xc = _conv(lambda j: ext[t + j], cw_ref, cb_ref[...])
    a, u = _lru_gates(xc, wa_ref, ba_ref[...], wx_ref, bx_ref[...], lam_ref[...])
    h = a * h + u
    y_ref[t] = h.astype(y_ref.dtype)
  hl_ref[...] = h


def _lru_sample(xl_t, cbuf_t, h0, lru_w):
  dec_seq, dec_batch, _ = xl_t.shape
  body = functools.partial(_lru_sample_body, dec_seq=dec_seq)
  return pl.pallas_call(
      body,
      out_shape=[
          jax.ShapeDtypeStruct((dec_seq, dec_batch, LRU_WIDTH), BF16),
          jax.ShapeDtypeStruct((dec_batch, LRU_WIDTH), F32),
      ],
      name="lru_sample",
  )(xl_t, cbuf_t, h0, *lru_w)


def _merge_body(attp_ref, atts_ref, lrup_ref, lrus_ref, ga0_ref, ga1_ref, gl0_ref, gl1_ref, xp_ref, xs_ref,
                wa_ref, wl_ref, wo_ref, g2_ref, h_ref, xn_ref, att_buf, lru_buf, x_buf, *, rem):
  i = pl.program_id(0)
  n = pl.num_programs(0)

  def fill(buf):
    def emit(rows, v):
      buf[rows, :] = v
    return emit

  _stacked_rows(i, n, rem, attp_ref, atts_ref, fill(att_buf))
  _stacked_rows(i, n, rem, lrup_ref, lrus_ref, fill(lru_buf))
  _stacked_rows(i, n, rem, xp_ref, xs_ref, fill(x_buf))
  a1 = jnp.dot(att_buf[...], wa_ref[...], preferred_element_type=F32)
  a2 = jnp.dot(lru_buf[...], wl_ref[...], preferred_element_type=F32)
  ga = jnp.concatenate([ga0_ref[...], ga1_ref[...]], axis=1)
  gl = jnp.concatenate([gl0_ref[...], gl1_ref[...]], axis=1)
  m = jax.nn.sigmoid(ga) * a1 + jax.nn.sigmoid(gl) * a2
  h = x_buf[...] + jnp.dot(m.astype(BF16), wo_ref[...], preferred_element_type=F32)
  h_ref[...] = h
  xn_ref[...] = _rms(h, g2_ref[...]).astype(BF16)


def _merge(att_p, att_s, lru_p, lru_s, proj, x_p, x_s, wa, wl, wo, g2):
  tp, ts = x_p.shape[0], x_s.shape[0]
  t = tp + ts
  tm = _pick_block(t, 320, 64)
  half = D_MODEL // 2
  const = lambda shape: pl.BlockSpec(shape, lambda i: (0, 0), pipeline_mode=pl.Buffered(1))
  gate = lambda blk: pl.BlockSpec((tm, half), lambda i: (i, blk))
  return pl.pallas_call(
      functools.partial(_merge_body, rem=_tail_rows(tp, ts, tm)),
      grid=(t // tm,),
      in_specs=[
          _prompt_rows_spec(tm, ATT_WIDTH, tp), _sample_rows_spec(ts, ATT_WIDTH),
          _prompt_rows_spec(tm, LRU_WIDTH, tp), _sample_rows_spec(ts, LRU_WIDTH),
          gate(OFF_L // half), gate(OFF_L // half + 1), gate(OFF_GA // half), gate(OFF_GA // half + 1),
          _prompt_rows_spec(tm, D_MODEL, tp), _sample_rows_spec(ts, D_MODEL),
          const((ATT_WIDTH, D_MODEL)), const((LRU_WIDTH, D_MODEL)), const((D_MODEL, D_MODEL)),
          const((1, D_MODEL)),
      ],
      scratch_shapes=[
          pltpu.VMEM((tm, ATT_WIDTH), BF16),
          pltpu.VMEM((tm, LRU_WIDTH), BF16),
          pltpu.VMEM((tm, D_MODEL), F32),
      ],
      out_specs=[
          pl.BlockSpec((tm, D_MODEL), lambda i: (i, 0)),
          pl.BlockSpec((tm, D_MODEL), lambda i: (i, 0)),
      ],
      out_shape=[
          jax.ShapeDtypeStruct((t, D_MODEL), F32),
          jax.ShapeDtypeStruct((t, D_MODEL), BF16),
      ],
      compiler_params=_params(("parallel",)),
      name="merge",
  )(att_p, att_s, lru_p, lru_s, proj, proj, proj, proj, x_p, x_s, wa, wl, wo, g2)


def _mm_body(x_ref, w_ref, o_ref):
  o_ref[...] = jnp.dot(x_ref[...], w_ref[...], preferred_element_type=F32)


def _matmul(x, w):
  t, kdim = x.shape
  n = w.shape[1]
  tm = _pick_block(t, 640, LANES)
  tn = 512
  return pl.pallas_call(
      _mm_body,
      grid=(t // tm, n // tn),
      in_specs=[pl.BlockSpec((tm, kdim), lambda i, j: (i, 0)), pl.BlockSpec((kdim, tn), lambda i, j: (0, j))],
      out_specs=pl.BlockSpec((tm, tn), lambda i, j: (i, j)),
      out_shape=jax.ShapeDtypeStruct((t, n), F32),
      compiler_params=_params(("parallel", "arbitrary")),
      name="peer_query",
  )(x, w)


def _top16_rows(s, val_ref, idx_ref, lane0):
  n, w = s.shape
  sub = lax.broadcasted_iota(jnp.int32, s.shape, 0).astype(F32)
  for k in range(PEER_TOPK):
    m = jnp.max(s, axis=0, keepdims=True)
    idx = jnp.min(jnp.where(s == m, sub, float(n)), axis=0, keepdims=True)
    val_ref[k:k + 1, lane0:lane0 + w] = m
    idx_ref[k:k + 1, lane0:lane0 + w] = idx
    s = jnp.where(sub == idx, NEG_INF, s)


def _odd_even_merge_sort_pairs(n):
  pairs = []
  p = 1
  while p < n:
    k = p
    while k >= 1:
      for j in range(k % p, n - k, 2 * k):
        for i in range(min(k, n - j - k)):
          if (i + j) // (2 * p) == (i + j + k) // (2 * p):
            pairs.append((i + j, i + j + k))
      k //= 2
    p *= 2
  return pairs


def _top16_distinct(s, val_ref, idx_ref, lane0):
  n, w = s.shape
  depth = n // SUBLANES
  assert depth == PEER_TOPK
  sub = lax.broadcasted_iota(jnp.int32, (SUBLANES, w), 0).astype(F32)
  col = [s[g * SUBLANES:(g + 1) * SUBLANES, :] for g in range(depth)]
  cid = [sub + float(g * SUBLANES) for g in range(depth)]
  for a, b in _odd_even_merge_sort_pairs(depth):
    up = col[b] > col[a]
    col[a], col[b] = jnp.where(up, col[b], col[a]), jnp.where(up, col[a], col[b])
    cid[a], cid[b] = jnp.where(up, cid[b], cid[a]), jnp.where(up, cid[a], cid[b])
  vals = []
  for t in range(PEER_TOPK):
    m = jnp.max(col[0], axis=0, keepdims=True)
    hit = col[0] == m
    vals.append(m)
    val_ref[t:t + 1, lane0:lane0 + w] = m
    idx_ref[t:t + 1, lane0:lane0 + w] = jnp.max(jnp.where(hit, cid[0], -1.0), axis=0, keepdims=True)
    for k in range(depth - 1 - t):
      col[k] = jnp.where(hit, col[k + 1], col[k])
      cid[k] = jnp.where(hit, cid[k + 1], cid[k])
  tied = jnp.zeros((1, w), F32)
  for t in range(PEER_TOPK - 1):
    tied = jnp.where(vals[t] == vals[t + 1], 1.0, tied)
  at_least_last = jnp.sum(jnp.where(s >= vals[-1], 1.0, 0.0), axis=0, keepdims=True)
  return jnp.where(at_least_last > float(PEER_TOPK), 1.0, tied)


PEER_HEADS_PER_ITER = 4


def _topk_body(xn_ref, wq_ref, k1_ref, k2_ref, eid_ref, gate_ref, val_ref, idx_ref, best_ref, sel_ref, eid_s,
               gate_s, q_s):
  tb = xn_ref.shape[0]
  kk = PEER_TOPK
  half_w = N_KEYS
  hp = PEER_HEADS_PER_ITER
  wide = hp * tb
  n_groups = PEER_HEADS // hp
  gw = hp * 2 * half_w

  def project(g):
    return jnp.dot(xn_ref[...], wq_ref[:, g * gw:(g + 1) * gw], preferred_element_type=F32)

  q_s[0] = project(0)

  def head_group(hg, carry):
    slot = hg % 2
    if hg + 1 < n_groups:
      q_s[1 - slot] = project(hg + 1)

    def scores(hh, c):
      col = (2 * hh + c) * half_w
      qh = q_s[slot, :, col:col + half_w]
      return lax.dot_general((k1_ref, k2_ref)[c][...], qh, NT_DIMS, preferred_element_type=F32,
                             precision=lax.Precision.HIGHEST)

    halves = [(hh, c) for hh in range(hp) for c in range(2)]
    tied = [_top16_distinct(scores(hh, c), val_ref, idx_ref, (2 * hh + c) * tb) for hh, c in halves]

    @pl.when(jnp.max(jnp.concatenate(tied, axis=1)) > 0.0)
    def _():
      for hh, c in halves:
        _top16_rows(scores(hh, c), val_ref, idx_ref, (2 * hh + c) * tb)

    def pick(ref, c):
      return jnp.concatenate([ref[:, (2 * hh + c) * tb:(2 * hh + c + 1) * tb] for hh in range(hp)], axis=1)

    v1, v2 = pick(val_ref, 0), pick(val_ref, 1)
    i1, i2 = pick(idx_ref, 0), pick(idx_ref, 1)
    b16 = lax.broadcasted_iota(jnp.int32, (kk, wide), 0).astype(F32)
    b8 = lax.broadcasted_iota(jnp.int32, (SUBLANES, wide), 0).astype(F32)
    vals = [v1[0:1] + v2]
    flat = [b16]
    code = [i1[0:1] * N_KEYS + i2]
    for a in range(1, SUBLANES):
      vals.append(v1[a:a + 1] + v2[0:SUBLANES])
      flat.append(a * kk + b8)
      code.append(i1[a:a + 1] * N_KEYS + i2[0:SUBLANES])
    vals.append(v1[SUBLANES:kk] + v2[0:1])
    flat.append((b8 + SUBLANES) * kk)
    code.append(i1[SUBLANES:kk] * N_KEYS + i2[0:1])
    cand = jnp.concatenate(vals, axis=0)
    flat = jnp.concatenate(flat, axis=0)
    code = jnp.concatenate(code, axis=0)

    first = b8 == 0.0
    lists, list_code = [], []
    for b in range(kk):
      t = v1[0:SUBLANES] + v2[b:b + 1]
      lists.append(t if b < SUBLANES else jnp.where(first, t, NEG_INF))
      list_code.append(i1[0:SUBLANES] * N_KEYS + i2[b:b + 1])
    single = v1[SUBLANES:kk] + v2[0:1]
    single_code = i1[SUBLANES:kk] * N_KEYS + i2[0:1]
    popped = []
    for k in range(kk):
      m = jnp.max(jnp.maximum(lists[0], single), axis=0, keepdims=True)
      hit_l, hit_s = lists[0] == m, single == m
      popped.append(m)
      best_ref[k:k + 1, :] = m
      sel_ref[k:k + 1, :] = jnp.max(jnp.maximum(jnp.where(hit_l, list_code[0], -1.0),
                                                jnp.where(hit_s, single_code, -1.0)), axis=0, keepdims=True)
      for b in range(kk - 1 - k):
        lists[b] = jnp.where(hit_l, lists[b + 1], lists[b])
        list_code[b] = jnp.where(hit_l, list_code[b + 1], list_code[b])
      single = jnp.where(hit_s, NEG_INF, single)
    tied = jnp.sum(jnp.where(cand >= popped[-1], 1.0, 0.0), axis=0, keepdims=True) > float(kk)
    for k in range(kk - 1):
      tied = jnp.logical_or(tied, popped[k] == popped[k + 1])

    @pl.when(jnp.max(jnp.where(tied, 1.0, 0.0)) > 0.0)
    def _():
      left = cand
      for k in range(kk):
        m = jnp.max(left, axis=0, keepdims=True)
        fsel = jnp.min(jnp.where(left == m, flat, float(kk * kk)), axis=0, keepdims=True)
        hit = flat == fsel
        best_ref[k:k + 1, :] = m
        sel_ref[k:k + 1, :] = jnp.max(jnp.where(hit, code, -1.0), axis=0, keepdims=True)
        left = jnp.where(hit, NEG_INF, left)

    best = best_ref[...]
    e = jnp.exp(best - best[0:1])
    gate = e / jnp.sum(e, axis=0, keepdims=True)
    eid = sel_ref[...].astype(jnp.int32)
    for hh in range(hp):
      row = pl.multiple_of((hg * hp + hh) * kk, kk)
      gate_s[pl.ds(row, kk), :] = gate[:, hh * tb:(hh + 1) * tb]
      eid_s[pl.ds(row, kk), :] = eid[:, hh * tb:(hh + 1) * tb]
    return carry

  for hg in range(n_groups):
    head_group(hg, 0)
  gate_ref[...] = gate_s[...].T
  eid_ref[...] = eid_s[...].T


def _peer_topk(xn2, wq, k1, k2):
  t = xn2.shape[0]
  tb = LANES
  return pl.pallas_call(
      _topk_body,
      grid=(t // tb,),
      in_specs=[
          pl.BlockSpec((tb, D_MODEL), lambda i: (i, 0)),
          pl.BlockSpec(wq.shape, lambda i: (0, 0), pipeline_mode=pl.Buffered(1)),
          pl.BlockSpec((N_KEYS, N_KEYS), lambda i: (0, 0)),
          pl.BlockSpec((N_KEYS, N_KEYS), lambda i: (0, 0)),
      ],
      out_specs=[
          pl.BlockSpec((tb, PEER_SEL), lambda i: (i, 0)),
          pl.BlockSpec((tb, PEER_SEL), lambda i: (i, 0)),
      ],
      out_shape=[
          jax.ShapeDtypeStruct((t, PEER_SEL), jnp.int32),
          jax.ShapeDtypeStruct((t, PEER_SEL), F32),
      ],
      scratch_shapes=[
          pltpu.VMEM((PEER_TOPK, 2 * PEER_HEADS_PER_ITER * tb), F32),
          pltpu.VMEM((PEER_TOPK, 2 * PEER_HEADS_PER_ITER * tb), F32),
          pltpu.VMEM((PEER_TOPK, PEER_HEADS_PER_ITER * tb), F32),
          pltpu.VMEM((PEER_TOPK, PEER_HEADS_PER_ITER * tb), F32),
          pltpu.VMEM((PEER_SEL, tb), jnp.int32),
          pltpu.VMEM((PEER_SEL, tb), F32),
          pltpu.VMEM((2, tb, 2 * N_KEYS * PEER_HEADS_PER_ITER), F32),
      ],
      compiler_params=_params(("parallel",)),
      name="peer_topk",
  )(xn2, wq, k1, k2)


def _gate_matrix_body(eid_ref, gate_ref, g_ref):
  tb = eid_ref.shape[0]
  sub = lax.broadcasted_iota(jnp.int32, (N_KEYS, PEER_SEL), 0)

  def token_group(gi, carry):
    tiles = []
    for tau in range(SUBLANES):
      t = gi * SUBLANES + tau
      e = eid_ref[pl.ds(t, 1), :]
      gt = gate_ref[pl.ds(t, 1), :]
      at = jnp.where(sub == lax.shift_right_logical(e, KEY_BITS), gt, 0.0).astype(BF16)
      bt = jnp.where(sub == jnp.bitwise_and(e, N_KEYS - 1), 1.0, 0.0).astype(BF16)
      tiles.append(lax.dot_general(at, bt, NT_DIMS, preferred_element_type=F32))
    blocks = jnp.stack([jnp.stack([tile[h * SUBLANES:(h + 1) * SUBLANES, :] for tile in tiles])
                        for h in range(N_KEYS // SUBLANES)])
    g_ref[gi] = jnp.swapaxes(blocks, 1, 2).reshape(N_KEYS, SUBLANES, N_KEYS)
    return carry

  lax.fori_loop(0, tb // SUBLANES, token_group, 0, unroll=8)


def _gate_matrix(eid_t, gate_t):
  t = eid_t.shape[0]
  tb = _pick_block(t, 64, SUBLANES)
  return pl.pallas_call(
      _gate_matrix_body,
      grid=(t // tb,),
      in_specs=[pl.BlockSpec((tb, PEER_SEL), lambda i: (i, 0)), pl.BlockSpec((tb, PEER_SEL), lambda i: (i, 0))],
      out_specs=pl.BlockSpec((tb // SUBLANES, N_KEYS, SUBLANES, N_KEYS), lambda i: (i, 0, 0, 0)),
      out_shape=jax.ShapeDtypeStruct((t // SUBLANES, N_KEYS, SUBLANES, N_KEYS), F32),
      compiler_params=_params(("parallel",)),
      name="peer_gate_matrix",
  )(eid_t, gate_t)


PEER_I1_PER_STEP = 8


def _peer_dense_body(x_ref, u_ref, v_ref, g_ref, h_ref, fg_ref, y_ref, ys_ref, *, rem):
  c = pl.program_id(1)
  s = lax.dot_general(x_ref[...], u_ref[...], NT_DIMS, preferred_element_type=F32)
  act = 0.5 * s * (1.0 + lax.erf(s * (2.0 ** -0.5)))
  tm = x_ref.shape[0]
  coef = jnp.concatenate(
      [g_ref[:, j].reshape(tm, N_KEYS) * act[:, j * N_KEYS:(j + 1) * N_KEYS] for j in range(PEER_I1_PER_STEP)],
      axis=1)
  contrib = jnp.dot(coef.astype(BF16), v_ref[...], preferred_element_type=F32)

  @pl.when(c == 0)
  def _():
    y_ref[...] = contrib

  @pl.when(c > 0)
  def _():
    y_ref[...] += contrib

  @pl.when(c == pl.num_programs(1) - 1)
  def _():
    y = _rms(h_ref[...] + y_ref[...], fg_ref[...])
    y_ref[...] = y

    @pl.when(pl.program_id(0) == pl.num_programs(0) - 1)
    def _():
      ys_ref[...] = y[rem:, :]


def _peer_dense(xn2, u_bf, v_bf, gmat, h, final_g, tp):
  t = xn2.shape[0]
  ts = t - tp
  tm = _pick_block(t, 640, LANES)
  te = PEER_I1_PER_STEP * N_KEYS
  return pl.pallas_call(
      functools.partial(_peer_dense_body, rem=_tail_rows(tp, ts, tm)),
      grid=(t // tm, N_EXPERTS // te),
      in_specs=[
          pl.BlockSpec((tm, D_MODEL), lambda i, c: (i, 0)),
          pl.BlockSpec((te, D_MODEL), lambda i, c: (c, 0)),
          pl.BlockSpec((te, D_MODEL), lambda i, c: (c, 0)),
          pl.BlockSpec((tm // SUBLANES, PEER_I1_PER_STEP, SUBLANES, N_KEYS), lambda i, c: (i, c, 0, 0)),
          pl.BlockSpec((tm, D_MODEL), lambda i, c: (i, 0), pipeline_mode=pl.Buffered(1)),
          pl.BlockSpec((1, D_MODEL), lambda i, c: (0, 0)),
      ],
      out_specs=[_prompt_rows_spec(tm, D_MODEL, tp), _sample_rows_spec(ts, D_MODEL)],
      out_shape=[jax.ShapeDtypeStruct((tp, D_MODEL), F32), jax.ShapeDtypeStruct((ts, D_MODEL), F32)],
      compiler_params=_params(("arbitrary", "arbitrary")),
      name="peer_dense",
  )(xn2, u_bf, v_bf, gmat, h, final_g)


def _rope_tables(pos):
  half = ROT_DIM // 2
  inv_freq = jnp.float32(ROPE_THETA) ** (-jnp.arange(half, dtype=F32) * 2.0 / ROT_DIM)
  ang = pos.astype(F32)[:, None] * inv_freq[None, :]
  cos, sin = jnp.cos(ang), jnp.sin(ang)
  n = pos.shape[0]
  ones = jnp.ones((n, HEAD_DIM - ROT_DIM), F32)
  zeros = jnp.zeros((n, HEAD_DIM - ROT_DIM), F32)
  zh = jnp.zeros((n, half), F32)
  cos_c = jnp.concatenate([cos, cos, ones], axis=1)
  sa_c = jnp.concatenate([-sin, zh, zeros], axis=1)
  sb_c = jnp.concatenate([zh, sin, zeros], axis=1)
  two = lambda a: jnp.concatenate([a, a], axis=1)
  return two(cos_c), two(sa_c), two(sb_c)


def kernel(x_prompt, x_sample, cache_k, cache_v, state_conv, state_h, page_table, norm1_g, w_in, lambda_q1, lambda_k1, lambda_q2, lambda_k2, subln_g, conv_w, conv_b, lru_wa, lru_ba, lru_wx, lru_bx, lru_lambda, w_att_up, w_lru_up, w_out, norm2_g, peer_wq, peer_k1, peer_k2, peer_u, peer_v, final_g):
  batch, seq, _ = x_prompt.shape
  dec_batch, dec_seq, _ = x_sample.shape
  n_pages = page_table.shape[1]
  past_len = n_pages * PAGE_SIZE
  tp = batch * seq
  ts = dec_batch * dec_seq
  assert w_in.shape[0] == 1, "one layer"

  x_p = x_prompt.reshape(tp, D_MODEL)
  x_s = x_sample.reshape(ts, D_MODEL)
  tabs_p = _rope_tables(jnp.arange(seq, dtype=jnp.int32))
  tabs_s = _rope_tables(past_len + jnp.arange(dec_seq, dtype=jnp.int32))
  cos_t, sa_t, sb_t = [jnp.concatenate([jnp.tile(a, (batch, 1)), jnp.tile(b, (dec_batch, 1))], axis=0)
                       for a, b in zip(tabs_p, tabs_s)]
  lam = (jnp.exp(jnp.sum(lambda_q1[0].astype(F32) * lambda_k1[0].astype(F32)))
         - jnp.exp(jnp.sum(lambda_q2[0].astype(F32) * lambda_k2[0].astype(F32))) + LAM_INIT).reshape(1)
  row = lambda a: a.reshape(1, -1)

  proj = _inproj(x_p, x_s, row(norm1_g[0]), w_in[0].astype(BF16), cos_t, sa_t, sb_t)

  att_p, u_bf, v_bf = _prompt_attention(proj, lam, row(subln_g[0]), batch, seq, peer_u[0], peer_v[0])
  proj_s = proj[tp:]
  q_s = proj_s[:, :OFF_Q].reshape(dec_batch, dec_seq, N_KV_HEADS, 2, 2, HEAD_DIM)
  q_s = q_s.transpose(0, 2, 4, 3, 1, 5).reshape(dec_batch, N_KV_HEADS, 2, 2 * dec_seq, HEAD_DIM)
  wq = jnp.einsum("bgcnd,ce->bgcned", q_s, jnp.eye(2, dtype=F32))
  wq = wq.reshape(dec_batch, N_KV_HEADS * 2 * 2 * dec_seq, HEAD_W).astype(BF16)
  new_rows = lambda a: a.reshape(dec_batch, dec_seq * N_KV_HEADS, HEAD_W).astype(BF16)
  k_new, v_new = proj_s[:, OFF_Q:OFF_K], proj_s[:, OFF_K:OFF_V]
  n_pool = cache_k.shape[1]
  att_s = _sample_attention(page_table, lam, wq, new_rows(k_new), new_rows(v_new), row(subln_g[0]),
                            cache_k.reshape(n_pool, PAGE_ROWS, HEAD_W),
                            cache_v.reshape(n_pool, PAGE_ROWS, HEAD_W), dec_seq)
  att_s = att_s.reshape(dec_batch, N_KV_HEADS, 2, dec_seq, HEAD_W).transpose(0, 3, 1, 2, 4)
  att_s = att_s.reshape(ts, ATT_WIDTH).astype(BF16)

  lru_w = (conv_w[0], row(conv_b[0]), lru_wa[0], row(lru_ba[0]), lru_wx[0], row(lru_bx[0]), row(lru_lambda[0]))
  lru_p, h_p = _lru_prompt(proj, lru_w, batch, seq)
  xl_s = proj_s[:, OFF_V:OFF_L].reshape(dec_batch, dec_seq, LRU_WIDTH)
  lru_s, h_s = _lru_sample(xl_s.transpose(1, 0, 2), state_conv[0].transpose(1, 0, 2), state_h[0], lru_w)
  lru_s = lru_s.transpose(1, 0, 2).reshape(ts, LRU_WIDTH)

  h_all, xn2 = _merge(att_p, att_s, lru_p, lru_s, proj, x_p, x_s, w_att_up[0].astype(BF16),
                      w_lru_up[0].astype(BF16), w_out[0].astype(BF16), row(norm2_g[0]))
  eid_t, gate_t = _peer_topk(xn2, peer_wq[0].astype(BF16), peer_k1[0], peer_k2[0])
  gmat = _gate_matrix(eid_t, gate_t)
  y_p, y_s = _peer_dense(xn2, u_bf, v_bf, gmat, h_all, row(final_g), tp)

  kv_shape_p = (1, batch, seq, N_KV_HEADS, HEAD_W)
  kv_shape_s = (1, dec_batch, dec_seq, N_KV_HEADS, HEAD_W)
  tail = CONV_W - 1
  conv_p = jnp.stack([proj[(b + 1) * seq - tail:(b + 1) * seq, OFF_V:OFF_L] for b in range(batch)])
  conv_s = jnp.concatenate([state_conv[0].astype(F32), xl_s], axis=1)[:, -tail:]
  return (
      y_p.reshape(batch, seq, D_MODEL),
      y_s.reshape(dec_batch, dec_seq, D_MODEL),
      proj[:tp, OFF_Q:OFF_K].reshape(kv_shape_p),
      proj[:tp, OFF_K:OFF_V].reshape(kv_shape_p),
      conv_p[None],
      h_p.reshape(1, batch, LRU_WIDTH),
      k_new.reshape(kv_shape_s),
      v_new.reshape(kv_shape_s),
      conv_s[None],
      h_s[None],
  )
```
